```python
import math
import jax, jax.numpy as jnp
from jax import lax
import numpy as np

D_MODEL = 1024
BATCH = 16
SEQ = 2048
DEPTH = 1

CHUNK = 64
Q_BLOCK = 128
MIX_WIDTH = D_MODEL
DIFF_WIDTH = MIX_WIDTH // 2
HGRN_WIDTH = MIX_WIDTH - DIFF_WIDTH
DIFF_HEAD_DIM = 64
DIFF_HEADS = DIFF_WIDTH // (2 * DIFF_HEAD_DIM)
DIFF_V_DIM = 2 * DIFF_HEAD_DIM
HGRN_EXPAND = 128
HGRN_HEADS = HGRN_WIDTH // HGRN_EXPAND
HGRN_DK = HGRN_EXPAND
HGRN_DV = HGRN_WIDTH // HGRN_HEADS
D_FF = ((8 * D_MODEL // 3 + 127) // 128) * 128
CONV_WIDTH = 3
EPS = 1e-6
IN_COLS = 3 * DIFF_WIDTH + 4 * HGRN_WIDTH

kernel_name = "hymba_diffattn_hgrn2_convffn"


def _rmsnorm(x, w):
    xf = x.astype(jnp.float32)
    y = xf * lax.rsqrt(jnp.mean(xf * xf, axis=-1, keepdims=True) + EPS)
    return (y * w.astype(jnp.float32)).astype(x.dtype)


def _lambda_init(layer):
    return 0.8 - 0.6 * math.exp(-0.3 * layer)


def _diff_attention(q, k, v, lam, lam_init, subln_w):
    B, S = q.shape[:2]
    nb = S // Q_BLOCK
    scale = DIFF_HEAD_DIM ** -0.5
    k_chunk = jnp.arange(S) // CHUNK
    q_blocks = jnp.moveaxis(q.reshape(B, nb, Q_BLOCK, DIFF_HEADS, 2, DIFF_HEAD_DIM), 1, 0)

    def block(args):
        qb, bi = args
        q_chunk = (bi * Q_BLOCK + jnp.arange(Q_BLOCK)) // CHUNK
        mask = k_chunk[None, :] <= q_chunk[:, None]
        s = jnp.einsum('bqhcd,bkhcd->bhcqk', qb, k).astype(jnp.float32) * scale
        p = jax.nn.softmax(jnp.where(mask, s, -jnp.inf), axis=-1)
        a = p[:, :, 0] - lam * p[:, :, 1]
        return jnp.einsum('bhqk,bkhe->bqhe', a.astype(v.dtype), v)

    o = lax.map(block, (q_blocks, jnp.arange(nb)))
    o = jnp.moveaxis(o, 0, 1).reshape(B, S, DIFF_HEADS, DIFF_V_DIM)
    o = _rmsnorm(o, subln_w) * (1.0 - lam_init)
    return o.reshape(B, S, DIFF_WIDTH)


def _hgrn2(q, f_logit, i, gate, lb, norm_w):
    B, S = q.shape[:2]
    nc = S // CHUNK
    f32 = jnp.float32
    f = lb + (1.0 - lb) * jax.nn.sigmoid(f_logit.astype(f32))
    log_f = jnp.log(f)
    k = 1.0 - f

    def to_chunks(t):
        return t.reshape(B, nc, CHUNK, t.shape[2], t.shape[3]).transpose(1, 0, 3, 2, 4)

    causal = jnp.tril(jnp.ones((CHUNK, CHUNK), bool))

    def step(state, xs):
        qc, kc, vc, gc = xs
        b = jnp.cumsum(gc, axis=-2)
        rel = jnp.where(causal[:, :, None],
                        b[:, :, :, None, :] - b[:, :, None, :, :], -jnp.inf)
        att = jnp.einsum('bhtk,bhsk,bhtsk->bhts', qc, kc, jnp.exp(rel))
        o = (jnp.einsum('bhts,bhsv->bhtv', att, vc)
             + jnp.einsum('bhtk,bhkv->bhtv', qc * jnp.exp(b), state))
        b_last = b[:, :, -1:, :]
        state = (jnp.exp(b_last[:, :, 0, :])[..., None] * state
                 + jnp.einsum('bhsk,bhsv->bhkv', kc * jnp.exp(b_last - b), vc))
        return state, o

    state0 = jnp.zeros((B, HGRN_HEADS, HGRN_DK, HGRN_DV), f32)
    _, o = lax.scan(step, state0, (to_chunks(q.astype(f32)), to_chunks(k),
                                   to_chunks(i.astype(f32)), to_chunks(log_f)))
    o = o.transpose(1, 0, 3, 2, 4).reshape(B, S, HGRN_HEADS, HGRN_DV)
    o = _rmsnorm(o, norm_w).reshape(B, S, HGRN_WIDTH)
    return (o * jax.nn.silu(gate.astype(f32))).astype(gate.dtype)


def _conv_ffn(x, w_up, conv_w, conv_b, w_down):
    S = x.shape[1]
    u, v = jnp.split(x @ w_up, 2, axis=-1)
    u_pad = jnp.pad(u, ((0, 0), (CONV_WIDTH - 1, 0), (0, 0)))
    c = conv_b
    for j in range(CONV_WIDTH):
        c = c + u_pad[:, j:j + S] * conv_w[j]
    return (jax.nn.silu(c) * v) @ w_down


def setup_inputs(seed: int = 0) -> dict:
    key = jax.random.key(seed)
    ks = jax.random.split(key, 20)
    n = jax.random.normal
    f32 = jnp.float32

    def gain(k, shape):
        return 1.0 + 0.02 * n(k, shape, f32)

    return {
        "x": n(ks[0], (BATCH, SEQ, D_MODEL), f32),
        "ln1_w": gain(ks[1], (DEPTH, D_MODEL)),
        "w_in": n(ks[2], (DEPTH, D_MODEL, IN_COLS), f32) * D_MODEL ** -0.5,
        "q_norm_w": gain(ks[3], (DEPTH, DIFF_HEAD_DIM)),
        "k_norm_w": gain(ks[4], (DEPTH, DIFF_HEAD_DIM)),
        "lam_q1": 0.1 * n(ks[5], (DEPTH, DIFF_HEAD_DIM), f32),
        "lam_k1": 0.1 * n(ks[6], (DEPTH, DIFF_HEAD_DIM), f32),
        "lam_q2": 0.1 * n(ks[7], (DEPTH, DIFF_HEAD_DIM), f32),
        "lam_k2": 0.1 * n(ks[8], (DEPTH, DIFF_HEAD_DIM), f32),
        "diff_subln_w": gain(ks[9], (DEPTH, DIFF_V_DIM)),
        "hgrn_lb_logits": 0.1 * n(ks[10], (DEPTH + 1, HGRN_WIDTH), f32),
        "hgrn_norm_w": gain(ks[11], (DEPTH, HGRN_DV)),
        "w_out": n(ks[12], (DEPTH, MIX_WIDTH, D_MODEL), f32) * MIX_WIDTH ** -0.5,
        "ln2_w": gain(ks[13], (DEPTH, D_MODEL)),
        "w_up": n(ks[14], (DEPTH, D_MODEL, 2 * D_FF), f32) * D_MODEL ** -0.5,
        "conv_w": n(ks[15], (DEPTH, CONV_WIDTH, D_FF), f32) * CONV_WIDTH ** -0.5,
        "conv_b": 0.02 * n(ks[16], (DEPTH, D_FF), f32),
        "w_down": n(ks[17], (DEPTH, D_FF, D_MODEL), f32) * D_FF ** -0.5,
    }


def reference(x, ln1_w, w_in, q_norm_w, k_norm_w, lam_q1, lam_k1, lam_q2, lam_k2,
              diff_subln_w, hgrn_lb_logits, hgrn_norm_w, w_out, ln2_w, w_up, conv_w,
              conv_b, w_down):
    B, S, _ = x.shape
    lb_all = jnp.cumsum(jax.nn.softmax(hgrn_lb_logits.astype(jnp.float32), axis=0), axis=0)
    offs = [DIFF_WIDTH, 2 * DIFF_WIDTH, 3 * DIFF_WIDTH,
            3 * DIFF_WIDTH + HGRN_WIDTH, 3 * DIFF_WIDTH + 2 * HGRN_WIDTH,
            3 * DIFF_WIDTH + 3 * HGRN_WIDTH]
    for l in range(DEPTH):
        h = _rmsnorm(x, ln1_w[l])
        proj = h @ w_in[l]
        dq, dk, dv, hq, hf, hi, hg = jnp.split(proj, offs, axis=-1)
        dq = _rmsnorm(dq.reshape(B, S, DIFF_HEADS, 2, DIFF_HEAD_DIM), q_norm_w[l])
        dk = _rmsnorm(dk.reshape(B, S, DIFF_HEADS, 2, DIFF_HEAD_DIM), k_norm_w[l])
        dv = dv.reshape(B, S, DIFF_HEADS, DIFF_V_DIM)
        lam_init = _lambda_init(l)
        lam = (jnp.exp(jnp.sum(lam_q1[l] * lam_k1[l]).astype(jnp.float32))
               - jnp.exp(jnp.sum(lam_q2[l] * lam_k2[l]).astype(jnp.float32)) + lam_init)
        o_diff = _diff_attention(dq, dk, dv, lam, lam_init, diff_subln_w[l])
        lb = lb_all[l].reshape(HGRN_HEADS, HGRN_DK)
        o_hgrn = _hgrn2(hq.reshape(B, S, HGRN_HEADS, HGRN_DK),
                        hf.reshape(B, S, HGRN_HEADS, HGRN_DK),
                        hi.reshape(B, S, HGRN_HEADS, HGRN_DV),
                        hg, lb, hgrn_norm_w[l])
        x = x + jnp.concatenate([o_diff, o_hgrn], axis=-1) @ w_out[l]
        x = x + _conv_ffn(_rmsnorm(x, ln2_w[l]), w_up[l], conv_w[l], conv_b[l], w_down[l])
    return x
```

```python
import functools
import math

import numpy as np
import jax
import jax.numpy as jnp
from jax import lax
from jax.experimental import pallas as pl
from jax.experimental.pallas import tpu as pltpu

F32 = jnp.float32
BF16 = jnp.bfloat16

D_MODEL = 1024
CHUNK = 64
HEAD_W = 128
DIFF_HEAD_DIM = 64
N_HEADS = 4
SEC_W = N_HEADS * HEAD_W
N_SEC = 7
IN_COLS = N_SEC * SEC_W
D_FF = 2816
CONV_WIDTH = 3
EPS = 1e-6
LAM_INIT = 0.8 - 0.6 * math.exp(-0.3 * 0)

SEC_DQ, SEC_DK, SEC_DV, SEC_HQ, SEC_HK, SEC_HI, SEC_HG = range(N_SEC)

LANES = 128
SUBLANES = 8
TM_PROJ = 512
TQ = 256
TK = 256
TM_FFN = 512
FF_CHUNK = 512
VMEM_LIMIT = 56 * 1024 * 1024

N_LEVELS = 6


def _nt_dot(a, b):
    return lax.dot_general(a, b, (((1,), (1,)), ((), ())), preferred_element_type=F32)


def _tn_dot(a, b):
    return lax.dot_general(a, b, (((0,), (0,)), ((), ())), preferred_element_type=F32)


def _dot(a, b):
    return jnp.dot(a, b, preferred_element_type=F32)


def _split_bf16(x):
    hi = x.astype(BF16)
    lo = (x - hi.astype(F32)).astype(BF16)
    return hi, lo


def _sigmoid(x):
    return 1.0 / (1.0 + jnp.exp(-x))


def _in_proj_kernel(x_ref, ln1_ref, w_ref, qw_ref, kw_ref, lbl_ref, proj_ref, g_ref):
    x = x_ref[...]
    ms = jnp.mean(x * x, axis=-1, keepdims=True)
    h = (x * lax.rsqrt(ms + EPS) * ln1_ref[...]).astype(BF16)

    r = lax.broadcasted_iota(jnp.int32, (LANES, LANES), 0) // DIFF_HEAD_DIM
    c = lax.broadcasted_iota(jnp.int32, (LANES, LANES), 1) // DIFF_HEAD_DIM
    grp = jnp.where(r == c, 1.0, 0.0).astype(BF16)

    def section(j):
        return _dot(h, w_ref[:, j * SEC_W:(j + 1) * SEC_W])

    def head_cols(a, hd):
        return a[:, hd * HEAD_W:(hd + 1) * HEAD_W]

    for sec, w_norm in ((SEC_DQ, qw_ref[...] * DIFF_HEAD_DIM ** -0.5), (SEC_DK, kw_ref[...])):
        acc = section(sec)
        for hd in range(N_HEADS):
            a = head_cols(acc, hd)
            hi, lo = _split_bf16(a * a)
            ss = _dot(hi, grp) + _dot(lo, grp)
            y = a * lax.rsqrt(ss * (1.0 / DIFF_HEAD_DIM) + EPS) * w_norm
            proj_ref[sec * N_HEADS + hd] = y.astype(BF16)

    for sec in (SEC_DV, SEC_HQ, SEC_HI):
        acc = section(sec)
        for hd in range(N_HEADS):
            proj_ref[sec * N_HEADS + hd] = head_cols(acc, hd).astype(BF16)

    lbl = lbl_ref[...]
    e = jnp.exp(lbl - jnp.max(lbl, axis=0, keepdims=True))
    lb = e[0:1] / jnp.sum(e, axis=0, keepdims=True)
    acc = section(SEC_HK)
    f = lb + (1.0 - lb) * _sigmoid(acc)
    log_f = jnp.log(f)
    key = 1.0 - f
    for hd in range(N_HEADS):
        g_ref[hd] = head_cols(log_f, hd)
        proj_ref[SEC_HK * N_HEADS + hd] = head_cols(key, hd).astype(BF16)

    acc = section(SEC_HG)
    gate = acc * _sigmoid(acc)
    for hd in range(N_HEADS):
        proj_ref[SEC_HG * N_HEADS + hd] = head_cols(gate, hd).astype(BF16)


def _in_proj(x2d, ln1_w, w_in, q_norm_w, k_norm_w, lb_logits):
    t = x2d.shape[0]
    const = lambda *_: (0, 0)
    return pl.pallas_call(
        _in_proj_kernel,
        grid=(t // TM_PROJ,),
        in_specs=[
            pl.BlockSpec((TM_PROJ, D_MODEL), lambda i: (i, 0)),
            pl.BlockSpec((1, D_MODEL), const),
            pl.BlockSpec((D_MODEL, IN_COLS), const, pipeline_mode=pl.Buffered(1)),
            pl.BlockSpec((1, HEAD_W), const),
            pl.BlockSpec((1, HEAD_W), const),
            pl.BlockSpec(lb_logits.shape, const),
        ],
        out_specs=[
            pl.BlockSpec((N_SEC * N_HEADS, TM_PROJ, HEAD_W), lambda i: (0, i, 0)),
            pl.BlockSpec((N_HEADS, TM_PROJ, HEAD_W), lambda i: (0, i, 0)),
        ],
        out_shape=[
            jax.ShapeDtypeStruct((N_SEC * N_HEADS, t, HEAD_W), BF16),
            jax.ShapeDtypeStruct((N_HEADS, t, HEAD_W), F32),
        ],
        compiler_params=pltpu.CompilerParams(
            dimension_semantics=("parallel",), vmem_limit_bytes=VMEM_LIMIT),
        name="in_proj",
    )(x2d, ln1_w, w_in, q_norm_w, k_norm_w, lb_logits)


NEG_BIG = -1e30


def _diff_attn_kernel(q_ref, k_ref, v_ref, lq1_ref, lk1_ref, lq2_ref, lk2_ref, sw_ref,
                      o_ref, m_ref, l_ref, acc_ref):
    i = pl.program_id(2)
    q = q_ref[0, 0]
    lane = lax.broadcasted_iota(jnp.int32, (TQ, HEAD_W), 1)
    zero = jnp.zeros_like(q)
    q_halves = (jnp.where(lane < DIFF_HEAD_DIM, q, zero), jnp.where(lane >= DIFF_HEAD_DIM, q, zero))

    m_ref[...] = jnp.full(m_ref.shape, NEG_BIG, F32)
    l_ref[...] = jnp.zeros(l_ref.shape, F32)
    acc_ref[...] = jnp.zeros(acc_ref.shape, F32)

    rq = lax.broadcasted_iota(jnp.int32, (TQ, TK), 0) // CHUNK
    ck = lax.broadcasted_iota(jnp.int32, (TQ, TK), 1) // CHUNK
    diag_mask = ck <= rq

    def kv_step(j, masked):
        start = pl.multiple_of(j * TK, TK)
        k = k_ref[0, 0, pl.ds(start, TK), :]
        v = v_ref[0, 0, pl.ds(start, TK), :]
        for c in range(2):
            s = _nt_dot(q_halves[c], k)
            if masked:
                s = jnp.where(diag_mask, s, NEG_BIG)
            m_old = m_ref[c]
            m_new = jnp.maximum(m_old, jnp.max(s, axis=-1, keepdims=True))
            alpha = jnp.exp(m_old - m_new)
            p = jnp.exp(s - pltpu.repeat(m_new, TK // LANES, axis=1))
            l_ref[c] = alpha * l_ref[c] + jnp.sum(p, axis=-1, keepdims=True)
            acc_ref[c] = alpha * acc_ref[c] + _dot(p.astype(BF16), v)
            m_ref[c] = m_new

    def off_diag(j, carry):
        kv_step(j, masked=False)
        return carry

    lax.fori_loop(0, i, off_diag, 0)
    kv_step(i, masked=True)

    lam = (jnp.exp(jnp.sum(lq1_ref[...] * lk1_ref[...], axis=-1, keepdims=True))
           - jnp.exp(jnp.sum(lq2_ref[...] * lk2_ref[...], axis=-1, keepdims=True)) + LAM_INIT)
    o = acc_ref[0] / l_ref[0] - lam * (acc_ref[1] / l_ref[1])
    ms = jnp.mean(o * o, axis=-1, keepdims=True)
    y = o * lax.rsqrt(ms + EPS) * (sw_ref[...] * (1.0 - LAM_INIT))
    o_ref[0, 0] = y.astype(BF16)


def _diff_attn(proj4, lam_q1, lam_k1, lam_q2, lam_k2, subln_w):
    _, b, s, _ = proj4.shape
    const = lambda *_: (0, 0)
    lam_spec = pl.BlockSpec((1, DIFF_HEAD_DIM), const)
    return pl.pallas_call(
        _diff_attn_kernel,
        grid=(b, N_HEADS, s // TQ),
        in_specs=[
            pl.BlockSpec((1, 1, TQ, HEAD_W), lambda bi, hi, qi: (SEC_DQ * N_HEADS + hi, bi, qi, 0)),
            pl.BlockSpec((1, 1, s, HEAD_W), lambda bi, hi, qi: (SEC_DK * N_HEADS + hi, bi, 0, 0)),
            pl.BlockSpec((1, 1, s, HEAD_W), lambda bi, hi, qi: (SEC_DV * N_HEADS + hi, bi, 0, 0)),
            lam_spec, lam_spec, lam_spec, lam_spec,
            pl.BlockSpec((1, HEAD_W), const),
        ],
        out_specs=pl.BlockSpec((1, 1, TQ, HEAD_W), lambda bi, hi, qi: (hi, bi, qi, 0)),
        out_shape=jax.ShapeDtypeStruct((N_HEADS, b, s, HEAD_W), BF16),
        scratch_shapes=[
            pltpu.VMEM((2, TQ, HEAD_W), F32),
            pltpu.VMEM((2, TQ, HEAD_W), F32),
            pltpu.VMEM((2, TQ, HEAD_W), F32),
        ],
        compiler_params=pltpu.CompilerParams(
            dimension_semantics=("parallel", "parallel", "parallel"), vmem_limit_bytes=VMEM_LIMIT),
        name="diff_attn",
    )(proj4, proj4, proj4, lam_q1, lam_k1, lam_q2, lam_k2, subln_w)


def _hgrn_constants():
    n = CHUNK
    t = np.arange(n)[:, None]
    s = np.arange(n)[None, :]
    mats = [(s <= t), (s > t)]
    masks = [(s == t)]
    roles = []
    for lvl in range(N_LEVELS):
        hs = n >> (lvl + 1)
        blk = t // (2 * hs)
        mid = blk * 2 * hs + hs - 1
        is_q = (t % (2 * hs)) >= hs
        w = np.where(is_q, (s > mid) & (s <= t), (s > t) & (s <= mid))
        mats.append(w)
        masks.append((blk == (s // (2 * hs))) & is_q & ((s % (2 * hs)) < hs))
        roles.append(np.broadcast_to(is_q, (n, HEAD_W)))
    wall = np.concatenate(mats, axis=0).astype(np.float32)
    masks = np.stack(masks).astype(np.float32)
    roles = np.stack(roles).astype(np.float32)
    return wall, masks, roles


def _hgrn_kernel(q_ref, k_ref, v_ref, gate_ref, g_ref, wall_ref, mask_ref, role_ref, nw_ref,
                 o_ref, st_ref):
    n_chunks = q_ref.shape[2] // CHUNK
    st_ref[...] = jnp.zeros(st_ref.shape, F32)
    wall = wall_ref[...]
    nw = nw_ref[...]

    def chunk(ci, carry):
        r0 = pl.multiple_of(ci * CHUNK, CHUNK)
        rows = pl.ds(r0, CHUNK)
        q_b = q_ref[0, 0, rows, :]
        k_b = k_ref[0, 0, rows, :]
        v_b = v_ref[0, 0, rows, :]
        q = q_b.astype(F32)
        k = k_b.astype(F32)
        g_hi, g_lo = _split_bf16(g_ref[0, 0, rows, :])
        e = jnp.exp(_dot(wall, g_hi) + _dot(wall, g_lo))
        e_b = e[0:CHUNK]
        e_u = e[CHUNK:2 * CHUNK]

        st = st_ref[...]
        o = _nt_dot((q * e_b).astype(BF16), st.astype(BF16))

        att = mask_ref[0] * _nt_dot(q_b, k_b)
        for lvl in range(N_LEVELS):
            z = e[(2 + lvl) * CHUNK:(3 + lvl) * CHUNK]
            xz = (jnp.where(role_ref[lvl] > 0.5, q, k) * z).astype(BF16)
            att = att + mask_ref[lvl + 1] * _nt_dot(xz, xz)
        o = o + _dot(att.astype(BF16), v_b)

        st_ref[...] = st * e_b[CHUNK - 1:CHUNK, :] + _tn_dot(v_b, (k * e_u).astype(BF16))

        ms = jnp.mean(o * o, axis=-1, keepdims=True)
        y = o * lax.rsqrt(ms + EPS) * nw * gate_ref[0, 0, rows, :].astype(F32)
        o_ref[0, 0, rows, :] = y.astype(BF16)
        return carry

    lax.fori_loop(0, n_chunks, chunk, 0)


def _hgrn2(proj4, g4, norm_w):
    _, b, s, _ = proj4.shape
    wall, masks, roles = _hgrn_constants()
    head_spec = lambda sec: pl.BlockSpec(
        (1, 1, s, HEAD_W), lambda bi, hi: (sec * N_HEADS + hi, bi, 0, 0))
    return pl.pallas_call(
        _hgrn_kernel,
        grid=(b, N_HEADS),
        in_specs=[
            head_spec(SEC_HQ), head_spec(SEC_HK), head_spec(SEC_HI), head_spec(SEC_HG),
            pl.BlockSpec((1, 1, s, HEAD_W), lambda bi, hi: (hi, bi, 0, 0)),
            pl.BlockSpec(wall.shape, lambda *_: (0, 0)),
            pl.BlockSpec(masks.shape, lambda *_: (0, 0, 0)),
            pl.BlockSpec(roles.shape, lambda *_: (0, 0, 0)),
            pl.BlockSpec((1, HEAD_W), lambda *_: (0, 0)),
        ],
        out_specs=pl.BlockSpec((1, 1, s, HEAD_W), lambda bi, hi: (hi, bi, 0, 0)),
        out_shape=jax.ShapeDtypeStruct((N_HEADS, b, s, HEAD_W), BF16),
        scratch_shapes=[pltpu.VMEM((HEAD_W, HEAD_W), F32)],
        compiler_params=pltpu.CompilerParams(
            dimension_semantics=("parallel", "parallel"), vmem_limit_bytes=VMEM_LIMIT),
        name="hgrn2",
    )(proj4, proj4, proj4, proj4, g4, jnp.asarray(wall, BF16), jnp.asarray(masks),
      jnp.asarray(roles), norm_w)


def _ff_chunks():
    chunks, c0 = [], 0
    while c0 < D_FF:
        fc = min(FF_CHUNK, D_FF - c0)
        chunks.append((c0, fc))
        c0 += fc
    return chunks


def _out_ffn_kernel(tiles_per_seq, x_ref, od_ref, oh_ref, wout_ref, ln2_ref, wup_ref, cw_ref,
                    cb_ref, wdn_ref, out_ref, tail_ref, act_ref):
    i = pl.program_id(0)
    mix = jnp.concatenate([od_ref[hd] for hd in range(N_HEADS)]
                          + [oh_ref[hd] for hd in range(N_HEADS)], axis=1)
    x1 = x_ref[...] + _dot(mix, wout_ref[...])
    ms = jnp.mean(x1 * x1, axis=-1, keepdims=True)
    h2 = (x1 * lax.rsqrt(ms + EPS) * ln2_ref[...]).astype(BF16)

    seq_start = (i % tiles_per_seq) == 0
    tm = x1.shape[0]
    for c0, fc in _ff_chunks():
        cols = slice(c0, c0 + fc)
        u = _dot(h2, wup_ref[:, cols])
        v = _dot(h2, wup_ref[:, D_FF + c0:D_FF + c0 + fc])
        tail = jnp.where(seq_start, 0.0, tail_ref[:, cols])
        tail_ref[:, cols] = u[tm - SUBLANES:, :]
        ext = jnp.concatenate([tail, u], axis=0)
        u1 = pltpu.roll(ext, 1, 0)[SUBLANES:]
        u2 = pltpu.roll(ext, 2, 0)[SUBLANES:]
        cw = cw_ref[:, cols]
        c = cb_ref[:, cols] + u2 * cw[0:1] + u1 * cw[1:2] + u * cw[2:3]
        act_ref[:, cols] = (c * _sigmoid(c) * v).astype(BF16)
    out_ref[...] = x1 + _dot(act_ref[...], wdn_ref[...])


def _out_ffn(x2d, o_diff, o_hgrn, w_out, ln2_w, w_up, conv_w, conv_b, w_down, seq_len):
    t = x2d.shape[0]
    const = lambda *_: (0, 0)
    resident = functools.partial(pl.BlockSpec, index_map=const, pipeline_mode=pl.Buffered(1))
    return pl.pallas_call(
        functools.partial(_out_ffn_kernel, seq_len // TM_FFN),
        grid=(t // TM_FFN,),
        in_specs=[
            pl.BlockSpec((TM_FFN, D_MODEL), lambda i: (i, 0)),
            pl.BlockSpec((N_HEADS, TM_FFN, HEAD_W), lambda i: (0, i, 0)),
            pl.BlockSpec((N_HEADS, TM_FFN, HEAD_W), lambda i: (0, i, 0)),
            resident((D_MODEL, D_MODEL)),
            pl.BlockSpec((1, D_MODEL), const),
            resident((D_MODEL, 2 * D_FF)),
            pl.BlockSpec((CONV_WIDTH, D_FF), const),
            pl.BlockSpec((1, D_FF), const),
            resident((D_FF, D_MODEL)),
        ],
        out_specs=pl.BlockSpec((TM_FFN, D_MODEL), lambda i: (i, 0)),
        out_shape=jax.ShapeDtypeStruct((t, D_MODEL), F32),
        scratch_shapes=[
            pltpu.VMEM((SUBLANES, D_FF), F32),
            pltpu.VMEM((TM_FFN, D_FF), BF16),
        ],
        compiler_params=pltpu.CompilerParams(
            dimension_semantics=("arbitrary",), vmem_limit_bytes=VMEM_LIMIT),
        name="out_ffn",
    )(x2d, o_diff, o_hgrn, w_out, ln2_w, w_up, conv_w, conv_b, w_down)


def kernel(x, ln1_w, w_in, q_norm_w, k_norm_w, lam_q1, lam_k1, lam_q2, lam_k2, diff_subln_w,
           hgrn_lb_logits, hgrn_norm_w, w_out, ln2_w, w_up, conv_w, conv_b, w_down):
    b, s, d = x.shape
    depth = ln1_w.shape[0]
    assert depth == 1 and d == D_MODEL and s % TM_FFN == 0 and s % TQ == 0
    t = b * s
    x2d = x.reshape(t, d)
    l = 0
    tile2 = lambda w: jnp.concatenate([w, w], axis=-1)[None, :]

    proj, g = _in_proj(x2d, ln1_w[l][None, :], w_in[l].astype(BF16), tile2(q_norm_w[l]),
                       tile2(k_norm_w[l]), hgrn_lb_logits)
    proj4 = proj.reshape(N_SEC * N_HEADS, b, s, HEAD_W)
    g4 = g.reshape(N_HEADS, b, s, HEAD_W)

    o_diff = _diff_attn(proj4, lam_q1[l][None, :], lam_k1[l][None, :], lam_q2[l][None, :],
                        lam_k2[l][None, :], diff_subln_w[l][None, :])
    o_hgrn = _hgrn2(proj4, g4, hgrn_norm_w[l][None, :])

    out = _out_ffn(x2d, o_diff.reshape(N_HEADS, t, HEAD_W), o_hgrn.reshape(N_HEADS, t, HEAD_W),
                   w_out[l].astype(BF16), ln2_w[l][None, :], w_up[l].astype(BF16), conv_w[l],
                   conv_b[l][None, :], w_down[l].astype(BF16), s)
    return out.reshape(b, s, d)
```

```python
import functools
import math

import numpy as np
import jax
import jax.numpy as jnp
from jax import lax
from jax.experimental import pallas as pl
from jax.experimental.pallas import tpu as pltpu

F32 = jnp.float32
BF16 = jnp.bfloat16

D_MODEL = 1024
CHUNK = 64
HEAD_W = 128
DIFF_HEAD_DIM = 64
N_HEADS = 4
SEC_W = N_HEADS * HEAD_W
N_SEC = 7
IN_COLS = N_SEC * SEC_W
D_FF = 2816
CONV_WIDTH = 3
EPS = 1e-6
LAM_INIT = 0.8 - 0.6 * math.exp(-0.3 * 0)
LOG2E = math.log2(math.e)

SEC_DQ, SEC_DK, SEC_DV, SEC_HQ, SEC_HK, SEC_HI, SEC_HG = range(N_SEC)

LANES = 128
SUBLANES = 8
TM_PROJ = 512
TQ = 256
TK = 256
TM_FFN = 512
FF_CHUNK = 512
VMEM_LIMIT = 56 * 1024 * 1024

N_LEVELS = 6


def _nt_dot(a, b):
    return lax.dot_general(a, b, (((1,), (1,)), ((), ())), preferred_element_type=F32)


def _tn_dot(a, b):
    return lax.dot_general(a, b, (((0,), (0,)), ((), ())), preferred_element_type=F32)


def _dot(a, b):
    return jnp.dot(a, b, preferred_element_type=F32)


def _split_bf16(x):
    hi = x.astype(BF16)
    lo = (x - hi.astype(F32)).astype(BF16)
    return hi, lo


def _sigmoid(x):
    return 1.0 / (1.0 + jnp.exp(-x))


def _in_proj_kernel(x_ref, ln1_ref, w_ref, qw_ref, kw_ref, lbl_ref, proj_ref, g_ref):
    x = x_ref[...]
    ms = jnp.mean(x * x, axis=-1, keepdims=True)
    h = (x * lax.rsqrt(ms + EPS) * ln1_ref[...]).astype(BF16)

    r = lax.broadcasted_iota(jnp.int32, (LANES, LANES), 0) // DIFF_HEAD_DIM
    c = lax.broadcasted_iota(jnp.int32, (LANES, LANES), 1) // DIFF_HEAD_DIM
    grp = jnp.where(r == c, 1.0, 0.0).astype(BF16)

    def section(j):
        return _dot(h, w_ref[:, j * SEC_W:(j + 1) * SEC_W])

    def head_cols(a, hd):
        return a[:, hd * HEAD_W:(hd + 1) * HEAD_W]

    q_gain = qw_ref[...] * (DIFF_HEAD_DIM ** -0.5 * LOG2E)
    for sec, w_norm in ((SEC_DQ, q_gain), (SEC_DK, kw_ref[...])):
        acc = section(sec)
        for hd in range(N_HEADS):
            a = head_cols(acc, hd)
            hi, lo = _split_bf16(a * a)
            ss = _dot(hi, grp) + _dot(lo, grp)
            y = a * lax.rsqrt(ss * (1.0 / DIFF_HEAD_DIM) + EPS) * w_norm
            proj_ref[sec * N_HEADS + hd] = y.astype(BF16)

    for sec in (SEC_DV, SEC_HQ, SEC_HI):
        acc = section(sec)
        for hd in range(N_HEADS):
            proj_ref[sec * N_HEADS + hd] = head_cols(acc, hd).astype(BF16)

    lbl = lbl_ref[...]
    e = jnp.exp(lbl - jnp.max(lbl, axis=0, keepdims=True))
    lb = e[0:1] / jnp.sum(e, axis=0, keepdims=True)
    acc = section(SEC_HK)
    f = lb + (1.0 - lb) * _sigmoid(acc)
    log_f = jnp.log(f)
    key = 1.0 - f
    for hd in range(N_HEADS):
        g_ref[hd] = head_cols(log_f, hd)
        proj_ref[SEC_HK * N_HEADS + hd] = head_cols(key, hd).astype(BF16)

    acc = section(SEC_HG)
    gate = acc * _sigmoid(acc)
    for hd in range(N_HEADS):
        proj_ref[SEC_HG * N_HEADS + hd] = head_cols(gate, hd).astype(BF16)


def _in_proj(x2d, ln1_w, w_in, q_norm_w, k_norm_w, lb_logits):
    t = x2d.shape[0]
    const = lambda *_: (0, 0)
    return pl.pallas_call(
        _in_proj_kernel,
        grid=(t // TM_PROJ,),
        in_specs=[
            pl.BlockSpec((TM_PROJ, D_MODEL), lambda i: (i, 0)),
            pl.BlockSpec((1, D_MODEL), const),
            pl.BlockSpec((D_MODEL, IN_COLS), const, pipeline_mode=pl.Buffered(1)),
            pl.BlockSpec((1, HEAD_W), const),
            pl.BlockSpec((1, HEAD_W), const),
            pl.BlockSpec(lb_logits.shape, const),
        ],
        out_specs=[
            pl.BlockSpec((N_SEC * N_HEADS, TM_PROJ, HEAD_W), lambda i: (0, i, 0)),
            pl.BlockSpec((N_HEADS, TM_PROJ, HEAD_W), lambda i: (0, i, 0)),
        ],
        out_shape=[
            jax.ShapeDtypeStruct((N_SEC * N_HEADS, t, HEAD_W), BF16),
            jax.ShapeDtypeStruct((N_HEADS, t, HEAD_W), F32),
        ],
        compiler_params=pltpu.CompilerParams(
            dimension_semantics=("parallel",), vmem_limit_bytes=VMEM_LIMIT),
        name="in_proj",
    )(x2d, ln1_w, w_in, q_norm_w, k_norm_w, lb_logits)


NEG_BIG = -1e30


def _diff_attn_kernel(q_ref, k_ref, v_ref, lq1_ref, lk1_ref, lq2_ref, lk2_ref, sw_ref,
                      o_ref, qs_ref, m_ref, l_ref, acc_ref):
    i = pl.program_id(1)
    n_hd = q_ref.shape[0]
    lane = lax.broadcasted_iota(jnp.int32, (TQ, HEAD_W), 1)
    for hd in range(n_hd):
        q = q_ref[hd, 0]
        zero = jnp.zeros_like(q)
        qs_ref[hd, 0:TQ] = jnp.where(lane < DIFF_HEAD_DIM, q, zero)
        qs_ref[hd, TQ:2 * TQ] = jnp.where(lane >= DIFF_HEAD_DIM, q, zero)

    m_ref[...] = jnp.full(m_ref.shape, NEG_BIG, F32)
    l_ref[...] = jnp.zeros(l_ref.shape, F32)
    acc_ref[...] = jnp.zeros(acc_ref.shape, F32)

    rq = (lax.broadcasted_iota(jnp.int32, (2 * TQ, TK), 0) % TQ) // CHUNK
    ck = lax.broadcasted_iota(jnp.int32, (2 * TQ, TK), 1) // CHUNK
    diag_mask = ck <= rq
    lane_groups = [slice(c * LANES, (c + 1) * LANES) for c in range(TK // LANES)]

    def scores(hd, start, masked):
        s = _nt_dot(qs_ref[hd], k_ref[hd, 0, pl.ds(start, TK), :])
        return jnp.where(diag_mask, s, NEG_BIG) if masked else s

    def max_step(j, masked):
        start = pl.multiple_of(j * TK, TK)
        for hd in range(n_hd):
            s = scores(hd, start, masked)
            m = m_ref[hd]
            for grp in lane_groups:
                m = jnp.maximum(m, s[:, grp])
            m_ref[hd] = m

    def acc_step(j, masked):
        start = pl.multiple_of(j * TK, TK)
        for hd in range(n_hd):
            s = scores(hd, start, masked)
            m = m_ref[hd]
            p = [jnp.exp2(s[:, grp] - m) for grp in lane_groups]
            l_ref[hd] += functools.reduce(lambda a, b: a + b, p)
            p = jnp.concatenate(p, axis=1).astype(BF16)
            acc_ref[hd] += _dot(p, v_ref[hd, 0, pl.ds(start, TK), :])

    def loop(step):
        def body(j, carry):
            step(j, masked=False)
            return carry
        lax.fori_loop(0, i, body, 0)
        step(i, masked=True)

    loop(max_step)
    for hd in range(n_hd):
        m_ref[hd] = jnp.broadcast_to(jnp.max(m_ref[hd], axis=-1, keepdims=True), (2 * TQ, HEAD_W))
    loop(acc_step)

    lam = (jnp.exp(jnp.sum(lq1_ref[...] * lk1_ref[...], axis=-1, keepdims=True))
           - jnp.exp(jnp.sum(lq2_ref[...] * lk2_ref[...], axis=-1, keepdims=True)) + LAM_INIT)
    gain = sw_ref[...] * (1.0 - LAM_INIT)
    for hd in range(n_hd):
        o = acc_ref[hd] / jnp.sum(l_ref[hd], axis=-1, keepdims=True)
        o = o[0:TQ] - lam * o[TQ:2 * TQ]
        ms = jnp.mean(o * o, axis=-1, keepdims=True)
        o_ref[hd, 0] = (o * lax.rsqrt(ms + EPS) * gain).astype(BF16)


def _diff_attn(proj4, lam_q1, lam_k1, lam_q2, lam_k2, subln_w):
    _, b, s, _ = proj4.shape
    const = lambda *_: (0, 0)
    lam_spec = pl.BlockSpec((1, DIFF_HEAD_DIM), const)
    return pl.pallas_call(
        _diff_attn_kernel,
        grid=(b, s // TQ),
        in_specs=[
            pl.BlockSpec((N_HEADS, 1, TQ, HEAD_W), lambda bi, qi: (SEC_DQ, bi, qi, 0)),
            pl.BlockSpec((N_HEADS, 1, s, HEAD_W), lambda bi, qi: (SEC_DK, bi, 0, 0)),
            pl.BlockSpec((N_HEADS, 1, s, HEAD_W), lambda bi, qi: (SEC_DV, bi, 0, 0)),
            lam_spec, lam_spec, lam_spec, lam_spec,
            pl.BlockSpec((1, HEAD_W), const),
        ],
        out_specs=pl.BlockSpec((N_HEADS, 1, TQ, HEAD_W), lambda bi, qi: (0, bi, qi, 0)),
        out_shape=jax.ShapeDtypeStruct((N_HEADS, b, s, HEAD_W), BF16),
        scratch_shapes=[
            pltpu.VMEM((N_HEADS, 2 * TQ, HEAD_W), BF16),
            pltpu.VMEM((N_HEADS, 2 * TQ, HEAD_W), F32),
            pltpu.VMEM((N_HEADS, 2 * TQ, HEAD_W), F32),
            pltpu.VMEM((N_HEADS, 2 * TQ, HEAD_W), F32),
        ],
        compiler_params=pltpu.CompilerParams(
            dimension_semantics=("parallel", "parallel"), vmem_limit_bytes=VMEM_LIMIT),
        name="diff_attn",
    )(proj4, proj4, proj4, lam_q1, lam_k1, lam_q2, lam_k2, subln_w)


def _hgrn_constants():
    n = CHUNK
    t = np.arange(n)[:, None]
    s = np.arange(n)[None, :]
    mats = [(s <= t), (s > t)]
    masks = [(s == t)]
    roles = []
    for lvl in range(N_LEVELS):
        hs = n >> (lvl + 1)
        blk = t // (2 * hs)
        mid = blk * 2 * hs + hs - 1
        is_q = (t % (2 * hs)) >= hs
        w = np.where(is_q, (s > mid) & (s <= t), (s > t) & (s <= mid))
        mats.append(w)
        masks.append((blk == (s // (2 * hs))) & is_q & ((s % (2 * hs)) < hs))
        roles.append(np.broadcast_to(is_q, (n, HEAD_W)))
    wall = np.concatenate(mats, axis=0).astype(np.float32)
    masks = np.stack(masks).astype(np.float32)
    roles = np.stack(roles).astype(np.float32)
    return wall, masks, roles


def _hgrn_kernel(q_ref, k_ref, v_ref, gate_ref, g_ref, wall_ref, mask_ref, role_ref, nw_ref,
                 o_ref, st_ref):
    n_hd = q_ref.shape[0]
    n_chunks = q_ref.shape[2] // CHUNK
    st_ref[...] = jnp.zeros(st_ref.shape, F32)
    wall = wall_ref[...]
    nw = nw_ref[...]

    def chunk_head(hd, rows):
        q_b = q_ref[hd, 0, rows, :]
        k_b = k_ref[hd, 0, rows, :]
        v_b = v_ref[hd, 0, rows, :]
        q = q_b.astype(F32)
        k = k_b.astype(F32)
        g_hi, g_lo = _split_bf16(g_ref[hd, 0, rows, :])
        e = jnp.exp(_dot(wall, g_hi) + _dot(wall, g_lo))
        e_b = e[0:CHUNK]
        e_u = e[CHUNK:2 * CHUNK]

        st = st_ref[hd]
        o = _nt_dot((q * e_b).astype(BF16), st.astype(BF16))

        att = mask_ref[0] * _nt_dot(q_b, k_b)
        for lvl in range(N_LEVELS):
            z = e[(2 + lvl) * CHUNK:(3 + lvl) * CHUNK]
            xz = (jnp.where(role_ref[lvl] > 0.5, q, k) * z).astype(BF16)
            att = att + mask_ref[lvl + 1] * _nt_dot(xz, xz)
        o = o + _dot(att.astype(BF16), v_b)

        st_ref[hd] = st * e_b[CHUNK - 1:CHUNK, :] + _tn_dot(v_b, (k * e_u).astype(BF16))

        ms = jnp.mean(o * o, axis=-1, keepdims=True)
        y = o * lax.rsqrt(ms + EPS) * nw * gate_ref[hd, 0, rows, :].astype(F32)
        o_ref[hd, 0, rows, :] = y.astype(BF16)

    def chunk(ci, carry):
        rows = pl.ds(pl.multiple_of(ci * CHUNK, CHUNK), CHUNK)
        for hd in range(n_hd):
            chunk_head(hd, rows)
        return carry

    lax.fori_loop(0, n_chunks, chunk, 0)


def _hgrn2(proj4, g4, norm_w):
    _, b, s, _ = proj4.shape
    wall, masks, roles = _hgrn_constants()
    heads_spec = lambda sec: pl.BlockSpec((N_HEADS, 1, s, HEAD_W), lambda bi: (sec, bi, 0, 0))
    return pl.pallas_call(
        _hgrn_kernel,
        grid=(b,),
        in_specs=[
            heads_spec(SEC_HQ), heads_spec(SEC_HK), heads_spec(SEC_HI), heads_spec(SEC_HG),
            heads_spec(0),
            pl.BlockSpec(wall.shape, lambda *_: (0, 0)),
            pl.BlockSpec(masks.shape, lambda *_: (0, 0, 0)),
            pl.BlockSpec(roles.shape, lambda *_: (0, 0, 0)),
            pl.BlockSpec((1, HEAD_W), lambda *_: (0, 0)),
        ],
        out_specs=heads_spec(0),
        out_shape=jax.ShapeDtypeStruct((N_HEADS, b, s, HEAD_W), BF16),
        scratch_shapes=[pltpu.VMEM((N_HEADS, HEAD_W, HEAD_W), F32)],
        compiler_params=pltpu.CompilerParams(
            dimension_semantics=("parallel",), vmem_limit_bytes=VMEM_LIMIT),
        name="hgrn2",
    )(proj4, proj4, proj4, proj4, g4, jnp.asarray(wall, BF16), jnp.asarray(masks),
      jnp.asarray(roles), norm_w)


def _ff_chunks():
    chunks, c0 = [], 0
    while c0 < D_FF:
        fc = min(FF_CHUNK, D_FF - c0)
        chunks.append((c0, fc))
        c0 += fc
    return chunks


def _out_ffn_kernel(tiles_per_seq, x_ref, od_ref, oh_ref, wout_ref, ln2_ref, wup_ref, cw_ref,
                    cb_ref, wdn_ref, out_ref, tail_ref, act_ref):
    i = pl.program_id(0)
    mix = jnp.concatenate([od_ref[hd] for hd in range(N_HEADS)]
                          + [oh_ref[hd] for hd in range(N_HEADS)], axis=1)
    x1 = x_ref[...] + _dot(mix, wout_ref[...])
    ms = jnp.mean(x1 * x1, axis=-1, keepdims=True)
    h2 = (x1 * lax.rsqrt(ms + EPS) * ln2_ref[...]).astype(BF16)

    seq_start = (i % tiles_per_seq) == 0
    tm = x1.shape[0]
    for c0, fc in _ff_chunks():
        cols = slice(c0, c0 + fc)
        u = _dot(h2, wup_ref[:, cols])
        v = _dot(h2, wup_ref[:, D_FF + c0:D_FF + c0 + fc])
        tail = jnp.where(seq_start, 0.0, tail_ref[:, cols])
        tail_ref[:, cols] = u[tm - SUBLANES:, :]
        ext = jnp.concatenate([tail, u], axis=0)
        u1 = pltpu.roll(ext, 1, 0)[SUBLANES:]
        u2 = pltpu.roll(ext, 2, 0)[SUBLANES:]
        cw = cw_ref[:, cols]
        c = cb_ref[:, cols] + u2 * cw[0:1] + u1 * cw[1:2] + u * cw[2:3]
        act_ref[:, cols] = (c * _sigmoid(c) * v).astype(BF16)
    out_ref[...] = x1 + _dot(act_ref[...], wdn_ref[...])


def _out_ffn(x2d, o_diff, o_hgrn, w_out, ln2_w, w_up, conv_w, conv_b, w_down, seq_len):
    t = x2d.shape[0]
    const = lambda *_: (0, 0)
    resident = functools.partial(pl.BlockSpec, index_map=const, pipeline_mode=pl.Buffered(1))
    return pl.pallas_call(
        functools.partial(_out_ffn_kernel, seq_len // TM_FFN),
        grid=(t // TM_FFN,),
        in_specs=[
            pl.BlockSpec((TM_FFN, D_MODEL), lambda i: (i, 0)),
            pl.BlockSpec((N_HEADS, TM_FFN, HEAD_W), lambda i: (0, i, 0)),
            pl.BlockSpec((N_HEADS, TM_FFN, HEAD_W), lambda i: (0, i, 0)),
            resident((D_MODEL, D_MODEL)),
            pl.BlockSpec((1, D_MODEL), const),
            resident((D_MODEL, 2 * D_FF)),
            pl.BlockSpec((CONV_WIDTH, D_FF), const),
            pl.BlockSpec((1, D_FF), const),
            resident((D_FF, D_MODEL)),
        ],
        out_specs=pl.BlockSpec((TM_FFN, D_MODEL), lambda i: (i, 0)),
        out_shape=jax.ShapeDtypeStruct((t, D_MODEL), F32),
        scratch_shapes=[
            pltpu.VMEM((SUBLANES, D_FF), F32),
            pltpu.VMEM((TM_FFN, D_FF), BF16),
        ],
        compiler_params=pltpu.CompilerParams(
            dimension_semantics=("arbitrary",), vmem_limit_bytes=VMEM_LIMIT),
        name="out_ffn",
    )(x2d, o_diff, o_hgrn, w_out, ln2_w, w_up, conv_w, conv_b, w_down)


def kernel(x, ln1_w, w_in, q_norm_w, k_norm_w, lam_q1, lam_k1, lam_q2, lam_k2, diff_subln_w,
           hgrn_lb_logits, hgrn_norm_w, w_out, ln2_w, w_up, conv_w, conv_b, w_down):
    b, s, d = x.shape
    depth = ln1_w.shape[0]
    assert depth == 1 and d == D_MODEL and s % TM_FFN == 0 and s % TQ == 0
    t = b * s
    x2d = x.reshape(t, d)
    l = 0
    tile2 = lambda w: jnp.concatenate([w, w], axis=-1)[None, :]

    proj, g = _in_proj(x2d, ln1_w[l][None, :], w_in[l].astype(BF16), tile2(q_norm_w[l]),
                       tile2(k_norm_w[l]), hgrn_lb_logits)
    proj4 = proj.reshape(N_SEC * N_HEADS, b, s, HEAD_W)
    g4 = g.reshape(N_HEADS, b, s, HEAD_W)

    o_diff = _diff_attn(proj4, lam_q1[l][None, :], lam_k1[l][None, :], lam_q2[l][None, :],
                        lam_k2[l][None, :], diff_subln_w[l][None, :])
    o_hgrn = _hgrn2(proj4, g4, hgrn_norm_w[l][None, :])

    out = _out_ffn(x2d, o_diff.reshape(N_HEADS, t, HEAD_W), o_hgrn.reshape(N_HEADS, t, HEAD_W),
                   w_out[l].astype(BF16), ln2_w[l][None, :], w_up[l].astype(BF16), conv_w[l],
                   conv_b[l][None, :], w_down[l].astype(BF16), s)
    return out.reshape(b, s, d)
```

```python
import functools
import math

import numpy as np
import jax
import jax.numpy as jnp
from jax import lax
from jax.experimental import pallas as pl
from jax.experimental.pallas import tpu as pltpu

F32 = jnp.float32
BF16 = jnp.bfloat16

D_MODEL = 1024
CHUNK = 64
HEAD_W = 128
DIFF_HEAD_DIM = 64
N_HEADS = 4
SEC_W = N_HEADS * HEAD_W
N_SEC = 7
IN_COLS = N_SEC * SEC_W
D_FF = 2816
CONV_WIDTH = 3
EPS = 1e-6
LAM_INIT = 0.8 - 0.6 * math.exp(-0.3 * 0)
LOG2E = math.log2(math.e)

SEC_DQ, SEC_DK, SEC_DV, SEC_HQ, SEC_HK, SEC_HI, SEC_HG = range(N_SEC)

LANES = 128
SUBLANES = 8
TM_PROJ = 512
TQ = 256
TK = 256
TM_FFN = 512
FF_CHUNK = 512
VMEM_LIMIT = 56 * 1024 * 1024

N_LEVELS = 6
N_COARSE = 3
HGRN_GROUP = 4


def _nt_dot(a, b):
    return lax.dot_general(a, b, (((1,), (1,)), ((), ())), preferred_element_type=F32)


def _tn_dot(a, b):
    return lax.dot_general(a, b, (((0,), (0,)), ((), ())), preferred_element_type=F32)


def _dot(a, b):
    return jnp.dot(a, b, preferred_element_type=F32)


def _split_bf16(x):
    hi = x.astype(BF16)
    lo = (x - hi.astype(F32)).astype(BF16)
    return hi, lo


def _sigmoid(x):
    return 1.0 / (1.0 + jnp.exp(-x))


def _in_proj_kernel(x_ref, ln1_ref, w_ref, qw_ref, kw_ref, lbl_ref, proj_ref, g_ref):
    x = x_ref[...]
    ms = jnp.mean(x * x, axis=-1, keepdims=True)
    h = (x * lax.rsqrt(ms + EPS) * ln1_ref[...]).astype(BF16)

    r = lax.broadcasted_iota(jnp.int32, (LANES, LANES), 0) // DIFF_HEAD_DIM
    c = lax.broadcasted_iota(jnp.int32, (LANES, LANES), 1) // DIFF_HEAD_DIM
    grp = jnp.where(r == c, 1.0, 0.0).astype(BF16)

    def section(j):
        return _dot(h, w_ref[:, j * SEC_W:(j + 1) * SEC_W])

    def head_cols(a, hd):
        return a[:, hd * HEAD_W:(hd + 1) * HEAD_W]

    q_gain = qw_ref[...] * (DIFF_HEAD_DIM ** -0.5 * LOG2E)
    for sec, w_norm in ((SEC_DQ, q_gain), (SEC_DK, kw_ref[...])):
        acc = section(sec)
        for hd in range(N_HEADS):
            a = head_cols(acc, hd)
            hi, lo = _split_bf16(a * a)
            ss = _dot(hi, grp) + _dot(lo, grp)
            y = a * lax.rsqrt(ss * (1.0 / DIFF_HEAD_DIM) + EPS) * w_norm
            proj_ref[sec * N_HEADS + hd] = y.astype(BF16)

    for sec in (SEC_DV, SEC_HQ, SEC_HI):
        acc = section(sec)
        for hd in range(N_HEADS):
            proj_ref[sec * N_HEADS + hd] = head_cols(acc, hd).astype(BF16)

    lbl = lbl_ref[...]
    e = jnp.exp(lbl - jnp.max(lbl, axis=0, keepdims=True))
    lb = e[0:1] / jnp.sum(e, axis=0, keepdims=True)
    acc = section(SEC_HK)
    f = lb + (1.0 - lb) * _sigmoid(acc)
    log_f = jnp.log(f)
    key = 1.0 - f
    for hd in range(N_HEADS):
        g_ref[hd] = head_cols(log_f, hd)
        proj_ref[SEC_HK * N_HEADS + hd] = head_cols(key, hd).astype(BF16)

    acc = section(SEC_HG)
    gate = acc * _sigmoid(acc)
    for hd in range(N_HEADS):
        proj_ref[SEC_HG * N_HEADS + hd] = head_cols(gate, hd).astype(BF16)


def _in_proj(x2d, ln1_w, w_in, q_norm_w, k_norm_w, lb_logits):
    t = x2d.shape[0]
    const = lambda *_: (0, 0)
    return pl.pallas_call(
        _in_proj_kernel,
        grid=(t // TM_PROJ,),
        in_specs=[
            pl.BlockSpec((TM_PROJ, D_MODEL), lambda i: (i, 0)),
            pl.BlockSpec((1, D_MODEL), const),
            pl.BlockSpec((D_MODEL, IN_COLS), const, pipeline_mode=pl.Buffered(1)),
            pl.BlockSpec((1, HEAD_W), const),
            pl.BlockSpec((1, HEAD_W), const),
            pl.BlockSpec(lb_logits.shape, const),
        ],
        out_specs=[
            pl.BlockSpec((N_SEC * N_HEADS, TM_PROJ, HEAD_W), lambda i: (0, i, 0)),
            pl.BlockSpec((N_HEADS, TM_PROJ, HEAD_W), lambda i: (0, i, 0)),
        ],
        out_shape=[
            jax.ShapeDtypeStruct((N_SEC * N_HEADS, t, HEAD_W), BF16),
            jax.ShapeDtypeStruct((N_HEADS, t, HEAD_W), F32),
        ],
        compiler_params=pltpu.CompilerParams(
            dimension_semantics=("parallel",), vmem_limit_bytes=VMEM_LIMIT),
        name="in_proj",
    )(x2d, ln1_w, w_in, q_norm_w, k_norm_w, lb_logits)


NEG_BIG = -1e30


def _diff_attn_kernel(q_ref, k_ref, v_ref, lq1_ref, lk1_ref, lq2_ref, lk2_ref, sw_ref,
                      o_ref, qs_ref, m_ref, l_ref, acc_ref):
    i = pl.program_id(1)
    n_hd = q_ref.shape[0]
    lane = lax.broadcasted_iota(jnp.int32, (TQ, HEAD_W), 1)
    for hd in range(n_hd):
        q = q_ref[hd, 0]
        zero = jnp.zeros_like(q)
        qs_ref[hd, 0:TQ] = jnp.where(lane < DIFF_HEAD_DIM, q, zero)
        qs_ref[hd, TQ:2 * TQ] = jnp.where(lane >= DIFF_HEAD_DIM, q, zero)

    m_ref[...] = jnp.full(m_ref.shape, NEG_BIG, F32)
    l_ref[...] = jnp.zeros(l_ref.shape, F32)
    acc_ref[...] = jnp.zeros(acc_ref.shape, F32)

    ck = lax.broadcasted_iota(jnp.int32, (TK, 2 * TQ), 0) // CHUNK
    rq = (lax.broadcasted_iota(jnp.int32, (TK, 2 * TQ), 1) % TQ) // CHUNK
    diag_mask = ck <= rq

    def kv_step(j, masked):
        rows = pl.ds(pl.multiple_of(j * TK, TK), TK)
        heads = range(n_hd)
        s_all = [_nt_dot(k_ref[hd, 0, rows, :], qs_ref[hd]) for hd in heads]
        if masked:
            s_all = [jnp.where(diag_mask, s, NEG_BIG) for s in s_all]
        p_all, alpha_all = [], []
        for hd, s in enumerate(s_all):
            m_old = m_ref[hd]
            m_new = jnp.maximum(m_old, jnp.max(s, axis=0, keepdims=True))
            alpha = jnp.exp2(m_old - m_new)
            p = jnp.exp2(s - m_new)
            l_ref[hd] = alpha * l_ref[hd] + jnp.sum(p, axis=0, keepdims=True)
            m_ref[hd] = m_new
            p_all.append(p.astype(BF16))
            alpha_all.append(alpha)
        pv_all = [_tn_dot(v_ref[hd, 0, rows, :], p_all[hd]) for hd in heads]
        for hd in heads:
            acc_ref[hd] = alpha_all[hd] * acc_ref[hd] + pv_all[hd]

    def off_diag(j, carry):
        kv_step(j, masked=False)
        return carry

    lax.fori_loop(0, i, off_diag, 0)
    kv_step(i, masked=True)

    lam = (jnp.exp(jnp.sum(lq1_ref[...] * lk1_ref[...], axis=-1, keepdims=True))
           - jnp.exp(jnp.sum(lq2_ref[...] * lk2_ref[...], axis=-1, keepdims=True)) + LAM_INIT)
    gain = sw_ref[...] * (1.0 - LAM_INIT)
    for hd in range(n_hd):
        o = acc_ref[hd] / l_ref[hd]
        o = o[:, 0:TQ] - lam * o[:, TQ:2 * TQ]
        ms = jnp.mean(o * o, axis=0, keepdims=True)
        o_ref[hd, 0] = (o * lax.rsqrt(ms + EPS) * gain).T.astype(BF16)


def _diff_attn(proj4, lam_q1, lam_k1, lam_q2, lam_k2, subln_w):
    _, b, s, _ = proj4.shape
    const = lambda *_: (0, 0)
    lam_spec = pl.BlockSpec((1, DIFF_HEAD_DIM), const)
    return pl.pallas_call(
        _diff_attn_kernel,
        grid=(b, s // TQ),
        in_specs=[
            pl.BlockSpec((N_HEADS, 1, TQ, HEAD_W), lambda bi, qi: (SEC_DQ, bi, qi, 0)),
            pl.BlockSpec((N_HEADS, 1, s, HEAD_W), lambda bi, qi: (SEC_DK, bi, 0, 0)),
            pl.BlockSpec((N_HEADS, 1, s, HEAD_W), lambda bi, qi: (SEC_DV, bi, 0, 0)),
            lam_spec, lam_spec, lam_spec, lam_spec,
            pl.BlockSpec((HEAD_W, 1), const),
        ],
        out_specs=pl.BlockSpec((N_HEADS, 1, TQ, HEAD_W), lambda bi, qi: (0, bi, qi, 0)),
        out_shape=jax.ShapeDtypeStruct((N_HEADS, b, s, HEAD_W), BF16),
        scratch_shapes=[
            pltpu.VMEM((N_HEADS, 2 * TQ, HEAD_W), BF16),
            pltpu.VMEM((N_HEADS, 1, 2 * TQ), F32),
            pltpu.VMEM((N_HEADS, 1, 2 * TQ), F32),
            pltpu.VMEM((N_HEADS, HEAD_W, 2 * TQ), F32),
        ],
        compiler_params=pltpu.CompilerParams(
            dimension_semantics=("parallel", "parallel"), vmem_limit_bytes=VMEM_LIMIT),
        name="diff_attn",
    )(proj4, proj4, proj4, lam_q1, lam_k1, lam_q2, lam_k2, subln_w)


def _hgrn_constants():
    n = CHUNK
    t = np.arange(n)[:, None]
    s = np.arange(n)[None, :]
    ltri = (s <= t)
    masks = [(s == t)]
    fine, roles = [], []
    for lvl in range(N_LEVELS):
        hs = n >> (lvl + 1)
        blk = t // (2 * hs)
        mid = blk * 2 * hs + hs - 1
        is_q = (t % (2 * hs)) >= hs
        if lvl >= N_COARSE:
            fine.append(np.where(is_q, (s > mid) & (s <= t), (s > t) & (s <= mid)))
        masks.append((blk == (s // (2 * hs))) & is_q & ((s % (2 * hs)) < hs))
        roles.append(np.broadcast_to(np.where(is_q, 1.0, -1.0), (n, HEAD_W)))
    ltri = ltri.astype(np.float32)
    wfine = np.concatenate(fine, axis=0).astype(np.float32)
    masks = np.stack(masks).astype(np.float32)
    roles = np.stack(roles).astype(np.float32)
    return ltri, wfine, masks, roles


def _hgrn_kernel(q_ref, k_ref, v_ref, gate_ref, g_ref, ltri_ref, wfine_ref, mask_ref, role_ref,
                 nw_ref, o_ref, st_ref):
    n_hd = q_ref.shape[0]
    n_chunks = q_ref.shape[2] // CHUNK
    st_ref[...] = jnp.zeros(st_ref.shape, F32)
    ltri = ltri_ref[...]
    wfine = wfine_ref[...]
    nw = nw_ref[...]

    def exponents(g):
        g_hi, g_lo = _split_bf16(g)
        return _dot(ltri, g_hi) + _dot(ltri, g_lo), _dot(wfine, g_hi)

    def decays(b, fine):
        def row_bcast(r, n):
            return jnp.broadcast_to(b[r:r + 1, :], (n, HEAD_W))

        e_b = jnp.exp(b)
        e_u = jnp.exp(row_bcast(CHUNK - 1, CHUNK) - b)
        z = []
        for lvl in range(N_COARSE):
            hs = CHUNK >> (lvl + 1)
            b_mid = jnp.concatenate([row_bcast(blk * 2 * hs + hs - 1, 2 * hs)
                                     for blk in range(CHUNK // (2 * hs))], axis=0)
            z.append(jnp.exp((b - b_mid) * role_ref[lvl]))
        fine = jnp.exp(fine)
        z += [fine[i * CHUNK:(i + 1) * CHUNK] for i in range(N_LEVELS - N_COARSE)]
        return e_b, e_u, z

    def group(gi, carry):
        base = gi * (HGRN_GROUP * CHUNK)
        items = [(hd, pl.ds(pl.multiple_of(base + u * CHUNK, CHUNK), CHUNK))
                 for u in range(HGRN_GROUP) for hd in range(n_hd)]
        q_b = [q_ref[hd, 0, rows, :] for hd, rows in items]
        k_b = [k_ref[hd, 0, rows, :] for hd, rows in items]
        v_b = [v_ref[hd, 0, rows, :] for hd, rows in items]
        expo = [exponents(g_ref[hd, 0, rows, :]) for hd, rows in items]
        dec = [decays(*e) for e in expo]

        xz, q_dec, k_dec = [], [], []
        for n, (e_b, e_u, z) in enumerate(dec):
            q = q_b[n].astype(F32)
            k = k_b[n].astype(F32)
            xz.append([(jnp.where(role_ref[lvl] > 0.0, q, k) * z[lvl]).astype(BF16)
                       for lvl in range(N_LEVELS)])
            q_dec.append((q * e_b).astype(BF16))
            k_dec.append((k * e_u).astype(BF16))

        pair = [[_nt_dot(q_b[n], k_b[n])] + [_nt_dot(x, x) for x in xz[n]]
                for n in range(len(items))]
        kv = [_tn_dot(v_b[n], k_dec[n]) for n in range(len(items))]
        att = [functools.reduce(lambda a, c: a + c,
                                [mask_ref[lvl] * p[lvl] for lvl in range(N_LEVELS + 1)]).astype(BF16)
               for p in pair]
        o = [_dot(att[n], v_b[n]) for n in range(len(items))]

        st = [st_ref[hd] for hd in range(n_hd)]
        for n, (hd, rows) in enumerate(items):
            o[n] = o[n] + _nt_dot(q_dec[n], st[hd].astype(BF16))
            st[hd] = st[hd] * dec[n][0][CHUNK - 1:CHUNK, :] + kv[n]
        for hd in range(n_hd):
            st_ref[hd] = st[hd]

        for n, (hd, rows) in enumerate(items):
            ms = jnp.mean(o[n] * o[n], axis=-1, keepdims=True)
            y = o[n] * lax.rsqrt(ms + EPS) * nw * gate_ref[hd, 0, rows, :].astype(F32)
            o_ref[hd, 0, rows, :] = y.astype(BF16)
        return carry

    lax.fori_loop(0, n_chunks // HGRN_GROUP, group, 0)


def _hgrn2(proj4, g4, norm_w):
    _, b, s, _ = proj4.shape
    ltri, wfine, masks, roles = _hgrn_constants()
    heads_spec = lambda sec: pl.BlockSpec((N_HEADS, 1, s, HEAD_W), lambda bi: (sec, bi, 0, 0))
    return pl.pallas_call(
        _hgrn_kernel,
        grid=(b,),
        in_specs=[
            heads_spec(SEC_HQ), heads_spec(SEC_HK), heads_spec(SEC_HI), heads_spec(SEC_HG),
            heads_spec(0),
            pl.BlockSpec(ltri.shape, lambda *_: (0, 0)),
            pl.BlockSpec(wfine.shape, lambda *_: (0, 0)),
            pl.BlockSpec(masks.shape, lambda *_: (0, 0, 0)),
            pl.BlockSpec(roles.shape, lambda *_: (0, 0, 0)),
            pl.BlockSpec((1, HEAD_W), lambda *_: (0, 0)),
        ],
        out_specs=heads_spec(0),
        out_shape=jax.ShapeDtypeStruct((N_HEADS, b, s, HEAD_W), BF16),
        scratch_shapes=[pltpu.VMEM((N_HEADS, HEAD_W, HEAD_W), F32)],
        compiler_params=pltpu.CompilerParams(
            dimension_semantics=("parallel",), vmem_limit_bytes=VMEM_LIMIT),
        name="hgrn2",
    )(proj4, proj4, proj4, proj4, g4, jnp.asarray(ltri, BF16), jnp.asarray(wfine, BF16),
      jnp.asarray(masks), jnp.asarray(roles), norm_w)


def _ff_chunks():
    chunks, c0 = [], 0
    while c0 < D_FF:
        fc = min(FF_CHUNK, D_FF - c0)
        chunks.append((c0, fc))
        c0 += fc
    return chunks


def _out_ffn_kernel(tiles_per_seq, x_ref, od_ref, oh_ref, wout_ref, ln2_ref, wup_ref, cw_ref,
                    cb_ref, wdn_ref, out_ref, tail_ref, act_ref):
    i = pl.program_id(0)
    mix = jnp.concatenate([od_ref[hd] for hd in range(N_HEADS)]
                          + [oh_ref[hd] for hd in range(N_HEADS)], axis=1)
    x1 = x_ref[...] + _dot(mix, wout_ref[...])
    ms = jnp.mean(x1 * x1, axis=-1, keepdims=True)
    h2 = (x1 * lax.rsqrt(ms + EPS) * ln2_ref[...]).astype(BF16)

    seq_start = (i % tiles_per_seq) == 0
    tm = x1.shape[0]
    for c0, fc in _ff_chunks():
        cols = slice(c0, c0 + fc)
        u = _dot(h2, wup_ref[:, cols])
        v = _dot(h2, wup_ref[:, D_FF + c0:D_FF + c0 + fc])
        tail = jnp.where(seq_start, 0.0, tail_ref[:, cols])
        tail_ref[:, cols] = u[tm - SUBLANES:, :]
        ext = jnp.concatenate([tail, u], axis=0)
        u1 = pltpu.roll(ext, 1, 0)[SUBLANES:]
        u2 = pltpu.roll(ext, 2, 0)[SUBLANES:]
        cw = cw_ref[:, cols]
        c = cb_ref[:, cols] + u2 * cw[0:1] + u1 * cw[1:2] + u * cw[2:3]
        act_ref[:, cols] = (c * _sigmoid(c) * v).astype(BF16)
    out_ref[...] = x1 + _dot(act_ref[...], wdn_ref[...])


def _out_ffn(x2d, o_diff, o_hgrn, w_out, ln2_w, w_up, conv_w, conv_b, w_down, seq_len):
    t = x2d.shape[0]
    const = lambda *_: (0, 0)
    resident = functools.partial(pl.BlockSpec, index_map=const, pipeline_mode=pl.Buffered(1))
    return pl.pallas_call(
        functools.partial(_out_ffn_kernel, seq_len // TM_FFN),
        grid=(t // TM_FFN,),
        in_specs=[
            pl.BlockSpec((TM_FFN, D_MODEL), lambda i: (i, 0)),
            pl.BlockSpec((N_HEADS, TM_FFN, HEAD_W), lambda i: (0, i, 0)),
            pl.BlockSpec((N_HEADS, TM_FFN, HEAD_W), lambda i: (0, i, 0)),
            resident((D_MODEL, D_MODEL)),
            pl.BlockSpec((1, D_MODEL), const),
            resident((D_MODEL, 2 * D_FF)),
            pl.BlockSpec((CONV_WIDTH, D_FF), const),
            pl.BlockSpec((1, D_FF), const),
            resident((D_FF, D_MODEL)),
        ],
        out_specs=pl.BlockSpec((TM_FFN, D_MODEL), lambda i: (i, 0)),
        out_shape=jax.ShapeDtypeStruct((t, D_MODEL), F32),
        scratch_shapes=[
            pltpu.VMEM((SUBLANES, D_FF), F32),
            pltpu.VMEM((TM_FFN, D_FF), BF16),
        ],
        compiler_params=pltpu.CompilerParams(
            dimension_semantics=("arbitrary",), vmem_limit_bytes=VMEM_LIMIT),
        name="out_ffn",
    )(x2d, o_diff, o_hgrn, w_out, ln2_w, w_up, conv_w, conv_b, w_down)


def kernel(x, ln1_w, w_in, q_norm_w, k_norm_w, lam_q1, lam_k1, lam_q2, lam_k2, diff_subln_w,
           hgrn_lb_logits, hgrn_norm_w, w_out, ln2_w, w_up, conv_w, conv_b, w_down):
    b, s, d = x.shape
    depth = ln1_w.shape[0]
    assert depth == 1 and d == D_MODEL and s % TM_FFN == 0 and s % TQ == 0
    t = b * s
    x2d = x.reshape(t, d)
    l = 0
    tile2 = lambda w: jnp.concatenate([w, w], axis=-1)[None, :]

    proj, g = _in_proj(x2d, ln1_w[l][None, :], w_in[l].astype(BF16), tile2(q_norm_w[l]),
                       tile2(k_norm_w[l]), hgrn_lb_logits)
    proj4 = proj.reshape(N_SEC * N_HEADS, b, s, HEAD_W)
    g4 = g.reshape(N_HEADS, b, s, HEAD_W)

    o_diff = _diff_attn(proj4, lam_q1[l][None, :], lam_k1[l][None, :], lam_q2[l][None, :],
                        lam_k2[l][None, :], diff_subln_w[l][:, None])
    o_hgrn = _hgrn2(proj4, g4, hgrn_norm_w[l][None, :])

    out = _out_ffn(x2d, o_diff.reshape(N_HEADS, t, HEAD_W), o_hgrn.reshape(N_HEADS, t, HEAD_W),
                   w_out[l].astype(BF16), ln2_w[l][None, :], w_up[l].astype(BF16), conv_w[l],
                   conv_b[l][None, :], w_down[l].astype(BF16), s)
    return out.reshape(b, s, d)
```

```python
import functools
import math

import numpy as np
import jax
import jax.numpy as jnp
from jax import lax
from jax.experimental import pallas as pl
from jax.experimental.pallas import tpu as pltpu

F32 = jnp.float32
BF16 = jnp.bfloat16

D_MODEL = 1024
CHUNK = 64
HEAD_W = 128
DIFF_HEAD_DIM = 64
N_HEADS = 4
SEC_W = N_HEADS * HEAD_W
N_SEC = 7
IN_COLS = N_SEC * SEC_W
D_FF = 2816
CONV_WIDTH = 3
EPS = 1e-6
LAM_INIT = 0.8 - 0.6 * math.exp(-0.3 * 0)
LOG2E = math.log2(math.e)

SEC_DQ, SEC_DK, SEC_DV, SEC_HQ, SEC_HK, SEC_HI, SEC_HG = range(N_SEC)

LANES = 128
SUBLANES = 8
MXU_W = 256
TM_PROJ = 512
TQ = 256
TK = 256
TM_FFN = 512
FF_CHUNK = 512
VMEM_LIMIT = 56 * 1024 * 1024

N_LEVELS = 6
N_COARSE = 3
HGRN_GROUP = 4


def _nt_dot(a, b):
    return lax.dot_general(a, b, (((1,), (1,)), ((), ())), preferred_element_type=F32)


def _tn_dot(a, b):
    return lax.dot_general(a, b, (((0,), (0,)), ((), ())), preferred_element_type=F32)


def _dot(a, b):
    return jnp.dot(a, b, preferred_element_type=F32)


def _split_bf16(x):
    hi = x.astype(BF16)
    lo = (x - hi.astype(F32)).astype(BF16)
    return hi, lo


def _sigmoid(x):
    return 1.0 / (1.0 + jnp.exp(-x))


def _in_proj_kernel(x_ref, ln1_ref, w_ref, qw_ref, kw_ref, lbl_ref, proj_ref, g_ref):
    x = x_ref[...]
    ms = jnp.mean(x * x, axis=-1, keepdims=True)
    h = (x * lax.rsqrt(ms + EPS) * ln1_ref[...]).astype(BF16)

    r = lax.broadcasted_iota(jnp.int32, (MXU_W, MXU_W), 0) // DIFF_HEAD_DIM
    c = lax.broadcasted_iota(jnp.int32, (MXU_W, MXU_W), 1) // DIFF_HEAD_DIM
    grp = jnp.where(r == c, 1.0, 0.0).astype(BF16)

    def head_cols(a, hd):
        return a[:, hd * HEAD_W:(hd + 1) * HEAD_W]

    def store_heads(sec, a):
        for hd in range(N_HEADS):
            proj_ref[sec * N_HEADS + hd] = head_cols(a, hd).astype(BF16)

    acc = [_dot(h, w_ref[:, j * SEC_W:(j + 1) * SEC_W]) for j in range(N_SEC)]

    q_gain = qw_ref[...] * (DIFF_HEAD_DIM ** -0.5 * LOG2E)
    halves = [slice(c0, c0 + MXU_W) for c0 in range(0, SEC_W, MXU_W)]
    ss = {sec: [_dot((acc[sec][:, cols] * acc[sec][:, cols]).astype(BF16), grp) for cols in halves]
          for sec in (SEC_DQ, SEC_DK)}
    for sec, w_norm in ((SEC_DQ, q_gain), (SEC_DK, kw_ref[...])):
        w2 = jnp.concatenate([w_norm] * (MXU_W // HEAD_W), axis=1)
        y = [acc[sec][:, cols] * lax.rsqrt(ss[sec][n] * (1.0 / DIFF_HEAD_DIM) + EPS) * w2
             for n, cols in enumerate(halves)]
        store_heads(sec, jnp.concatenate(y, axis=1))

    for sec in (SEC_DV, SEC_HQ, SEC_HI):
        store_heads(sec, acc[sec])

    lbl = lbl_ref[...]
    e = jnp.exp(lbl - jnp.max(lbl, axis=0, keepdims=True))
    lb = e[0:1] / jnp.sum(e, axis=0, keepdims=True)
    f = lb + (1.0 - lb) * _sigmoid(acc[SEC_HK])
    log2_f = jnp.log(f) * LOG2E
    for hd in range(N_HEADS):
        g_ref[hd] = head_cols(log2_f, hd)
    store_heads(SEC_HK, 1.0 - f)

    store_heads(SEC_HG, acc[SEC_HG] * _sigmoid(acc[SEC_HG]))


def _in_proj(x2d, ln1_w, w_in, q_norm_w, k_norm_w, lb_logits):
    t = x2d.shape[0]
    const = lambda *_: (0, 0)
    return pl.pallas_call(
        _in_proj_kernel,
        grid=(t // TM_PROJ,),
        in_specs=[
            pl.BlockSpec((TM_PROJ, D_MODEL), lambda i: (i, 0)),
            pl.BlockSpec((1, D_MODEL), const),
            pl.BlockSpec((D_MODEL, IN_COLS), const, pipeline_mode=pl.Buffered(1)),
            pl.BlockSpec((1, HEAD_W), const),
            pl.BlockSpec((1, HEAD_W), const),
            pl.BlockSpec(lb_logits.shape, const),
        ],
        out_specs=[
            pl.BlockSpec((N_SEC * N_HEADS, TM_PROJ, HEAD_W), lambda i: (0, i, 0)),
            pl.BlockSpec((N_HEADS, TM_PROJ, HEAD_W), lambda i: (0, i, 0)),
        ],
        out_shape=[
            jax.ShapeDtypeStruct((N_SEC * N_HEADS, t, HEAD_W), BF16),
            jax.ShapeDtypeStruct((N_HEADS, t, HEAD_W), F32),
        ],
        compiler_params=pltpu.CompilerParams(
            dimension_semantics=("parallel",), vmem_limit_bytes=VMEM_LIMIT),
        name="in_proj",
    )(x2d, ln1_w, w_in, q_norm_w, k_norm_w, lb_logits)


NEG_BIG = -1e30


def _diff_attn_kernel(q_ref, k_ref, v_ref, lq1_ref, lk1_ref, lq2_ref, lk2_ref, sw_ref,
                      o_ref, qs_ref, s_ref, m_ref, l_ref, acc_ref):
    i = pl.program_id(1)
    n_hd = q_ref.shape[0]
    lane = lax.broadcasted_iota(jnp.int32, (TQ, HEAD_W), 1)
    for hd in range(n_hd):
        q = q_ref[hd, 0]
        zero = jnp.zeros_like(q)
        qs_ref[hd, 0:TQ] = jnp.where(lane < DIFF_HEAD_DIM, q, zero)
        qs_ref[hd, TQ:2 * TQ] = jnp.where(lane >= DIFF_HEAD_DIM, q, zero)

    m_ref[...] = jnp.full(m_ref.shape, NEG_BIG, F32)
    l_ref[...] = jnp.zeros(l_ref.shape, F32)
    acc_ref[...] = jnp.zeros(acc_ref.shape, F32)

    ck = lax.broadcasted_iota(jnp.int32, (TK, 2 * TQ), 0) // CHUNK
    rq = (lax.broadcasted_iota(jnp.int32, (TK, 2 * TQ), 1) % TQ) // CHUNK
    diag_mask = ck <= rq

    heads = range(n_hd)

    def kv_rows(j):
        return pl.ds(pl.multiple_of(j * TK, TK), TK)

    def score_step(j, slot):
        rows = kv_rows(j)
        for hd in heads:
            s_ref[slot, hd] = _nt_dot(k_ref[hd, 0, rows, :], qs_ref[hd])

    def softmax_step(j, slot, masked):
        rows = kv_rows(j)
        p_all, alpha_all = [], []
        for hd in heads:
            s = s_ref[slot, hd]
            if masked:
                s = jnp.where(diag_mask, s, NEG_BIG)
            m_old = m_ref[hd]
            m_new = jnp.maximum(m_old, jnp.max(s, axis=0, keepdims=True))
            alpha = jnp.exp2(m_old - m_new)
            p = jnp.exp2(s - m_new)
            l_ref[hd] = alpha * l_ref[hd] + jnp.sum(p, axis=0, keepdims=True)
            m_ref[hd] = m_new
            p_all.append(p.astype(BF16))
            alpha_all.append(alpha)
        pv_all = [_tn_dot(v_ref[hd, 0, rows, :], p_all[hd]) for hd in heads]
        for hd in heads:
            acc_ref[hd] = alpha_all[hd] * acc_ref[hd] + pv_all[hd]

    score_step(0, 0)

    def off_diag_pair(jj, carry):
        j = 2 * jj
        score_step(j + 1, 1)
        softmax_step(j, 0, masked=False)
        score_step(j + 2, 0)
        softmax_step(j + 1, 1, masked=False)
        return carry

    lax.fori_loop(0, i // 2, off_diag_pair, 0)

    @pl.when(i % 2 == 0)
    def _():
        softmax_step(i, 0, masked=True)

    @pl.when(i % 2 == 1)
    def _():
        score_step(i, 1)
        softmax_step(i - 1, 0, masked=False)
        softmax_step(i, 1, masked=True)

    lam = (jnp.exp(jnp.sum(lq1_ref[...] * lk1_ref[...], axis=-1, keepdims=True))
           - jnp.exp(jnp.sum(lq2_ref[...] * lk2_ref[...], axis=-1, keepdims=True)) + LAM_INIT)
    gain = sw_ref[...] * (1.0 - LAM_INIT)
    for hd in range(n_hd):
        o = acc_ref[hd] / l_ref[hd]
        o = o[:, 0:TQ] - lam * o[:, TQ:2 * TQ]
        ms = jnp.mean(o * o, axis=0, keepdims=True)
        o_ref[hd, 0] = (o * lax.rsqrt(ms + EPS) * gain).T.astype(BF16)


def _diff_attn(proj4, lam_q1, lam_k1, lam_q2, lam_k2, subln_w):
    _, b, s, _ = proj4.shape
    const = lambda *_: (0, 0)
    lam_spec = pl.BlockSpec((1, DIFF_HEAD_DIM), const)
    return pl.pallas_call(
        _diff_attn_kernel,
        grid=(b, s // TQ),
        in_specs=[
            pl.BlockSpec((N_HEADS, 1, TQ, HEAD_W), lambda bi, qi: (SEC_DQ, bi, qi, 0)),
            pl.BlockSpec((N_HEADS, 1, s, HEAD_W), lambda bi, qi: (SEC_DK, bi, 0, 0)),
            pl.BlockSpec((N_HEADS, 1, s, HEAD_W), lambda bi, qi: (SEC_DV, bi, 0, 0)),
            lam_spec, lam_spec, lam_spec, lam_spec,
            pl.BlockSpec((HEAD_W, 1), const),
        ],
        out_specs=pl.BlockSpec((N_HEADS, 1, TQ, HEAD_W), lambda bi, qi: (0, bi, qi, 0)),
        out_shape=jax.ShapeDtypeStruct((N_HEADS, b, s, HEAD_W), BF16),
        scratch_shapes=[
            pltpu.VMEM((N_HEADS, 2 * TQ, HEAD_W), BF16),
            pltpu.VMEM((2, N_HEADS, TK, 2 * TQ), F32),
            pltpu.VMEM((N_HEADS, 1, 2 * TQ), F32),
            pltpu.VMEM((N_HEADS, 1, 2 * TQ), F32),
            pltpu.VMEM((N_HEADS, HEAD_W, 2 * TQ), F32),
        ],
        compiler_params=pltpu.CompilerParams(
            dimension_semantics=("parallel", "parallel"), vmem_limit_bytes=VMEM_LIMIT),
        name="diff_attn",
    )(proj4, proj4, proj4, lam_q1, lam_k1, lam_q2, lam_k2, subln_w)


def _hgrn_constants():
    n = CHUNK
    t = np.arange(n)[:, None]
    s = np.arange(n)[None, :]
    ltri = (s <= t)
    masks = [(s == t)]
    fine, roles = [], []
    for lvl in range(N_LEVELS):
        hs = n >> (lvl + 1)
        blk = t // (2 * hs)
        mid = blk * 2 * hs + hs - 1
        is_q = (t % (2 * hs)) >= hs
        if lvl >= N_COARSE:
            fine.append(np.where(is_q, (s > mid) & (s <= t), (s > t) & (s <= mid)))
        masks.append((blk == (s // (2 * hs))) & is_q & ((s % (2 * hs)) < hs))
        roles.append(np.broadcast_to(np.where(is_q, 1.0, -1.0), (n, HEAD_W)))
    ltri = ltri.astype(np.float32)
    wfine = np.concatenate(fine, axis=0).astype(np.float32)
    masks = np.stack(masks).astype(np.float32)
    roles = np.stack(roles).astype(np.float32)
    return ltri, wfine, masks, roles


def _hgrn_kernel(q_ref, k_ref, v_ref, gate_ref, g_ref, ltri_ref, wfine_ref, mask_ref, role_ref,
                 nw_ref, o_ref, st_ref):
    n_hd = q_ref.shape[0]
    n_chunks = q_ref.shape[2] // CHUNK
    st_ref[...] = jnp.zeros(st_ref.shape, F32)
    ltri = ltri_ref[...]
    wfine = wfine_ref[...]
    nw = nw_ref[...]

    def exponents(g):
        g_hi, g_lo = _split_bf16(g)
        return _dot(ltri, g_hi) + _dot(ltri, g_lo), _dot(wfine, g_hi)

    def decays(b, fine):
        def row_bcast(r, n):
            return jnp.broadcast_to(b[r:r + 1, :], (n, HEAD_W))

        e_b = jnp.exp2(b)
        e_u = jnp.exp2(row_bcast(CHUNK - 1, CHUNK) - b)
        z = []
        for lvl in range(N_COARSE):
            hs = CHUNK >> (lvl + 1)
            b_mid = jnp.concatenate([row_bcast(blk * 2 * hs + hs - 1, 2 * hs)
                                     for blk in range(CHUNK // (2 * hs))], axis=0)
            z.append(jnp.exp2((b - b_mid) * role_ref[lvl]))
        fine = jnp.exp2(fine)
        z += [fine[i * CHUNK:(i + 1) * CHUNK] for i in range(N_LEVELS - N_COARSE)]
        return e_b, e_u, z

    def group(gi, carry):
        base = gi * (HGRN_GROUP * CHUNK)
        items = [(hd, pl.ds(pl.multiple_of(base + u * CHUNK, CHUNK), CHUNK))
                 for u in range(HGRN_GROUP) for hd in range(n_hd)]
        q_b = [q_ref[hd, 0, rows, :] for hd, rows in items]
        k_b = [k_ref[hd, 0, rows, :] for hd, rows in items]
        v_b = [v_ref[hd, 0, rows, :] for hd, rows in items]
        expo = [exponents(g_ref[hd, 0, rows, :]) for hd, rows in items]
        dec = [decays(*e) for e in expo]

        xz, q_dec, k_dec = [], [], []
        for n, (e_b, e_u, z) in enumerate(dec):
            q = q_b[n].astype(F32)
            k = k_b[n].astype(F32)
            xz.append([(jnp.where(role_ref[lvl] > 0.0, q, k) * z[lvl]).astype(BF16)
                       for lvl in range(N_LEVELS)])
            q_dec.append((q * e_b).astype(BF16))
            k_dec.append((k * e_u).astype(BF16))

        pair = [[_nt_dot(q_b[n], k_b[n])] + [_nt_dot(x, x) for x in xz[n]]
                for n in range(len(items))]
        kv = [_tn_dot(v_b[n], k_dec[n]) for n in range(len(items))]
        att = [functools.reduce(lambda a, c: a + c,
                                [mask_ref[lvl] * p[lvl] for lvl in range(N_LEVELS + 1)]).astype(BF16)
               for p in pair]
        o = [_dot(att[n], v_b[n]) for n in range(len(items))]

        st = [st_ref[hd] for hd in range(n_hd)]
        for n, (hd, rows) in enumerate(items):
            o[n] = o[n] + _nt_dot(q_dec[n], st[hd].astype(BF16))
            st[hd] = st[hd] * dec[n][0][CHUNK - 1:CHUNK, :] + kv[n]
        for hd in range(n_hd):
            st_ref[hd] = st[hd]

        for n, (hd, rows) in enumerate(items):
            ms = jnp.mean(o[n] * o[n], axis=-1, keepdims=True)
            y = o[n] * lax.rsqrt(ms + EPS) * nw * gate_ref[hd, 0, rows, :].astype(F32)
            o_ref[hd, 0, rows, :] = y.astype(BF16)
        return carry

    lax.fori_loop(0, n_chunks // HGRN_GROUP, group, 0)


def _hgrn2(proj4, g4, norm_w):
    _, b, s, _ = proj4.shape
    ltri, wfine, masks, roles = _hgrn_constants()
    heads_spec = lambda sec: pl.BlockSpec((N_HEADS, 1, s, HEAD_W), lambda bi: (sec, bi, 0, 0))
    return pl.pallas_call(
        _hgrn_kernel,
        grid=(b,),
        in_specs=[
            heads_spec(SEC_HQ), heads_spec(SEC_HK), heads_spec(SEC_HI), heads_spec(SEC_HG),
            heads_spec(0),
            pl.BlockSpec(ltri.shape, lambda *_: (0, 0)),
            pl.BlockSpec(wfine.shape, lambda *_: (0, 0)),
            pl.BlockSpec(masks.shape, lambda *_: (0, 0, 0)),
            pl.BlockSpec(roles.shape, lambda *_: (0, 0, 0)),
            pl.BlockSpec((1, HEAD_W), lambda *_: (0, 0)),
        ],
        out_specs=heads_spec(0),
        out_shape=jax.ShapeDtypeStruct((N_HEADS, b, s, HEAD_W), BF16),
        scratch_shapes=[pltpu.VMEM((N_HEADS, HEAD_W, HEAD_W), F32)],
        compiler_params=pltpu.CompilerParams(
            dimension_semantics=("parallel",), vmem_limit_bytes=VMEM_LIMIT),
        name="hgrn2",
    )(proj4, proj4, proj4, proj4, g4, jnp.asarray(ltri, BF16), jnp.asarray(wfine, BF16),
      jnp.asarray(masks), jnp.asarray(roles), norm_w)


def _ff_chunks():
    chunks, c0 = [], 0
    while c0 < D_FF:
        fc = min(FF_CHUNK, D_FF - c0)
        chunks.append((c0, fc))
        c0 += fc
    return chunks


def _out_ffn_kernel(tiles_per_seq, x_ref, od_ref, oh_ref, wout_ref, ln2_ref, wup_ref, cw_ref,
                    cb_ref, wdn_ref, out_ref, tail_ref, act_ref):
    i = pl.program_id(0)
    mix = jnp.concatenate([od_ref[hd] for hd in range(N_HEADS)]
                          + [oh_ref[hd] for hd in range(N_HEADS)], axis=1)
    x1 = x_ref[...] + _dot(mix, wout_ref[...])
    ms = jnp.mean(x1 * x1, axis=-1, keepdims=True)
    h2 = (x1 * lax.rsqrt(ms + EPS) * ln2_ref[...]).astype(BF16)

    seq_start = (i % tiles_per_seq) == 0
    tm = x1.shape[0]
    for c0, fc in _ff_chunks():
        cols = slice(c0, c0 + fc)
        u = _dot(h2, wup_ref[:, cols])
        v = _dot(h2, wup_ref[:, D_FF + c0:D_FF + c0 + fc])
        tail = jnp.where(seq_start, 0.0, tail_ref[:, cols])
        tail_ref[:, cols] = u[tm - SUBLANES:, :]
        ext = jnp.concatenate([tail, u], axis=0)
        u1 = pltpu.roll(ext, 1, 0)[SUBLANES:]
        u2 = pltpu.roll(ext, 2, 0)[SUBLANES:]
        cw = cw_ref[:, cols]
        c = cb_ref[:, cols] + u2 * cw[0:1] + u1 * cw[1:2] + u * cw[2:3]
        act_ref[:, cols] = (c * _sigmoid(c) * v).astype(BF16)
    out_ref[...] = x1 + _dot(act_ref[...], wdn_ref[...])


def _out_ffn(x2d, o_diff, o_hgrn, w_out, ln2_w, w_up, conv_w, conv_b, w_down, seq_len):
    t = x2d.shape[0]
    const = lambda *_: (0, 0)
    resident = functools.partial(pl.BlockSpec, index_map=const, pipeline_mode=pl.Buffered(1))
    return pl.pallas_call(
        functools.partial(_out_ffn_kernel, seq_len // TM_FFN),
        grid=(t // TM_FFN,),
        in_specs=[
            pl.BlockSpec((TM_FFN, D_MODEL), lambda i: (i, 0)),
            pl.BlockSpec((N_HEADS, TM_FFN, HEAD_W), lambda i: (0, i, 0)),
            pl.BlockSpec((N_HEADS, TM_FFN, HEAD_W), lambda i: (0, i, 0)),
            resident((D_MODEL, D_MODEL)),
            pl.BlockSpec((1, D_MODEL), const),
            resident((D_MODEL, 2 * D_FF)),
            pl.BlockSpec((CONV_WIDTH, D_FF), const),
            pl.BlockSpec((1, D_FF), const),
            resident((D_FF, D_MODEL)),
        ],
        out_specs=pl.BlockSpec((TM_FFN, D_MODEL), lambda i: (i, 0)),
        out_shape=jax.ShapeDtypeStruct((t, D_MODEL), F32),
        scratch_shapes=[
            pltpu.VMEM((SUBLANES, D_FF), F32),
            pltpu.VMEM((TM_FFN, D_FF), BF16),
        ],
        compiler_params=pltpu.CompilerParams(
            dimension_semantics=("arbitrary",), vmem_limit_bytes=VMEM_LIMIT),
        name="out_ffn",
    )(x2d, o_diff, o_hgrn, w_out, ln2_w, w_up, conv_w, conv_b, w_down)


def kernel(x, ln1_w, w_in, q_norm_w, k_norm_w, lam_q1, lam_k1, lam_q2, lam_k2, diff_subln_w,
           hgrn_lb_logits, hgrn_norm_w, w_out, ln2_w, w_up, conv_w, conv_b, w_down):
    b, s, d = x.shape
    depth = ln1_w.shape[0]
    assert depth == 1 and d == D_MODEL and s % TM_FFN == 0 and s % TQ == 0
    t = b * s
    x2d = x.reshape(t, d)
    l = 0
    tile2 = lambda w: jnp.concatenate([w, w], axis=-1)[None, :]

    proj, g = _in_proj(x2d, ln1_w[l][None, :], w_in[l].astype(BF16), tile2(q_norm_w[l]),
                       tile2(k_norm_w[l]), hgrn_lb_logits)
    proj4 = proj.reshape(N_SEC * N_HEADS, b, s, HEAD_W)
    g4 = g.reshape(N_HEADS, b, s, HEAD_W)

    o_diff = _diff_attn(proj4, lam_q1[l][None, :], lam_k1[l][None, :], lam_q2[l][None, :],
                        lam_k2[l][None, :], diff_subln_w[l][:, None])
    o_hgrn = _hgrn2(proj4, g4, hgrn_norm_w[l][None, :])

    out = _out_ffn(x2d, o_diff.reshape(N_HEADS, t, HEAD_W), o_hgrn.reshape(N_HEADS, t, HEAD_W),
                   w_out[l].astype(BF16), ln2_w[l][None, :], w_up[l].astype(BF16), conv_w[l],
                   conv_b[l][None, :], w_down[l].astype(BF16), s)
    return out.reshape(b, s, d)
```

```python
import functools
import math

import numpy as np
import jax
import jax.numpy as jnp
from jax import lax
from jax.experimental import pallas as pl
from jax.experimental.pallas import tpu as pltpu

F32 = jnp.float32
BF16 = jnp.bfloat16

D_MODEL = 1024
CHUNK = 64
HEAD_W = 128
DIFF_HEAD_DIM = 64
N_HEADS = 4
SEC_W = N_HEADS * HEAD_W
N_SEC = 7
IN_COLS = N_SEC * SEC_W
D_FF = 2816
CONV_WIDTH = 3
EPS = 1e-6
LAM_INIT = 0.8 - 0.6 * math.exp(-0.3 * 0)
LOG2E = math.log2(math.e)

SEC_DQ, SEC_DK, SEC_DV, SEC_HQ, SEC_HK, SEC_HI, SEC_HG = range(N_SEC)
SLOT_DQ, SLOT_DK, SLOT_HQ, SLOT_HK, SLOT_HI, SLOT_HG = range(6)
N_SLOT = 6
BF16_SUBLANES = 16
V_ROWS = HEAD_W + BF16_SUBLANES

LANES = 128
SUBLANES = 8
MXU_W = 256
TM_PROJ = 512
TQ = 256
TK = 256
TM_FFN = 512
FF_CHUNK = 512
VMEM_LIMIT = 56 * 1024 * 1024

N_LEVELS = 6
N_COARSE = 3
HGRN_GROUP = 4


def _nt_dot(a, b):
    return lax.dot_general(a, b, (((1,), (1,)), ((), ())), preferred_element_type=F32)


def _tn_dot(a, b):
    return lax.dot_general(a, b, (((0,), (0,)), ((), ())), preferred_element_type=F32)


def _dot(a, b):
    return jnp.dot(a, b, preferred_element_type=F32)


def _split_bf16(x):
    hi = x.astype(BF16)
    lo = (x - hi.astype(F32)).astype(BF16)
    return hi, lo


def _sigmoid(x):
    return 1.0 / (1.0 + jnp.exp(-x))


def _in_proj_kernel(x_ref, ln1_ref, w_ref, qw_ref, kw_ref, lbl_ref, proj_ref, g_ref, vt_ref):
    x = x_ref[...]
    ms = jnp.mean(x * x, axis=-1, keepdims=True)
    h = (x * lax.rsqrt(ms + EPS) * ln1_ref[...]).astype(BF16)

    r = lax.broadcasted_iota(jnp.int32, (MXU_W, MXU_W), 0) // DIFF_HEAD_DIM
    c = lax.broadcasted_iota(jnp.int32, (MXU_W, MXU_W), 1) // DIFF_HEAD_DIM
    grp = jnp.where(r == c, 1.0, 0.0).astype(BF16)

    def head_cols(a, hd):
        return a[:, hd * HEAD_W:(hd + 1) * HEAD_W]

    def store_heads(slot, a):
        for hd in range(N_HEADS):
            proj_ref[slot * N_HEADS + hd] = head_cols(a, hd).astype(BF16)

    acc = [_dot(h, w_ref[:, j * SEC_W:(j + 1) * SEC_W]) for j in range(N_SEC)]

    q_gain = qw_ref[...] * (DIFF_HEAD_DIM ** -0.5 * LOG2E)
    halves = [slice(c0, c0 + MXU_W) for c0 in range(0, SEC_W, MXU_W)]
    ss = {sec: [_dot((acc[sec][:, cols] * acc[sec][:, cols]).astype(BF16), grp) for cols in halves]
          for sec in (SEC_DQ, SEC_DK)}
    for sec, slot, w_norm in ((SEC_DQ, SLOT_DQ, q_gain), (SEC_DK, SLOT_DK, kw_ref[...])):
        w2 = jnp.concatenate([w_norm] * (MXU_W // HEAD_W), axis=1)
        y = [acc[sec][:, cols] * lax.rsqrt(ss[sec][n] * (1.0 / DIFF_HEAD_DIM) + EPS) * w2
             for n, cols in enumerate(halves)]
        store_heads(slot, jnp.concatenate(y, axis=1))

    for hd in range(N_HEADS):
        vt_ref[hd, 0:HEAD_W, :] = head_cols(acc[SEC_DV], hd).T.astype(BF16)
        vt_ref[hd, HEAD_W:V_ROWS, :] = jnp.ones((V_ROWS - HEAD_W, x.shape[0]), BF16)

    store_heads(SLOT_HQ, acc[SEC_HQ])
    store_heads(SLOT_HI, acc[SEC_HI])

    lbl = lbl_ref[...]
    e = jnp.exp(lbl - jnp.max(lbl, axis=0, keepdims=True))
    lb = e[0:1] / jnp.sum(e, axis=0, keepdims=True)
    f = lb + (1.0 - lb) * _sigmoid(acc[SEC_HK])
    log2_f = jnp.log(f) * LOG2E
    for hd in range(N_HEADS):
        g_ref[hd] = head_cols(log2_f, hd)
    store_heads(SLOT_HK, 1.0 - f)

    store_heads(SLOT_HG, acc[SEC_HG] * _sigmoid(acc[SEC_HG]))


def _in_proj(x2d, ln1_w, w_in, q_norm_w, k_norm_w, lb_logits):
    t = x2d.shape[0]
    const = lambda *_: (0, 0)
    return pl.pallas_call(
        _in_proj_kernel,
        grid=(t // TM_PROJ,),
        in_specs=[
            pl.BlockSpec((TM_PROJ, D_MODEL), lambda i: (i, 0)),
            pl.BlockSpec((1, D_MODEL), const),
            pl.BlockSpec((D_MODEL, IN_COLS), const, pipeline_mode=pl.Buffered(1)),
            pl.BlockSpec((1, HEAD_W), const),
            pl.BlockSpec((1, HEAD_W), const),
            pl.BlockSpec(lb_logits.shape, const),
        ],
        out_specs=[
            pl.BlockSpec((N_SLOT * N_HEADS, TM_PROJ, HEAD_W), lambda i: (0, i, 0)),
            pl.BlockSpec((N_HEADS, TM_PROJ, HEAD_W), lambda i: (0, i, 0)),
            pl.BlockSpec((N_HEADS, V_ROWS, TM_PROJ), lambda i: (0, 0, i)),
        ],
        out_shape=[
            jax.ShapeDtypeStruct((N_SLOT * N_HEADS, t, HEAD_W), BF16),
            jax.ShapeDtypeStruct((N_HEADS, t, HEAD_W), F32),
            jax.ShapeDtypeStruct((N_HEADS, V_ROWS, t), BF16),
        ],
        compiler_params=pltpu.CompilerParams(
            dimension_semantics=("parallel",), vmem_limit_bytes=VMEM_LIMIT),
        name="in_proj",
    )(x2d, ln1_w, w_in, q_norm_w, k_norm_w, lb_logits)


NEG_BIG = -1e30


def _diff_attn_kernel(q_ref, k_ref, vt_ref, lq1_ref, lk1_ref, lq2_ref, lk2_ref, sw_ref,
                      o_ref, qs_ref, s_ref, m_ref, acc_ref):
    i = pl.program_id(1)
    n_hd = q_ref.shape[0]
    lane = lax.broadcasted_iota(jnp.int32, (TQ, HEAD_W), 1)
    for hd in range(n_hd):
        q = q_ref[hd, 0]
        zero = jnp.zeros_like(q)
        qs_ref[hd, 0:TQ] = jnp.where(lane < DIFF_HEAD_DIM, q, zero)
        qs_ref[hd, TQ:2 * TQ] = jnp.where(lane >= DIFF_HEAD_DIM, q, zero)

    m_ref[...] = jnp.full(m_ref.shape, NEG_BIG, F32)
    acc_ref[...] = jnp.zeros(acc_ref.shape, F32)

    ck = lax.broadcasted_iota(jnp.int32, (TK, 2 * TQ), 0) // CHUNK
    rq = (lax.broadcasted_iota(jnp.int32, (TK, 2 * TQ), 1) % TQ) // CHUNK
    diag_mask = ck <= rq

    heads = range(n_hd)

    def kv_rows(j):
        return pl.ds(pl.multiple_of(j * TK, TK), TK)

    def score_step(j, slot):
        rows = kv_rows(j)
        for hd in heads:
            s_ref[slot, hd] = _nt_dot(k_ref[hd, 0, rows, :], qs_ref[hd])

    def softmax_step(j, slot, masked):
        rows = kv_rows(j)
        p_all, alpha_all = [], []
        for hd in heads:
            s = s_ref[slot, hd]
            if masked:
                s = jnp.where(diag_mask, s, NEG_BIG)
            m_old = m_ref[hd]
            m_new = jnp.maximum(m_old, jnp.max(s, axis=0, keepdims=True))
            m_ref[hd] = m_new
            alpha_all.append(jnp.exp2(m_old - m_new))
            p_all.append(jnp.exp2((s - m_new).astype(BF16)))
        pv_all = [_dot(vt_ref[hd, :, rows], p_all[hd]) for hd in heads]
        for hd in heads:
            acc_ref[hd] = alpha_all[hd] * acc_ref[hd] + pv_all[hd]

    score_step(0, 0)

    def off_diag_pair(jj, carry):
        j = 2 * jj
        score_step(j + 1, 1)
        softmax_step(j, 0, masked=False)
        score_step(j + 2, 0)
        softmax_step(j + 1, 1, masked=False)
        return carry

    lax.fori_loop(0, i // 2, off_diag_pair, 0)

    @pl.when(i % 2 == 0)
    def _():
        softmax_step(i, 0, masked=True)

    @pl.when(i % 2 == 1)
    def _():
        score_step(i, 1)
        softmax_step(i - 1, 0, masked=False)
        softmax_step(i, 1, masked=True)

    lam = (jnp.exp(jnp.sum(lq1_ref[...] * lk1_ref[...], axis=-1, keepdims=True))
           - jnp.exp(jnp.sum(lq2_ref[...] * lk2_ref[...], axis=-1, keepdims=True)) + LAM_INIT)
    gain = sw_ref[...] * (1.0 - LAM_INIT)
    for hd in range(n_hd):
        o = acc_ref[hd, 0:HEAD_W] / acc_ref[hd, HEAD_W:HEAD_W + 1]
        o = o[:, 0:TQ] - lam * o[:, TQ:2 * TQ]
        ms = jnp.mean(o * o, axis=0, keepdims=True)
        o_ref[hd, 0] = (o * lax.rsqrt(ms + EPS) * gain).T.astype(BF16)


def _diff_attn(proj4, vt, lam_q1, lam_k1, lam_q2, lam_k2, subln_w):
    _, b, s, _ = proj4.shape
    const = lambda *_: (0, 0)
    lam_spec = pl.BlockSpec((1, DIFF_HEAD_DIM), const)
    return pl.pallas_call(
        _diff_attn_kernel,
        grid=(b, s // TQ),
        in_specs=[
            pl.BlockSpec((N_HEADS, 1, TQ, HEAD_W), lambda bi, qi: (SLOT_DQ, bi, qi, 0)),
            pl.BlockSpec((N_HEADS, 1, s, HEAD_W), lambda bi, qi: (SLOT_DK, bi, 0, 0)),
            pl.BlockSpec((N_HEADS, V_ROWS, s), lambda bi, qi: (0, 0, bi)),
            lam_spec, lam_spec, lam_spec, lam_spec,
            pl.BlockSpec((HEAD_W, 1), const),
        ],
        out_specs=pl.BlockSpec((N_HEADS, 1, TQ, HEAD_W), lambda bi, qi: (0, bi, qi, 0)),
        out_shape=jax.ShapeDtypeStruct((N_HEADS, b, s, HEAD_W), BF16),
        scratch_shapes=[
            pltpu.VMEM((N_HEADS, 2 * TQ, HEAD_W), BF16),
            pltpu.VMEM((2, N_HEADS, TK, 2 * TQ), F32),
            pltpu.VMEM((N_HEADS, 1, 2 * TQ), F32),
            pltpu.VMEM((N_HEADS, V_ROWS, 2 * TQ), F32),
        ],
        compiler_params=pltpu.CompilerParams(
            dimension_semantics=("parallel", "parallel"), vmem_limit_bytes=VMEM_LIMIT),
        name="diff_attn",
    )(proj4, proj4, vt, lam_q1, lam_k1, lam_q2, lam_k2, subln_w)


def _hgrn_constants():
    n = CHUNK
    t = np.arange(n)[:, None]
    s = np.arange(n)[None, :]
    ltri = (s <= t)
    masks = [(s == t)]
    fine, roles = [], []
    for lvl in range(N_LEVELS):
        hs = n >> (lvl + 1)
        blk = t // (2 * hs)
        mid = blk * 2 * hs + hs - 1
        is_q = (t % (2 * hs)) >= hs
        if lvl >= N_COARSE:
            fine.append(np.where(is_q, (s > mid) & (s <= t), (s > t) & (s <= mid)))
        masks.append((blk == (s // (2 * hs))) & is_q & ((s % (2 * hs)) < hs))
        roles.append(np.broadcast_to(np.where(is_q, 1.0, -1.0), (n, HEAD_W)))
    ltri = ltri.astype(np.float32)
    wfine = np.concatenate(fine, axis=0).astype(np.float32)
    masks = np.stack(masks).astype(np.float32)
    roles = np.stack(roles).astype(np.float32)
    return ltri, wfine, masks, roles


def _hgrn_kernel(q_ref, k_ref, v_ref, gate_ref, g_ref, ltri_ref, wfine_ref, mask_ref, role_ref,
                 nw_ref, o_ref, st_ref):
    n_hd = q_ref.shape[0]
    n_chunks = q_ref.shape[2] // CHUNK
    st_ref[...] = jnp.zeros(st_ref.shape, F32)
    ltri = ltri_ref[...]
    wfine = wfine_ref[...]
    nw = nw_ref[...]

    def exponents(g):
        g_hi, g_lo = _split_bf16(g)
        return _dot(ltri, g_hi) + _dot(ltri, g_lo), _dot(wfine, g_hi)

    def decays(b, fine):
        def row_bcast(r, n):
            return jnp.broadcast_to(b[r:r + 1, :], (n, HEAD_W))

        e_b = jnp.exp2(b)
        e_u = jnp.exp2(row_bcast(CHUNK - 1, CHUNK) - b)
        z = []
        for lvl in range(N_COARSE):
            hs = CHUNK >> (lvl + 1)
            b_mid = jnp.concatenate([row_bcast(blk * 2 * hs + hs - 1, 2 * hs)
                                     for blk in range(CHUNK // (2 * hs))], axis=0)
            z.append(jnp.exp2((b - b_mid) * role_ref[lvl]))
        fine = jnp.exp2(fine)
        z += [fine[i * CHUNK:(i + 1) * CHUNK] for i in range(N_LEVELS - N_COARSE)]
        return e_b, e_u, z

    def group(gi, carry):
        base = gi * (HGRN_GROUP * CHUNK)
        items = [(hd, pl.ds(pl.multiple_of(base + u * CHUNK, CHUNK), CHUNK))
                 for u in range(HGRN_GROUP) for hd in range(n_hd)]
        q_b = [q_ref[hd, 0, rows, :] for hd, rows in items]
        k_b = [k_ref[hd, 0, rows, :] for hd, rows in items]
        v_b = [v_ref[hd, 0, rows, :] for hd, rows in items]
        expo = [exponents(g_ref[hd, 0, rows, :]) for hd, rows in items]
        dec = [decays(*e) for e in expo]

        xz, q_dec, k_dec = [], [], []
        for n, (e_b, e_u, z) in enumerate(dec):
            q = q_b[n].astype(F32)
            k = k_b[n].astype(F32)
            xz.append([(jnp.where(role_ref[lvl] > 0.0, q, k) * z[lvl]).astype(BF16)
                       for lvl in range(N_LEVELS)])
            q_dec.append((q * e_b).astype(BF16))
            k_dec.append((k * e_u).astype(BF16))

        pair = [[_nt_dot(q_b[n], k_b[n])] + [_nt_dot(x, x) for x in xz[n]]
                for n in range(len(items))]
        kv = [_tn_dot(v_b[n], k_dec[n]) for n in range(len(items))]
        att = [functools.reduce(lambda a, c: a + c,
                                [mask_ref[lvl] * p[lvl] for lvl in range(N_LEVELS + 1)]).astype(BF16)
               for p in pair]
        o = [_dot(att[n], v_b[n]) for n in range(len(items))]

        st = [st_ref[hd] for hd in range(n_hd)]
        for n, (hd, rows) in enumerate(items):
            o[n] = o[n] + _nt_dot(q_dec[n], st[hd].astype(BF16))
            st[hd] = st[hd] * dec[n][0][CHUNK - 1:CHUNK, :] + kv[n]
        for hd in range(n_hd):
            st_ref[hd] = st[hd]

        for n, (hd, rows) in enumerate(items):
            ms = jnp.mean(o[n] * o[n], axis=-1, keepdims=True)
            y = o[n] * lax.rsqrt(ms + EPS) * nw * gate_ref[hd, 0, rows, :].astype(F32)
            o_ref[hd, 0, rows, :] = y.astype(BF16)
        return carry

    lax.fori_loop(0, n_chunks // HGRN_GROUP, group, 0)


def _hgrn2(proj4, g4, norm_w):
    _, b, s, _ = proj4.shape
    ltri, wfine, masks, roles = _hgrn_constants()
    heads_spec = lambda sec: pl.BlockSpec((N_HEADS, 1, s, HEAD_W), lambda bi: (sec, bi, 0, 0))
    return pl.pallas_call(
        _hgrn_kernel,
        grid=(b,),
        in_specs=[
            heads_spec(SLOT_HQ), heads_spec(SLOT_HK), heads_spec(SLOT_HI), heads_spec(SLOT_HG),
            heads_spec(0),
            pl.BlockSpec(ltri.shape, lambda *_: (0, 0)),
            pl.BlockSpec(wfine.shape, lambda *_: (0, 0)),
            pl.BlockSpec(masks.shape, lambda *_: (0, 0, 0)),
            pl.BlockSpec(roles.shape, lambda *_: (0, 0, 0)),
            pl.BlockSpec((1, HEAD_W), lambda *_: (0, 0)),
        ],
        out_specs=heads_spec(0),
        out_shape=jax.ShapeDtypeStruct((N_HEADS, b, s, HEAD_W), BF16),
        scratch_shapes=[pltpu.VMEM((N_HEADS, HEAD_W, HEAD_W), F32)],
        compiler_params=pltpu.CompilerParams(
            dimension_semantics=("parallel",), vmem_limit_bytes=VMEM_LIMIT),
        name="hgrn2",
    )(proj4, proj4, proj4, proj4, g4, jnp.asarray(ltri, BF16), jnp.asarray(wfine, BF16),
      jnp.asarray(masks), jnp.asarray(roles), norm_w)


def _ff_chunks():
    chunks, c0 = [], 0
    while c0 < D_FF:
        fc = min(FF_CHUNK, D_FF - c0)
        chunks.append((c0, fc))
        c0 += fc
    return chunks


def _out_ffn_kernel(tiles_per_seq, x_ref, od_ref, oh_ref, wout_ref, ln2_ref, wup_ref, cw_ref,
                    cb_ref, wdn_ref, out_ref, tail_ref, act_ref):
    i = pl.program_id(0)
    mix = jnp.concatenate([od_ref[hd] for hd in range(N_HEADS)]
                          + [oh_ref[hd] for hd in range(N_HEADS)], axis=1)
    x1 = x_ref[...] + _dot(mix, wout_ref[...])
    ms = jnp.mean(x1 * x1, axis=-1, keepdims=True)
    h2 = (x1 * lax.rsqrt(ms + EPS) * ln2_ref[...]).astype(BF16)

    seq_start = (i % tiles_per_seq) == 0
    tm = x1.shape[0]
    for c0, fc in _ff_chunks():
        cols = slice(c0, c0 + fc)
        u = _dot(h2, wup_ref[:, cols])
        v = _dot(h2, wup_ref[:, D_FF + c0:D_FF + c0 + fc])
        tail = jnp.where(seq_start, 0.0, tail_ref[:, cols])
        tail_ref[:, cols] = u[tm - SUBLANES:, :]
        ext = jnp.concatenate([tail, u], axis=0)
        u1 = pltpu.roll(ext, 1, 0)[SUBLANES:]
        u2 = pltpu.roll(ext, 2, 0)[SUBLANES:]
        cw = cw_ref[:, cols]
        c = cb_ref[:, cols] + u2 * cw[0:1] + u1 * cw[1:2] + u * cw[2:3]
        act_ref[:, cols] = (c * _sigmoid(c) * v).astype(BF16)
    out_ref[...] = x1 + _dot(act_ref[...], wdn_ref[...])


def _out_ffn(x2d, o_diff, o_hgrn, w_out, ln2_w, w_up, conv_w, conv_b, w_down, seq_len):
    t = x2d.shape[0]
    const = lambda *_: (0, 0)
    resident = functools.partial(pl.BlockSpec, index_map=const, pipeline_mode=pl.Buffered(1))
    return pl.pallas_call(
        functools.partial(_out_ffn_kernel, seq_len // TM_FFN),
        grid=(t // TM_FFN,),
        in_specs=[
            pl.BlockSpec((TM_FFN, D_MODEL), lambda i: (i, 0)),
            pl.BlockSpec((N_HEADS, TM_FFN, HEAD_W), lambda i: (0, i, 0)),
            pl.BlockSpec((N_HEADS, TM_FFN, HEAD_W), lambda i: (0, i, 0)),
            resident((D_MODEL, D_MODEL)),
            pl.BlockSpec((1, D_MODEL), const),
            resident((D_MODEL, 2 * D_FF)),
            pl.BlockSpec((CONV_WIDTH, D_FF), const),
            pl.BlockSpec((1, D_FF), const),
            resident((D_FF, D_MODEL)),
        ],
        out_specs=pl.BlockSpec((TM_FFN, D_MODEL), lambda i: (i, 0)),
        out_shape=jax.ShapeDtypeStruct((t, D_MODEL), F32),
        scratch_shapes=[
            pltpu.VMEM((SUBLANES, D_FF), F32),
            pltpu.VMEM((TM_FFN, D_FF), BF16),
        ],
        compiler_params=pltpu.CompilerParams(
            dimension_semantics=("arbitrary",), vmem_limit_bytes=VMEM_LIMIT),
        name="out_ffn",
    )(x2d, o_diff, o_hgrn, w_out, ln2_w, w_up, conv_w, conv_b, w_down)


def kernel(x, ln1_w, w_in, q_norm_w, k_norm_w, lam_q1, lam_k1, lam_q2, lam_k2, diff_subln_w,
           hgrn_lb_logits, hgrn_norm_w, w_out, ln2_w, w_up, conv_w, conv_b, w_down):
    b, s, d = x.shape
    depth = ln1_w.shape[0]
    assert depth == 1 and d == D_MODEL and s % TM_FFN == 0 and s % TQ == 0
    t = b * s
    x2d = x.reshape(t, d)
    l = 0
    tile2 = lambda w: jnp.concatenate([w, w], axis=-1)[None, :]

    proj, g, vt = _in_proj(x2d, ln1_w[l][None, :], w_in[l].astype(BF16), tile2(q_norm_w[l]),
                           tile2(k_norm_w[l]), hgrn_lb_logits)
    proj4 = proj.reshape(N_SLOT * N_HEADS, b, s, HEAD_W)
    g4 = g.reshape(N_HEADS, b, s, HEAD_W)

    o_diff = _diff_attn(proj4, vt, lam_q1[l][None, :], lam_k1[l][None, :], lam_q2[l][None, :],
                        lam_k2[l][None, :], diff_subln_w[l][:, None])
    o_hgrn = _hgrn2(proj4, g4, hgrn_norm_w[l][None, :])

    out = _out_ffn(x2d, o_diff.reshape(N_HEADS, t, HEAD_W), o_hgrn.reshape(N_HEADS, t, HEAD_W),
                   w_out[l].astype(BF16), ln2_w[l][None, :], w_up[l].astype(BF16), conv_w[l],
                   conv_b[l][None, :], w_down[l].astype(BF16), s)
    return out.reshape(b, s, d)
```

```python
import functools
import math

import numpy as np
import jax
import jax.numpy as jnp
from jax import lax
from jax.experimental import pallas as pl
from jax.experimental.pallas import tpu as pltpu

F32 = jnp.float32
BF16 = jnp.bfloat16

D_MODEL = 1024
CHUNK = 64
HEAD_W = 128
DIFF_HEAD_DIM = 64
N_HEADS = 4
SEC_W = N_HEADS * HEAD_W
N_SEC = 7
IN_COLS = N_SEC * SEC_W
D_FF = 2816
CONV_WIDTH = 3
EPS = 1e-6
LAM_INIT = 0.8 - 0.6 * math.exp(-0.3 * 0)
LOG2E = math.log2(math.e)

SEC_DQ, SEC_DK, SEC_DV, SEC_HQ, SEC_HK, SEC_HI, SEC_HG = range(N_SEC)
SLOT_DQ, SLOT_DK, SLOT_HQ, SLOT_HK, SLOT_HI, SLOT_HG = range(6)
N_SLOT = 6
BF16_SUBLANES = 16
V_ROWS = HEAD_W + BF16_SUBLANES

LANES = 128
SUBLANES = 8
MXU_W = 256
TM_PROJ = 1024
TQ = 256
TK = 256
TM_FFN = 512
FF_CHUNK = 512
VMEM_LIMIT = 56 * 1024 * 1024

HGRN_CHUNK = 128
N_LEVELS = 7
N_COARSE = 4
HGRN_GROUP = 2


def _nt_dot(a, b):
    return lax.dot_general(a, b, (((1,), (1,)), ((), ())), preferred_element_type=F32)


def _tn_dot(a, b):
    return lax.dot_general(a, b, (((0,), (0,)), ((), ())), preferred_element_type=F32)


def _dot(a, b):
    return jnp.dot(a, b, preferred_element_type=F32)


def _split_bf16(x):
    hi = x.astype(BF16)
    lo = (x - hi.astype(F32)).astype(BF16)
    return hi, lo


def _sigmoid(x):
    return 1.0 / (1.0 + jnp.exp(-x))


def _in_proj_kernel(x_ref, ln1_ref, w_ref, qw_ref, kw_ref, lbl_ref, proj_ref, g_ref, vt_ref):
    x = x_ref[...]
    ms = jnp.mean(x * x, axis=-1, keepdims=True)
    h = (x * lax.rsqrt(ms + EPS) * ln1_ref[...]).astype(BF16)

    r = lax.broadcasted_iota(jnp.int32, (MXU_W, MXU_W), 0) // DIFF_HEAD_DIM
    c = lax.broadcasted_iota(jnp.int32, (MXU_W, MXU_W), 1) // DIFF_HEAD_DIM
    grp = jnp.where(r == c, 1.0, 0.0).astype(BF16)

    def head_cols(a, hd):
        return a[:, hd * HEAD_W:(hd + 1) * HEAD_W]

    def store_heads(slot, a):
        for hd in range(N_HEADS):
            proj_ref[slot * N_HEADS + hd] = head_cols(a, hd).astype(BF16)

    def project(j):
        return _dot(h, w_ref[:, j * SEC_W:(j + 1) * SEC_W])

    acc = [None] * N_SEC
    halves = [slice(c0, c0 + MXU_W) for c0 in range(0, SEC_W, MXU_W)]
    ss = {}
    for sec in (SEC_DQ, SEC_DK):
        acc[sec] = project(sec)
        ss[sec] = [_dot((acc[sec][:, cols] * acc[sec][:, cols]).astype(BF16), grp)
                   for cols in halves]
    for sec in (SEC_HK, SEC_HG, SEC_DV, SEC_HQ, SEC_HI):
        acc[sec] = project(sec)

    q_gain = qw_ref[...] * (DIFF_HEAD_DIM ** -0.5 * LOG2E)
    for sec, slot, w_norm in ((SEC_DQ, SLOT_DQ, q_gain), (SEC_DK, SLOT_DK, kw_ref[...])):
        w2 = jnp.concatenate([w_norm] * (MXU_W // HEAD_W), axis=1)
        y = [acc[sec][:, cols] * lax.rsqrt(ss[sec][n] * (1.0 / DIFF_HEAD_DIM) + EPS) * w2
             for n, cols in enumerate(halves)]
        store_heads(slot, jnp.concatenate(y, axis=1))

    for hd in range(N_HEADS):
        vt_ref[hd, 0:HEAD_W, :] = head_cols(acc[SEC_DV], hd).T.astype(BF16)
        vt_ref[hd, HEAD_W:V_ROWS, :] = jnp.ones((V_ROWS - HEAD_W, x.shape[0]), BF16)

    store_heads(SLOT_HQ, acc[SEC_HQ])
    store_heads(SLOT_HI, acc[SEC_HI])

    lbl = lbl_ref[...]
    e = jnp.exp(lbl - jnp.max(lbl, axis=0, keepdims=True))
    lb = e[0:1] / jnp.sum(e, axis=0, keepdims=True)
    f = lb + (1.0 - lb) * _sigmoid(acc[SEC_HK])
    log2_f = jnp.log(f) * LOG2E
    for hd in range(N_HEADS):
        g_ref[hd] = head_cols(log2_f, hd)
    store_heads(SLOT_HK, 1.0 - f)

    store_heads(SLOT_HG, acc[SEC_HG] * _sigmoid(acc[SEC_HG]))


def _in_proj(x2d, ln1_w, w_in, q_norm_w, k_norm_w, lb_logits):
    t = x2d.shape[0]
    const = lambda *_: (0, 0)
    return pl.pallas_call(
        _in_proj_kernel,
        grid=(t // TM_PROJ,),
        in_specs=[
            pl.BlockSpec((TM_PROJ, D_MODEL), lambda i: (i, 0)),
            pl.BlockSpec((1, D_MODEL), const),
            pl.BlockSpec((D_MODEL, IN_COLS), const, pipeline_mode=pl.Buffered(1)),
            pl.BlockSpec((1, HEAD_W), const),
            pl.BlockSpec((1, HEAD_W), const),
            pl.BlockSpec(lb_logits.shape, const),
        ],
        out_specs=[
            pl.BlockSpec((N_SLOT * N_HEADS, TM_PROJ, HEAD_W), lambda i: (0, i, 0)),
            pl.BlockSpec((N_HEADS, TM_PROJ, HEAD_W), lambda i: (0, i, 0)),
            pl.BlockSpec((N_HEADS, V_ROWS, TM_PROJ), lambda i: (0, 0, i)),
        ],
        out_shape=[
            jax.ShapeDtypeStruct((N_SLOT * N_HEADS, t, HEAD_W), BF16),
            jax.ShapeDtypeStruct((N_HEADS, t, HEAD_W), F32),
            jax.ShapeDtypeStruct((N_HEADS, V_ROWS, t), BF16),
        ],
        compiler_params=pltpu.CompilerParams(
            dimension_semantics=("parallel",), vmem_limit_bytes=VMEM_LIMIT),
        name="in_proj",
    )(x2d, ln1_w, w_in, q_norm_w, k_norm_w, lb_logits)


NEG_BIG = -1e30


def _diff_attn_kernel(q_ref, k_ref, vt_ref, lq1_ref, lk1_ref, lq2_ref, lk2_ref, sw_ref,
                      o_ref, qs_ref, s_ref, m_ref, acc_ref):
    i = pl.program_id(1)
    n_hd = q_ref.shape[0]
    lane = lax.broadcasted_iota(jnp.int32, (TQ, HEAD_W), 1)
    for hd in range(n_hd):
        q = q_ref[hd, 0]
        zero = jnp.zeros_like(q)
        qs_ref[hd, 0:TQ] = jnp.where(lane < DIFF_HEAD_DIM, q, zero)
        qs_ref[hd, TQ:2 * TQ] = jnp.where(lane >= DIFF_HEAD_DIM, q, zero)

    m_ref[...] = jnp.full(m_ref.shape, NEG_BIG, F32)
    acc_ref[...] = jnp.zeros(acc_ref.shape, F32)

    ck = lax.broadcasted_iota(jnp.int32, (TK, 2 * TQ), 0) // CHUNK
    rq = (lax.broadcasted_iota(jnp.int32, (TK, 2 * TQ), 1) % TQ) // CHUNK
    diag_mask = ck <= rq

    heads = range(n_hd)

    def kv_rows(j):
        return pl.ds(pl.multiple_of(j * TK, TK), TK)

    def score_step(j, slot):
        rows = kv_rows(j)
        for hd in heads:
            s_ref[slot, hd] = _nt_dot(k_ref[hd, 0, rows, :], qs_ref[hd])

    def softmax_step(j, slot, masked):
        rows = kv_rows(j)
        p_all, alpha_all = [], []
        for hd in heads:
            s = s_ref[slot, hd]
            if masked:
                s = jnp.where(diag_mask, s, NEG_BIG)
            m_old = m_ref[hd]
            m_new = jnp.maximum(m_old, jnp.max(s, axis=0, keepdims=True))
            m_ref[hd] = m_new
            alpha_all.append(jnp.exp2(m_old - m_new))
            p_all.append(jnp.exp2((s - m_new).astype(BF16)))
        pv_all = [_dot(vt_ref[hd, :, rows], p_all[hd]) for hd in heads]
        for hd in heads:
            acc_ref[hd] = alpha_all[hd] * acc_ref[hd] + pv_all[hd]

    score_step(0, 0)

    def off_diag_pair(jj, carry):
        j = 2 * jj
        score_step(j + 1, 1)
        softmax_step(j, 0, masked=False)
        score_step(j + 2, 0)
        softmax_step(j + 1, 1, masked=False)
        return carry

    lax.fori_loop(0, i // 2, off_diag_pair, 0)

    @pl.when(i % 2 == 0)
    def _():
        softmax_step(i, 0, masked=True)

    @pl.when(i % 2 == 1)
    def _():
        score_step(i, 1)
        softmax_step(i - 1, 0, masked=False)
        softmax_step(i, 1, masked=True)

    lam = (jnp.exp(jnp.sum(lq1_ref[...] * lk1_ref[...], axis=-1, keepdims=True))
           - jnp.exp(jnp.sum(lq2_ref[...] * lk2_ref[...], axis=-1, keepdims=True)) + LAM_INIT)
    gain = sw_ref[...] * (1.0 - LAM_INIT)
    for hd in range(n_hd):
        o = acc_ref[hd, 0:HEAD_W] / acc_ref[hd, HEAD_W:HEAD_W + 1]
        o = o[:, 0:TQ] - lam * o[:, TQ:2 * TQ]
        ms = jnp.mean(o * o, axis=0, keepdims=True)
        o_ref[hd, 0] = (o * lax.rsqrt(ms + EPS) * gain).T.astype(BF16)


def _diff_attn(proj4, vt, lam_q1, lam_k1, lam_q2, lam_k2, subln_w):
    _, b, s, _ = proj4.shape
    const = lambda *_: (0, 0)
    lam_spec = pl.BlockSpec((1, DIFF_HEAD_DIM), const)
    return pl.pallas_call(
        _diff_attn_kernel,
        grid=(b, s // TQ),
        in_specs=[
            pl.BlockSpec((N_HEADS, 1, TQ, HEAD_W), lambda bi, qi: (SLOT_DQ, bi, qi, 0)),
            pl.BlockSpec((N_HEADS, 1, s, HEAD_W), lambda bi, qi: (SLOT_DK, bi, 0, 0)),
            pl.BlockSpec((N_HEADS, V_ROWS, s), lambda bi, qi: (0, 0, bi)),
            lam_spec, lam_spec, lam_spec, lam_spec,
            pl.BlockSpec((HEAD_W, 1), const),
        ],
        out_specs=pl.BlockSpec((N_HEADS, 1, TQ, HEAD_W), lambda bi, qi: (0, bi, qi, 0)),
        out_shape=jax.ShapeDtypeStruct((N_HEADS, b, s, HEAD_W), BF16),
        scratch_shapes=[
            pltpu.VMEM((N_HEADS, 2 * TQ, HEAD_W), BF16),
            pltpu.VMEM((2, N_HEADS, TK, 2 * TQ), F32),
            pltpu.VMEM((N_HEADS, 1, 2 * TQ), F32),
            pltpu.VMEM((N_HEADS, V_ROWS, 2 * TQ), F32),
        ],
        compiler_params=pltpu.CompilerParams(
            dimension_semantics=("parallel", "parallel"), vmem_limit_bytes=VMEM_LIMIT),
        name="diff_attn",
    )(proj4, proj4, vt, lam_q1, lam_k1, lam_q2, lam_k2, subln_w)


def _hgrn_constants():
    n = HGRN_CHUNK
    t = np.arange(n)[:, None]
    s = np.arange(n)[None, :]
    ltri = (s <= t)
    masks = [(s == t)]
    fine, roles = [], []
    for lvl in range(N_LEVELS):
        hs = n >> (lvl + 1)
        blk = t // (2 * hs)
        mid = blk * 2 * hs + hs - 1
        is_q = (t % (2 * hs)) >= hs
        if lvl >= N_COARSE:
            fine.append(np.where(is_q, (s > mid) & (s <= t), (s > t) & (s <= mid)))
        masks.append((blk == (s // (2 * hs))) & is_q & ((s % (2 * hs)) < hs))
        roles.append(np.broadcast_to(np.where(is_q, 1.0, -1.0), (n, HEAD_W)))
    ltri = ltri.astype(np.float32)
    wfine = np.concatenate(fine, axis=0).astype(np.float32)
    masks = np.stack(masks).astype(np.float32)
    roles = np.stack(roles).astype(np.float32)
    return ltri, wfine, masks, roles


def _hgrn_kernel(q_ref, k_ref, v_ref, gate_ref, g_ref, ltri_ref, wfine_ref, mask_ref, role_ref,
                 nw_ref, o_ref, st_ref):
    n_hd = q_ref.shape[0]
    n_chunks = q_ref.shape[2] // HGRN_CHUNK
    st_ref[...] = jnp.zeros(st_ref.shape, F32)
    ltri = ltri_ref[...]
    wfine = wfine_ref[...]
    nw = nw_ref[...]

    def exponents(g):
        g_hi, g_lo = _split_bf16(g)
        return _dot(ltri, g_hi) + _dot(ltri, g_lo), _dot(wfine, g_hi)

    def decays(b, fine):
        def row_bcast(r, n):
            return jnp.broadcast_to(b[r:r + 1, :], (n, HEAD_W))

        e_b = jnp.exp2(b)
        e_u = jnp.exp2(row_bcast(HGRN_CHUNK - 1, HGRN_CHUNK) - b)
        z = []
        for lvl in range(N_COARSE):
            hs = HGRN_CHUNK >> (lvl + 1)
            b_mid = jnp.concatenate([row_bcast(blk * 2 * hs + hs - 1, 2 * hs)
                                     for blk in range(HGRN_CHUNK // (2 * hs))], axis=0)
            z.append(jnp.exp2((b - b_mid) * role_ref[lvl]))
        fine = jnp.exp2(fine)
        z += [fine[i * HGRN_CHUNK:(i + 1) * HGRN_CHUNK] for i in range(N_LEVELS - N_COARSE)]
        return e_b, e_u, z

    def group(gi, carry):
        base = gi * (HGRN_GROUP * HGRN_CHUNK)
        items = [(hd, pl.ds(pl.multiple_of(base + u * HGRN_CHUNK, HGRN_CHUNK), HGRN_CHUNK))
                 for u in range(HGRN_GROUP) for hd in range(n_hd)]
        q_b = [q_ref[hd, 0, rows, :] for hd, rows in items]
        k_b = [k_ref[hd, 0, rows, :] for hd, rows in items]
        v_b = [v_ref[hd, 0, rows, :] for hd, rows in items]
        expo = [exponents(g_ref[hd, 0, rows, :]) for hd, rows in items]
        dec = [decays(*e) for e in expo]

        def by_role(lvl, q, k):
            if lvl >= N_COARSE:
                return jnp.where(role_ref[lvl] > 0.0, q, k)
            hs = HGRN_CHUNK >> (lvl + 1)
            return jnp.concatenate([(q if part % 2 else k)[part * hs:(part + 1) * hs]
                                    for part in range(HGRN_CHUNK // hs)], axis=0)

        xz, q_dec, k_dec = [], [], []
        for n, (e_b, e_u, z) in enumerate(dec):
            q = q_b[n].astype(F32)
            k = k_b[n].astype(F32)
            xz.append([(by_role(lvl, q, k) * z[lvl]).astype(BF16) for lvl in range(N_LEVELS)])
            q_dec.append((q * e_b).astype(BF16))
            k_dec.append((k * e_u).astype(BF16))

        pair = [[_nt_dot(q_b[n], k_b[n])] + [_nt_dot(x, x) for x in xz[n]]
                for n in range(len(items))]
        kv = [_tn_dot(v_b[n], k_dec[n]) for n in range(len(items))]
        owns = [mask_ref[lvl] > 0.5 for lvl in range(N_LEVELS + 1)]
        att = []
        for p in pair:
            a = jnp.where(owns[0], p[0], 0.0)
            for lvl in range(1, N_LEVELS + 1):
                a = jnp.where(owns[lvl], p[lvl], a)
            att.append(a.astype(BF16))
        o = [_dot(att[n], v_b[n]) for n in range(len(items))]

        st = [st_ref[hd] for hd in range(n_hd)]
        for n, (hd, rows) in enumerate(items):
            o[n] = o[n] + _nt_dot(q_dec[n], st[hd].astype(BF16))
            st[hd] = st[hd] * dec[n][0][HGRN_CHUNK - 1:HGRN_CHUNK, :] + kv[n]
        for hd in range(n_hd):
            st_ref[hd] = st[hd]

        for n, (hd, rows) in enumerate(items):
            ms = jnp.mean(o[n] * o[n], axis=-1, keepdims=True)
            y = o[n] * lax.rsqrt(ms + EPS) * nw * gate_ref[hd, 0, rows, :].astype(F32)
            o_ref[hd, 0, rows, :] = y.astype(BF16)
        return carry

    lax.fori_loop(0, n_chunks // HGRN_GROUP, group, 0)


def _hgrn2(proj4, g4, norm_w):
    _, b, s, _ = proj4.shape
    ltri, wfine, masks, roles = _hgrn_constants()
    heads_spec = lambda sec: pl.BlockSpec((N_HEADS, 1, s, HEAD_W), lambda bi: (sec, bi, 0, 0))
    return pl.pallas_call(
        _hgrn_kernel,
        grid=(b,),
        in_specs=[
            heads_spec(SLOT_HQ), heads_spec(SLOT_HK), heads_spec(SLOT_HI), heads_spec(SLOT_HG),
            heads_spec(0),
            pl.BlockSpec(ltri.shape, lambda *_: (0, 0)),
            pl.BlockSpec(wfine.shape, lambda *_: (0, 0)),
            pl.BlockSpec(masks.shape, lambda *_: (0, 0, 0)),
            pl.BlockSpec(roles.shape, lambda *_: (0, 0, 0)),
            pl.BlockSpec((1, HEAD_W), lambda *_: (0, 0)),
        ],
        out_specs=heads_spec(0),
        out_shape=jax.ShapeDtypeStruct((N_HEADS, b, s, HEAD_W), BF16),
        scratch_shapes=[pltpu.VMEM((N_HEADS, HEAD_W, HEAD_W), F32)],
        compiler_params=pltpu.CompilerParams(
            dimension_semantics=("parallel",), vmem_limit_bytes=VMEM_LIMIT),
        name="hgrn2",
    )(proj4, proj4, proj4, proj4, g4, jnp.asarray(ltri, BF16), jnp.asarray(wfine, BF16),
      jnp.asarray(masks), jnp.asarray(roles), norm_w)


def _ff_chunks():
    chunks, c0 = [], 0
    while c0 < D_FF:
        fc = min(FF_CHUNK, D_FF - c0)
        chunks.append((c0, fc))
        c0 += fc
    return chunks


def _out_ffn_kernel(tiles_per_seq, x_ref, od_ref, oh_ref, wout_ref, ln2_ref, wup_ref, cw_ref,
                    cb_ref, wdn_ref, out_ref, tail_ref, act_ref):
    i = pl.program_id(0)
    mix = jnp.concatenate([od_ref[hd] for hd in range(N_HEADS)]
                          + [oh_ref[hd] for hd in range(N_HEADS)], axis=1)
    x1 = x_ref[...] + _dot(mix, wout_ref[...])
    ms = jnp.mean(x1 * x1, axis=-1, keepdims=True)
    h2 = (x1 * lax.rsqrt(ms + EPS) * ln2_ref[...]).astype(BF16)

    seq_start = (i % tiles_per_seq) == 0
    tm = x1.shape[0]
    for c0, fc in _ff_chunks():
        cols = slice(c0, c0 + fc)
        u = _dot(h2, wup_ref[:, cols])
        v = _dot(h2, wup_ref[:, D_FF + c0:D_FF + c0 + fc])
        tail = jnp.where(seq_start, 0.0, tail_ref[:, cols])
        tail_ref[:, cols] = u[tm - SUBLANES:, :]
        ext = jnp.concatenate([tail, u], axis=0)
        u1 = pltpu.roll(ext, 1, 0)[SUBLANES:]
        u2 = pltpu.roll(ext, 2, 0)[SUBLANES:]
        cw = cw_ref[:, cols]
        c = cb_ref[:, cols] + u2 * cw[0:1] + u1 * cw[1:2] + u * cw[2:3]
        act_ref[:, cols] = (c * _sigmoid(c) * v).astype(BF16)
    out_ref[...] = x1 + _dot(act_ref[...], wdn_ref[...])


def _out_ffn(x2d, o_diff, o_hgrn, w_out, ln2_w, w_up, conv_w, conv_b, w_down, seq_len):
    t = x2d.shape[0]
    const = lambda *_: (0, 0)
    resident = functools.partial(pl.BlockSpec, index_map=const, pipeline_mode=pl.Buffered(1))
    return pl.pallas_call(
        functools.partial(_out_ffn_kernel, seq_len // TM_FFN),
        grid=(t // TM_FFN,),
        in_specs=[
            pl.BlockSpec((TM_FFN, D_MODEL), lambda i: (i, 0)),
            pl.BlockSpec((N_HEADS, TM_FFN, HEAD_W), lambda i: (0, i, 0)),
            pl.BlockSpec((N_HEADS, TM_FFN, HEAD_W), lambda i: (0, i, 0)),
            resident((D_MODEL, D_MODEL)),
            pl.BlockSpec((1, D_MODEL), const),
            resident((D_MODEL, 2 * D_FF)),
            pl.BlockSpec((CONV_WIDTH, D_FF), const),
            pl.BlockSpec((1, D_FF), const),
            resident((D_FF, D_MODEL)),
        ],
        out_specs=pl.BlockSpec((TM_FFN, D_MODEL), lambda i: (i, 0)),
        out_shape=jax.ShapeDtypeStruct((t, D_MODEL), F32),
        scratch_shapes=[
            pltpu.VMEM((SUBLANES, D_FF), F32),
            pltpu.VMEM((TM_FFN, D_FF), BF16),
        ],
        compiler_params=pltpu.CompilerParams(
            dimension_semantics=("arbitrary",), vmem_limit_bytes=VMEM_LIMIT),
        name="out_ffn",
    )(x2d, o_diff, o_hgrn, w_out, ln2_w, w_up, conv_w, conv_b, w_down)


def kernel(x, ln1_w, w_in, q_norm_w, k_norm_w, lam_q1, lam_k1, lam_q2, lam_k2, diff_subln_w,
           hgrn_lb_logits, hgrn_norm_w, w_out, ln2_w, w_up, conv_w, conv_b, w_down):
    b, s, d = x.shape
    depth = ln1_w.shape[0]
    assert depth == 1 and d == D_MODEL and s % TM_FFN == 0 and s % TQ == 0
    t = b * s
    x2d = x.reshape(t, d)
    l = 0
    tile2 = lambda w: jnp.concatenate([w, w], axis=-1)[None, :]

    proj, g, vt = _in_proj(x2d, ln1_w[l][None, :], w_in[l].astype(BF16), tile2(q_norm_w[l]),
                           tile2(k_norm_w[l]), hgrn_lb_logits)
    proj4 = proj.reshape(N_SLOT * N_HEADS, b, s, HEAD_W)
    g4 = g.reshape(N_HEADS, b, s, HEAD_W)

    o_diff = _diff_attn(proj4, vt, lam_q1[l][None, :], lam_k1[l][None, :], lam_q2[l][None, :],
                        lam_k2[l][None, :], diff_subln_w[l][:, None])
    o_hgrn = _hgrn2(proj4, g4, hgrn_norm_w[l][None, :])

    out = _out_ffn(x2d, o_diff.reshape(N_HEADS, t, HEAD_W), o_hgrn.reshape(N_HEADS, t, HEAD_W),
                   w_out[l].astype(BF16), ln2_w[l][None, :], w_up[l].astype(BF16), conv_w[l],
                   conv_b[l][None, :], w_down[l].astype(BF16), s)
    return out.reshape(b, s, d)
```

```python
import functools
import math

import numpy as np
import jax
import jax.numpy as jnp
from jax import lax
from jax.experimental import pallas as pl
from jax.experimental.pallas import tpu as pltpu

F32 = jnp.float32
BF16 = jnp.bfloat16

D_MODEL = 1024
CHUNK = 64
HEAD_W = 128
DIFF_HEAD_DIM = 64
N_HEADS = 4
SEC_W = N_HEADS * HEAD_W
N_SEC = 7
IN_COLS = N_SEC * SEC_W
D_FF = 2816
CONV_WIDTH = 3
EPS = 1e-6
LAM_INIT = 0.8 - 0.6 * math.exp(-0.3 * 0)
LOG2E = math.log2(math.e)

SEC_DQ, SEC_DK, SEC_DV, SEC_HQ, SEC_HK, SEC_HI, SEC_HG = range(N_SEC)
SLOT_DQ, SLOT_DK, SLOT_HQ, SLOT_HK, SLOT_HI, SLOT_HG = range(6)
N_SLOT = 6
BF16_SUBLANES = 16
V_ROWS = HEAD_W + BF16_SUBLANES

LANES = 128
SUBLANES = 8
MXU_W = 256
TM_PROJ = 1024
TQ = 512
HQ = TQ // 2
TK = 256
TM_FFN = 512
FF_CHUNK = 512
VMEM_LIMIT = 56 * 1024 * 1024

HGRN_CHUNK = 64
N_LEVELS = 6
N_COARSE = 3
HGRN_GROUP = 4


def _nt_dot(a, b):
    return lax.dot_general(a, b, (((1,), (1,)), ((), ())), preferred_element_type=F32)


def _tn_dot(a, b):
    return lax.dot_general(a, b, (((0,), (0,)), ((), ())), preferred_element_type=F32)


def _dot(a, b):
    return jnp.dot(a, b, preferred_element_type=F32)


def _split_bf16(x):
    hi = x.astype(BF16)
    lo = (x - hi.astype(F32)).astype(BF16)
    return hi, lo


def _sigmoid(x):
    return 1.0 / (1.0 + jnp.exp(-x))


def _in_proj_kernel(x_ref, ln1_ref, w_ref, qw_ref, kw_ref, lbl_ref, proj_ref, g_ref, vt_ref):
    x = x_ref[...]
    ms = jnp.mean(x * x, axis=-1, keepdims=True)
    h = (x * lax.rsqrt(ms + EPS) * ln1_ref[...]).astype(BF16)

    r = lax.broadcasted_iota(jnp.int32, (MXU_W, MXU_W), 0) // DIFF_HEAD_DIM
    c = lax.broadcasted_iota(jnp.int32, (MXU_W, MXU_W), 1) // DIFF_HEAD_DIM
    grp = jnp.where(r == c, 1.0, 0.0).astype(BF16)

    def head_cols(a, hd):
        return a[:, hd * HEAD_W:(hd + 1) * HEAD_W]

    def store_heads(slot, a):
        for hd in range(N_HEADS):
            proj_ref[slot * N_HEADS + hd] = head_cols(a, hd).astype(BF16)

    def project(j):
        return _dot(h, w_ref[:, j * SEC_W:(j + 1) * SEC_W])

    acc = [None] * N_SEC
    halves = [slice(c0, c0 + MXU_W) for c0 in range(0, SEC_W, MXU_W)]
    ss = {}
    for sec in (SEC_DQ, SEC_DK):
        acc[sec] = project(sec)
        ss[sec] = [_dot((acc[sec][:, cols] * acc[sec][:, cols]).astype(BF16), grp)
                   for cols in halves]
    for sec in (SEC_HK, SEC_HG, SEC_DV, SEC_HQ, SEC_HI):
        acc[sec] = project(sec)

    q_gain = qw_ref[...] * (DIFF_HEAD_DIM ** -0.5 * LOG2E)
    for sec, slot, w_norm in ((SEC_DQ, SLOT_DQ, q_gain), (SEC_DK, SLOT_DK, kw_ref[...])):
        w2 = jnp.concatenate([w_norm] * (MXU_W // HEAD_W), axis=1)
        y = [acc[sec][:, cols] * lax.rsqrt(ss[sec][n] * (1.0 / DIFF_HEAD_DIM) + EPS) * w2
             for n, cols in enumerate(halves)]
        store_heads(slot, jnp.concatenate(y, axis=1))

    for hd in range(N_HEADS):
        vt_ref[hd, 0:HEAD_W, :] = head_cols(acc[SEC_DV], hd).T.astype(BF16)
        vt_ref[hd, HEAD_W:V_ROWS, :] = jnp.ones((V_ROWS - HEAD_W, x.shape[0]), BF16)

    store_heads(SLOT_HQ, acc[SEC_HQ])
    store_heads(SLOT_HI, acc[SEC_HI])

    lbl = lbl_ref[...]
    e = jnp.exp(lbl - jnp.max(lbl, axis=0, keepdims=True))
    lb = e[0:1] / jnp.sum(e, axis=0, keepdims=True)
    f = lb + (1.0 - lb) * _sigmoid(acc[SEC_HK])
    log2_f = jnp.log(f) * LOG2E
    for hd in range(N_HEADS):
        g_ref[hd] = head_cols(log2_f, hd)
    store_heads(SLOT_HK, 1.0 - f)

    store_heads(SLOT_HG, acc[SEC_HG] * _sigmoid(acc[SEC_HG]))


def _in_proj(x2d, ln1_w, w_in, q_norm_w, k_norm_w, lb_logits):
    t = x2d.shape[0]
    const = lambda *_: (0, 0)
    return pl.pallas_call(
        _in_proj_kernel,
        grid=(t // TM_PROJ,),
        in_specs=[
            pl.BlockSpec((TM_PROJ, D_MODEL), lambda i: (i, 0)),
            pl.BlockSpec((1, D_MODEL), const),
            pl.BlockSpec((D_MODEL, IN_COLS), const, pipeline_mode=pl.Buffered(1)),
            pl.BlockSpec((1, HEAD_W), const),
            pl.BlockSpec((1, HEAD_W), const),
            pl.BlockSpec(lb_logits.shape, const),
        ],
        out_specs=[
            pl.BlockSpec((N_SLOT * N_HEADS, TM_PROJ, HEAD_W), lambda i: (0, i, 0)),
            pl.BlockSpec((N_HEADS, TM_PROJ, HEAD_W), lambda i: (0, i, 0)),
            pl.BlockSpec((N_HEADS, V_ROWS, TM_PROJ), lambda i: (0, 0, i)),
        ],
        out_shape=[
            jax.ShapeDtypeStruct((N_SLOT * N_HEADS, t, HEAD_W), BF16),
            jax.ShapeDtypeStruct((N_HEADS, t, HEAD_W), F32),
            jax.ShapeDtypeStruct((N_HEADS, V_ROWS, t), BF16),
        ],
        compiler_params=pltpu.CompilerParams(
            dimension_semantics=("parallel",), vmem_limit_bytes=VMEM_LIMIT),
        name="in_proj",
    )(x2d, ln1_w, w_in, q_norm_w, k_norm_w, lb_logits)


NEG_BIG = -1e30


def _diff_attn_kernel(q_ref, k_ref, vt_ref, lq1_ref, lk1_ref, lq2_ref, lk2_ref, sw_ref,
                      o_ref, qs_ref, s_ref, m_ref, acc_ref):
    qi = pl.program_id(1)
    n_hd = q_ref.shape[0]
    heads = range(n_hd)
    lane = lax.broadcasted_iota(jnp.int32, (HQ, HEAD_W), 1)
    for hd in heads:
        for half in range(2):
            q = q_ref[hd, 0, half * HQ:(half + 1) * HQ, :]
            zero = jnp.zeros_like(q)
            qs_ref[hd, (2 * half) * HQ:(2 * half + 1) * HQ] = jnp.where(lane < DIFF_HEAD_DIM, q, zero)
            qs_ref[hd, (2 * half + 1) * HQ:(2 * half + 2) * HQ] = jnp.where(lane >= DIFF_HEAD_DIM, q, zero)

    m_ref[...] = jnp.full(m_ref.shape, NEG_BIG, F32)
    acc_ref[...] = jnp.zeros(acc_ref.shape, F32)

    ck = lax.broadcasted_iota(jnp.int32, (TK, 4 * HQ), 0) // CHUNK
    col = lax.broadcasted_iota(jnp.int32, (TK, 4 * HQ), 1)
    rq = (col % HQ) // CHUNK
    first_mask = ck <= jnp.where(col >= 2 * HQ, TK // CHUNK, rq)
    diag_mask = (ck <= rq)[:, 0:2 * HQ]
    half1 = slice(2 * HQ, 4 * HQ)

    def kv_rows(j):
        return pl.ds(pl.multiple_of(j * TK, TK), TK)

    def score_step(j, slot, cols=slice(None)):
        rows = kv_rows(j)
        for hd in heads:
            s_ref[slot, hd, :, cols] = _nt_dot(k_ref[hd, 0, rows, :], qs_ref[hd, cols])

    def softmax_step(j, slot, cols=slice(None), mask=None):
        rows = kv_rows(j)
        p_all, alpha_all = [], []
        for hd in heads:
            s = s_ref[slot, hd, :, cols]
            if mask is not None:
                s = jnp.where(mask, s, NEG_BIG)
            m_old = m_ref[hd, :, cols]
            m_new = jnp.maximum(m_old, jnp.max(s, axis=0, keepdims=True))
            m_ref[hd, :, cols] = m_new
            alpha_all.append(jnp.exp2(m_old - m_new))
            p_all.append(jnp.exp2((s - m_new).astype(BF16)))
        pv_all = [_dot(vt_ref[hd, :, rows], p_all[hd]) for hd in heads]
        for hd in heads:
            acc_ref[hd, :, cols] = alpha_all[hd] * acc_ref[hd, :, cols] + pv_all[hd]

    score_step(0, 0)

    def visible_pair(jj, carry):
        j = 2 * jj
        score_step(j + 1, 1)
        softmax_step(j, 0)
        score_step(j + 2, 0)
        softmax_step(j + 1, 1)
        return carry

    lax.fori_loop(0, qi, visible_pair, 0)
    j = 2 * qi
    score_step(j + 1, 1, half1)
    softmax_step(j, 0, mask=first_mask)
    softmax_step(j + 1, 1, half1, mask=diag_mask)

    lam = (jnp.exp(jnp.sum(lq1_ref[...] * lk1_ref[...], axis=-1, keepdims=True))
           - jnp.exp(jnp.sum(lq2_ref[...] * lk2_ref[...], axis=-1, keepdims=True)) + LAM_INIT)
    gain = sw_ref[...] * (1.0 - LAM_INIT)
    for hd in heads:
        o = acc_ref[hd, 0:HEAD_W] / acc_ref[hd, HEAD_W:HEAD_W + 1]
        for half in range(2):
            c0 = 2 * half * HQ
            d = o[:, c0:c0 + HQ] - lam * o[:, c0 + HQ:c0 + 2 * HQ]
            ms = jnp.mean(d * d, axis=0, keepdims=True)
            o_ref[hd, 0, half * HQ:(half + 1) * HQ, :] = (
                d * lax.rsqrt(ms + EPS) * gain).T.astype(BF16)


def _diff_attn(proj4, vt, lam_q1, lam_k1, lam_q2, lam_k2, subln_w):
    _, b, s, _ = proj4.shape
    const = lambda *_: (0, 0)
    lam_spec = pl.BlockSpec((1, DIFF_HEAD_DIM), const)
    return pl.pallas_call(
        _diff_attn_kernel,
        grid=(b, s // TQ),
        in_specs=[
            pl.BlockSpec((N_HEADS, 1, TQ, HEAD_W), lambda bi, qi: (SLOT_DQ, bi, qi, 0)),
            pl.BlockSpec((N_HEADS, 1, s, HEAD_W), lambda bi, qi: (SLOT_DK, bi, 0, 0)),
            pl.BlockSpec((N_HEADS, V_ROWS, s), lambda bi, qi: (0, 0, bi)),
            lam_spec, lam_spec, lam_spec, lam_spec,
            pl.BlockSpec((HEAD_W, 1), const),
        ],
        out_specs=pl.BlockSpec((N_HEADS, 1, TQ, HEAD_W), lambda bi, qi: (0, bi, qi, 0)),
        out_shape=jax.ShapeDtypeStruct((N_HEADS, b, s, HEAD_W), BF16),
        scratch_shapes=[
            pltpu.VMEM((N_HEADS, 2 * TQ, HEAD_W), BF16),
            pltpu.VMEM((2, N_HEADS, TK, 2 * TQ), F32),
            pltpu.VMEM((N_HEADS, 1, 2 * TQ), F32),
            pltpu.VMEM((N_HEADS, V_ROWS, 2 * TQ), F32),
        ],
        compiler_params=pltpu.CompilerParams(
            dimension_semantics=("parallel", "parallel"), vmem_limit_bytes=VMEM_LIMIT),
        name="diff_attn",
    )(proj4, proj4, vt, lam_q1, lam_k1, lam_q2, lam_k2, subln_w)


def _hgrn_constants():
    n = HGRN_CHUNK
    t = np.arange(n)[:, None]
    s = np.arange(n)[None, :]
    ltri = (s <= t)
    masks = [(s == t)]
    fine, roles = [], []
    for lvl in range(N_LEVELS):
        hs = n >> (lvl + 1)
        blk = t // (2 * hs)
        mid = blk * 2 * hs + hs - 1
        is_q = (t % (2 * hs)) >= hs
        if lvl >= N_COARSE:
            fine.append(np.where(is_q, (s > mid) & (s <= t), (s > t) & (s <= mid)))
        masks.append((blk == (s // (2 * hs))) & is_q & ((s % (2 * hs)) < hs))
        roles.append(np.broadcast_to(np.where(is_q, 1.0, -1.0), (n, HEAD_W)))
    ltri = ltri.astype(np.float32)
    wfine = np.concatenate(fine, axis=0).astype(np.float32)
    masks = np.stack(masks).astype(np.float32)
    roles = np.stack(roles).astype(np.float32)
    return ltri, wfine, masks, roles


def _hgrn_kernel(q_ref, k_ref, v_ref, gate_ref, g_ref, ltri_ref, wfine_ref, mask_ref, role_ref,
                 nw_ref, o_ref, st_ref):
    n_hd = q_ref.shape[0]
    n_chunks = q_ref.shape[2] // HGRN_CHUNK
    st_ref[...] = jnp.zeros(st_ref.shape, F32)
    ltri = ltri_ref[...]
    wfine = wfine_ref[...]
    nw = nw_ref[...]

    def exponents(g):
        g_hi, g_lo = _split_bf16(g)
        return _dot(ltri, g_hi) + _dot(ltri, g_lo), _dot(wfine, g_hi)

    def decays(b, fine):
        def row_bcast(r, n):
            return jnp.broadcast_to(b[r:r + 1, :], (n, HEAD_W))

        e_b = jnp.exp2(b)
        e_u = jnp.exp2(row_bcast(HGRN_CHUNK - 1, HGRN_CHUNK) - b)
        z = []
        for lvl in range(N_COARSE):
            hs = HGRN_CHUNK >> (lvl + 1)
            b_mid = jnp.concatenate([row_bcast(blk * 2 * hs + hs - 1, 2 * hs)
                                     for blk in range(HGRN_CHUNK // (2 * hs))], axis=0)
            z.append(jnp.exp2((b - b_mid) * role_ref[lvl]))
        fine = jnp.exp2(fine)
        z += [fine[i * HGRN_CHUNK:(i + 1) * HGRN_CHUNK] for i in range(N_LEVELS - N_COARSE)]
        return e_b, e_u, z

    def group(gi, carry):
        base = gi * (HGRN_GROUP * HGRN_CHUNK)
        items = [(hd, pl.ds(pl.multiple_of(base + u * HGRN_CHUNK, HGRN_CHUNK), HGRN_CHUNK))
                 for u in range(HGRN_GROUP) for hd in range(n_hd)]
        q_b = [q_ref[hd, 0, rows, :] for hd, rows in items]
        k_b = [k_ref[hd, 0, rows, :] for hd, rows in items]
        v_b = [v_ref[hd, 0, rows, :] for hd, rows in items]
        expo = [exponents(g_ref[hd, 0, rows, :]) for hd, rows in items]
        dec = [decays(*e) for e in expo]

        def by_role(lvl, q, k):
            if lvl >= N_COARSE:
                return jnp.where(role_ref[lvl] > 0.0, q, k)
            hs = HGRN_CHUNK >> (lvl + 1)
            return jnp.concatenate([(q if part % 2 else k)[part * hs:(part + 1) * hs]
                                    for part in range(HGRN_CHUNK // hs)], axis=0)

        xz, q_dec, k_dec = [], [], []
        for n, (e_b, e_u, z) in enumerate(dec):
            q = q_b[n].astype(F32)
            k = k_b[n].astype(F32)
            xz.append([(by_role(lvl, q, k) * z[lvl]).astype(BF16) for lvl in range(N_LEVELS)])
            q_dec.append((q * e_b).astype(BF16))
            k_dec.append((k * e_u).astype(BF16))

        pair = [[_nt_dot(q_b[n], k_b[n])] + [_nt_dot(x, x) for x in xz[n]]
                for n in range(len(items))]
        kv = [_tn_dot(v_b[n], k_dec[n]) for n in range(len(items))]
        owns = [mask_ref[lvl] > 0.5 for lvl in range(N_LEVELS + 1)]
        att = []
        for p in pair:
            a = jnp.where(owns[0], p[0], 0.0)
            for lvl in range(1, N_LEVELS + 1):
                a = jnp.where(owns[lvl], p[lvl], a)
            att.append(a.astype(BF16))
        o = [_dot(att[n], v_b[n]) for n in range(len(items))]

        st = [st_ref[hd] for hd in range(n_hd)]
        for n, (hd, rows) in enumerate(items):
            o[n] = o[n] + _nt_dot(q_dec[n], st[hd].astype(BF16))
            st[hd] = st[hd] * dec[n][0][HGRN_CHUNK - 1:HGRN_CHUNK, :] + kv[n]
        for hd in range(n_hd):
            st_ref[hd] = st[hd]

        for n, (hd, rows) in enumerate(items):
            ms = jnp.mean(o[n] * o[n], axis=-1, keepdims=True)
            y = o[n] * lax.rsqrt(ms + EPS) * nw * gate_ref[hd, 0, rows, :].astype(F32)
            o_ref[hd, 0, rows, :] = y.astype(BF16)
        return carry

    lax.fori_loop(0, n_chunks // HGRN_GROUP, group, 0)


def _hgrn2(proj4, g4, norm_w):
    _, b, s, _ = proj4.shape
    ltri, wfine, masks, roles = _hgrn_constants()
    heads_spec = lambda sec: pl.BlockSpec((N_HEADS, 1, s, HEAD_W), lambda bi: (sec, bi, 0, 0))
    return pl.pallas_call(
        _hgrn_kernel,
        grid=(b,),
        in_specs=[
            heads_spec(SLOT_HQ), heads_spec(SLOT_HK), heads_spec(SLOT_HI), heads_spec(SLOT_HG),
            heads_spec(0),
            pl.BlockSpec(ltri.shape, lambda *_: (0, 0)),
            pl.BlockSpec(wfine.shape, lambda *_: (0, 0)),
            pl.BlockSpec(masks.shape, lambda *_: (0, 0, 0)),
            pl.BlockSpec(roles.shape, lambda *_: (0, 0, 0)),
            pl.BlockSpec((1, HEAD_W), lambda *_: (0, 0)),
        ],
        out_specs=heads_spec(0),
        out_shape=jax.ShapeDtypeStruct((N_HEADS, b, s, HEAD_W), BF16),
        scratch_shapes=[pltpu.VMEM((N_HEADS, HEAD_W, HEAD_W), F32)],
        compiler_params=pltpu.CompilerParams(
            dimension_semantics=("parallel",), vmem_limit_bytes=VMEM_LIMIT),
        name="hgrn2",
    )(proj4, proj4, proj4, proj4, g4, jnp.asarray(ltri, BF16), jnp.asarray(wfine, BF16),
      jnp.asarray(masks), jnp.asarray(roles), norm_w)


def _ff_chunks():
    chunks, c0 = [], 0
    while c0 < D_FF:
        fc = min(FF_CHUNK, D_FF - c0)
        chunks.append((c0, fc))
        c0 += fc
    return chunks


def _out_ffn_kernel(tiles_per_seq, x_ref, od_ref, oh_ref, wout_ref, ln2_ref, wup_ref, cw_ref,
                    cb_ref, wdn_ref, out_ref, tail_ref, act_ref):
    i = pl.program_id(0)
    mix = jnp.concatenate([od_ref[hd] for hd in range(N_HEADS)]
                          + [oh_ref[hd] for hd in range(N_HEADS)], axis=1)
    x1 = x_ref[...] + _dot(mix, wout_ref[...])
    ms = jnp.mean(x1 * x1, axis=-1, keepdims=True)
    h2 = (x1 * lax.rsqrt(ms + EPS) * ln2_ref[...]).astype(BF16)

    seq_start = (i % tiles_per_seq) == 0
    tm = x1.shape[0]
    for c0, fc in _ff_chunks():
        cols = slice(c0, c0 + fc)
        u = _dot(h2, wup_ref[:, cols])
        v = _dot(h2, wup_ref[:, D_FF + c0:D_FF + c0 + fc])
        tail = jnp.where(seq_start, 0.0, tail_ref[:, cols])
        tail_ref[:, cols] = u[tm - SUBLANES:, :]
        ext = jnp.concatenate([tail, u], axis=0)
        u1 = pltpu.roll(ext, 1, 0)[SUBLANES:]
        u2 = pltpu.roll(ext, 2, 0)[SUBLANES:]
        cw = cw_ref[:, cols]
        c = cb_ref[:, cols] + u2 * cw[0:1] + u1 * cw[1:2] + u * cw[2:3]
        act_ref[:, cols] = (c * _sigmoid(c) * v).astype(BF16)
    out_ref[...] = x1 + _dot(act_ref[...], wdn_ref[...])


def _out_ffn(x2d, o_diff, o_hgrn, w_out, ln2_w, w_up, conv_w, conv_b, w_down, seq_len):
    t = x2d.shape[0]
    const = lambda *_: (0, 0)
    resident = functools.partial(pl.BlockSpec, index_map=const, pipeline_mode=pl.Buffered(1))
    return pl.pallas_call(
        functools.partial(_out_ffn_kernel, seq_len // TM_FFN),
        grid=(t // TM_FFN,),
        in_specs=[
            pl.BlockSpec((TM_FFN, D_MODEL), lambda i: (i, 0)),
            pl.BlockSpec((N_HEADS, TM_FFN, HEAD_W), lambda i: (0, i, 0)),
            pl.BlockSpec((N_HEADS, TM_FFN, HEAD_W), lambda i: (0, i, 0)),
            resident((D_MODEL, D_MODEL)),
            pl.BlockSpec((1, D_MODEL), const),
            resident((D_MODEL, 2 * D_FF)),
            pl.BlockSpec((CONV_WIDTH, D_FF), const),
            pl.BlockSpec((1, D_FF), const),
            resident((D_FF, D_MODEL)),
        ],
        out_specs=pl.BlockSpec((TM_FFN, D_MODEL), lambda i: (i, 0)),
        out_shape=jax.ShapeDtypeStruct((t, D_MODEL), F32),
        scratch_shapes=[
            pltpu.VMEM((SUBLANES, D_FF), F32),
            pltpu.VMEM((TM_FFN, D_FF), BF16),
        ],
        compiler_params=pltpu.CompilerParams(
            dimension_semantics=("arbitrary",), vmem_limit_bytes=VMEM_LIMIT),
        name="out_ffn",
    )(x2d, o_diff, o_hgrn, w_out, ln2_w, w_up, conv_w, conv_b, w_down)


def kernel(x, ln1_w, w_in, q_norm_w, k_norm_w, lam_q1, lam_k1, lam_q2, lam_k2, diff_subln_w,
           hgrn_lb_logits, hgrn_norm_w, w_out, ln2_w, w_up, conv_w, conv_b, w_down):
    b, s, d = x.shape
    depth = ln1_w.shape[0]
    assert depth == 1 and d == D_MODEL and s % TM_FFN == 0 and s % TQ == 0 and TK == HQ
    t = b * s
    x2d = x.reshape(t, d)
    l = 0
    tile2 = lambda w: jnp.concatenate([w, w], axis=-1)[None, :]

    proj, g, vt = _in_proj(x2d, ln1_w[l][None, :], w_in[l].astype(BF16), tile2(q_norm_w[l]),
                           tile2(k_norm_w[l]), hgrn_lb_logits)
    proj4 = proj.reshape(N_SLOT * N_HEADS, b, s, HEAD_W)
    g4 = g.reshape(N_HEADS, b, s, HEAD_W)

    o_diff = _diff_attn(proj4, vt, lam_q1[l][None, :], lam_k1[l][None, :], lam_q2[l][None, :],
                        lam_k2[l][None, :], diff_subln_w[l][:, None])
    o_hgrn = _hgrn2(proj4, g4, hgrn_norm_w[l][None, :])

    out = _out_ffn(x2d, o_diff.reshape(N_HEADS, t, HEAD_W), o_hgrn.reshape(N_HEADS, t, HEAD_W),
                   w_out[l].astype(BF16), ln2_w[l][None, :], w_up[l].astype(BF16), conv_w[l],
                   conv_b[l][None, :], w_down[l].astype(BF16), s)
    return out.reshape(b, s, d)
```

```python
import functools
import math

import numpy as np
import jax
import jax.numpy as jnp
from jax import lax
from jax.experimental import pallas as pl
from jax.experimental.pallas import tpu as pltpu

F32 = jnp.float32
BF16 = jnp.bfloat16

D_MODEL = 1024
CHUNK = 64
HEAD_W = 128
DIFF_HEAD_DIM = 64
N_HEADS = 4
SEC_W = N_HEADS * HEAD_W
N_SEC = 7
IN_COLS = N_SEC * SEC_W
D_FF = 2816
CONV_WIDTH = 3
EPS = 1e-6
LAM_INIT = 0.8 - 0.6 * math.exp(-0.3 * 0)
LOG2E = math.log2(math.e)

SEC_DQ, SEC_DK, SEC_DV, SEC_HQ, SEC_HK, SEC_HI, SEC_HG = range(N_SEC)
SLOT_DQ, SLOT_DK, SLOT_HQ, SLOT_HK, SLOT_HI, SLOT_HG = range(6)
N_SLOT = 6
BF16_SUBLANES = 16
V_ROWS = HEAD_W + BF16_SUBLANES

LANES = 128
SUBLANES = 8
MXU_W = 256
TM_PROJ = 1024
TQ = 1024
TK = 256
N_SUB = TQ // TK
HQ = TK
TM_FFN = 512
FF_CHUNK = 512
VMEM_LIMIT = 56 * 1024 * 1024

HGRN_CHUNK = 64
N_LEVELS = 6
N_COARSE = 3
HGRN_GROUP = 4


def _nt_dot(a, b):
    return lax.dot_general(a, b, (((1,), (1,)), ((), ())), preferred_element_type=F32)


def _tn_dot(a, b):
    return lax.dot_general(a, b, (((0,), (0,)), ((), ())), preferred_element_type=F32)


def _dot(a, b):
    return jnp.dot(a, b, preferred_element_type=F32)


def _split_bf16(x):
    hi = x.astype(BF16)
    lo = (x - hi.astype(F32)).astype(BF16)
    return hi, lo


def _sigmoid(x):
    return 1.0 / (1.0 + jnp.exp(-x))


def _in_proj_kernel(x_ref, ln1_ref, w_ref, qw_ref, kw_ref, lbl_ref, proj_ref, g_ref, vt_ref):
    x = x_ref[...]
    ms = jnp.mean(x * x, axis=-1, keepdims=True)
    h = (x * lax.rsqrt(ms + EPS) * ln1_ref[...]).astype(BF16)

    r = lax.broadcasted_iota(jnp.int32, (MXU_W, MXU_W), 0) // DIFF_HEAD_DIM
    c = lax.broadcasted_iota(jnp.int32, (MXU_W, MXU_W), 1) // DIFF_HEAD_DIM
    grp = jnp.where(r == c, 1.0, 0.0).astype(BF16)

    def head_cols(a, hd):
        return a[:, hd * HEAD_W:(hd + 1) * HEAD_W]

    def store_heads(slot, a):
        for hd in range(N_HEADS):
            proj_ref[slot * N_HEADS + hd] = head_cols(a, hd).astype(BF16)

    def project(j):
        return _dot(h, w_ref[:, j * SEC_W:(j + 1) * SEC_W])

    acc = [None] * N_SEC
    halves = [slice(c0, c0 + MXU_W) for c0 in range(0, SEC_W, MXU_W)]
    ss = {}
    for sec in (SEC_DQ, SEC_DK):
        acc[sec] = project(sec)
        ss[sec] = [_dot((acc[sec][:, cols] * acc[sec][:, cols]).astype(BF16), grp)
                   for cols in halves]
    for sec in (SEC_HK, SEC_HG, SEC_DV, SEC_HQ, SEC_HI):
        acc[sec] = project(sec)

    q_gain = qw_ref[...] * (DIFF_HEAD_DIM ** -0.5 * LOG2E)
    for sec, slot, w_norm in ((SEC_DQ, SLOT_DQ, q_gain), (SEC_DK, SLOT_DK, kw_ref[...])):
        w2 = jnp.concatenate([w_norm] * (MXU_W // HEAD_W), axis=1)
        y = [acc[sec][:, cols] * lax.rsqrt(ss[sec][n] * (1.0 / DIFF_HEAD_DIM) + EPS) * w2
             for n, cols in enumerate(halves)]
        store_heads(slot, jnp.concatenate(y, axis=1))

    for hd in range(N_HEADS):
        vt_ref[hd, 0:HEAD_W, :] = head_cols(acc[SEC_DV], hd).T.astype(BF16)
        vt_ref[hd, HEAD_W:V_ROWS, :] = jnp.ones((V_ROWS - HEAD_W, x.shape[0]), BF16)

    store_heads(SLOT_HQ, acc[SEC_HQ])
    store_heads(SLOT_HI, acc[SEC_HI])

    lbl = lbl_ref[...]
    e = jnp.exp(lbl - jnp.max(lbl, axis=0, keepdims=True))
    lb = e[0:1] / jnp.sum(e, axis=0, keepdims=True)
    f = lb + (1.0 - lb) * _sigmoid(acc[SEC_HK])
    log2_f = jnp.log(f) * LOG2E
    for hd in range(N_HEADS):
        g_ref[hd] = head_cols(log2_f, hd)
    store_heads(SLOT_HK, 1.0 - f)

    store_heads(SLOT_HG, acc[SEC_HG] * _sigmoid(acc[SEC_HG]))


def _in_proj(x2d, ln1_w, w_in, q_norm_w, k_norm_w, lb_logits):
    t = x2d.shape[0]
    const = lambda *_: (0, 0)
    return pl.pallas_call(
        _in_proj_kernel,
        grid=(t // TM_PROJ,),
        in_specs=[
            pl.BlockSpec((TM_PROJ, D_MODEL), lambda i: (i, 0)),
            pl.BlockSpec((1, D_MODEL), const),
            pl.BlockSpec((D_MODEL, IN_COLS), const, pipeline_mode=pl.Buffered(1)),
            pl.BlockSpec((1, HEAD_W), const),
            pl.BlockSpec((1, HEAD_W), const),
            pl.BlockSpec(lb_logits.shape, const),
        ],
        out_specs=[
            pl.BlockSpec((N_SLOT * N_HEADS, TM_PROJ, HEAD_W), lambda i: (0, i, 0)),
            pl.BlockSpec((N_HEADS, TM_PROJ, HEAD_W), lambda i: (0, i, 0)),
            pl.BlockSpec((N_HEADS, V_ROWS, TM_PROJ), lambda i: (0, 0, i)),
        ],
        out_shape=[
            jax.ShapeDtypeStruct((N_SLOT * N_HEADS, t, HEAD_W), BF16),
            jax.ShapeDtypeStruct((N_HEADS, t, HEAD_W), F32),
            jax.ShapeDtypeStruct((N_HEADS, V_ROWS, t), BF16),
        ],
        compiler_params=pltpu.CompilerParams(
            dimension_semantics=("parallel",), vmem_limit_bytes=VMEM_LIMIT),
        name="in_proj",
    )(x2d, ln1_w, w_in, q_norm_w, k_norm_w, lb_logits)


NEG_BIG = -1e30


def _diff_attn_kernel(q_ref, k_ref, vt_ref, lq1_ref, lk1_ref, lq2_ref, lk2_ref, sw_ref,
                      o_ref, qs_ref, s_ref, m_ref, acc_ref):
    qi = pl.program_id(1)
    n_hd = q_ref.shape[0]
    heads = range(n_hd)
    lane = lax.broadcasted_iota(jnp.int32, (HQ, HEAD_W), 1)
    for hd in heads:
        for grp in range(N_SUB):
            q = q_ref[hd, 0, grp * HQ:(grp + 1) * HQ, :]
            zero = jnp.zeros_like(q)
            qs_ref[hd, (2 * grp) * HQ:(2 * grp + 1) * HQ] = jnp.where(lane < DIFF_HEAD_DIM, q, zero)
            qs_ref[hd, (2 * grp + 1) * HQ:(2 * grp + 2) * HQ] = jnp.where(lane >= DIFF_HEAD_DIM, q, zero)

    m_ref[...] = jnp.full(m_ref.shape, NEG_BIG, F32)
    acc_ref[...] = jnp.zeros(acc_ref.shape, F32)

    ck = lax.broadcasted_iota(jnp.int32, (TK, 2 * TQ), 0) // CHUNK
    col = lax.broadcasted_iota(jnp.int32, (TK, 2 * TQ), 1)
    diag_mask = ck <= jnp.where(col >= 2 * HQ, TK // CHUNK, (col % HQ) // CHUNK)

    def kv_rows(j):
        return pl.ds(pl.multiple_of(j * TK, TK), TK)

    def score_step(j, slot, cols=slice(None)):
        rows = kv_rows(j)
        for hd in heads:
            s_ref[slot, hd, :, cols] = _nt_dot(k_ref[hd, 0, rows, :], qs_ref[hd, cols])

    def softmax_step(j, slot, cols=slice(None), mask=None):
        rows = kv_rows(j)
        p_all, alpha_all = [], []
        for hd in heads:
            s = s_ref[slot, hd, :, cols]
            if mask is not None:
                s = jnp.where(mask, s, NEG_BIG)
            m_old = m_ref[hd, :, cols]
            m_new = jnp.maximum(m_old, jnp.max(s, axis=0, keepdims=True))
            m_ref[hd, :, cols] = m_new
            alpha_all.append(jnp.exp2(m_old - m_new))
            p_all.append(jnp.exp2((s - m_new).astype(BF16)))
        pv_all = [_dot(vt_ref[hd, :, rows], p_all[hd]) for hd in heads]
        for hd in heads:
            acc_ref[hd, :, cols] = alpha_all[hd] * acc_ref[hd, :, cols] + pv_all[hd]

    score_step(0, 0)

    def visible_pair(jj, carry):
        j = 2 * jj
        score_step(j + 1, 1)
        softmax_step(j, 0)
        score_step(j + 2, 0)
        softmax_step(j + 1, 1)
        return carry

    lax.fori_loop(0, qi * (N_SUB // 2), visible_pair, 0)
    j0 = N_SUB * qi
    from_group = [slice(2 * HQ * r, 2 * TQ) for r in range(N_SUB)]
    for r in range(N_SUB):
        if r + 1 < N_SUB:
            score_step(j0 + r + 1, (r + 1) % 2, from_group[r + 1])
        softmax_step(j0 + r, r % 2, from_group[r], mask=diag_mask[:, 0:2 * TQ - 2 * HQ * r])

    lam = (jnp.exp(jnp.sum(lq1_ref[...] * lk1_ref[...], axis=-1, keepdims=True))
           - jnp.exp(jnp.sum(lq2_ref[...] * lk2_ref[...], axis=-1, keepdims=True)) + LAM_INIT)
    gain = sw_ref[...] * (1.0 - LAM_INIT)
    for hd in heads:
        o = acc_ref[hd, 0:HEAD_W] / acc_ref[hd, HEAD_W:HEAD_W + 1]
        for grp in range(N_SUB):
            c0 = 2 * grp * HQ
            d = o[:, c0:c0 + HQ] - lam * o[:, c0 + HQ:c0 + 2 * HQ]
            ms = jnp.mean(d * d, axis=0, keepdims=True)
            o_ref[hd, 0, grp * HQ:(grp + 1) * HQ, :] = (
                d * lax.rsqrt(ms + EPS) * gain).T.astype(BF16)


def _diff_attn(proj4, vt, lam_q1, lam_k1, lam_q2, lam_k2, subln_w):
    _, b, s, _ = proj4.shape
    const = lambda *_: (0, 0)
    lam_spec = pl.BlockSpec((1, DIFF_HEAD_DIM), const)
    return pl.pallas_call(
        _diff_attn_kernel,
        grid=(b, s // TQ),
        in_specs=[
            pl.BlockSpec((N_HEADS, 1, TQ, HEAD_W), lambda bi, qi: (SLOT_DQ, bi, qi, 0)),
            pl.BlockSpec((N_HEADS, 1, s, HEAD_W), lambda bi, qi: (SLOT_DK, bi, 0, 0)),
            pl.BlockSpec((N_HEADS, V_ROWS, s), lambda bi, qi: (0, 0, bi)),
            lam_spec, lam_spec, lam_spec, lam_spec,
            pl.BlockSpec((HEAD_W, 1), const),
        ],
        out_specs=pl.BlockSpec((N_HEADS, 1, TQ, HEAD_W), lambda bi, qi: (0, bi, qi, 0)),
        out_shape=jax.ShapeDtypeStruct((N_HEADS, b, s, HEAD_W), BF16),
        scratch_shapes=[
            pltpu.VMEM((N_HEADS, 2 * TQ, HEAD_W), BF16),
            pltpu.VMEM((2, N_HEADS, TK, 2 * TQ), F32),
            pltpu.VMEM((N_HEADS, 1, 2 * TQ), F32),
            pltpu.VMEM((N_HEADS, V_ROWS, 2 * TQ), F32),
        ],
        compiler_params=pltpu.CompilerParams(
            dimension_semantics=("parallel", "parallel"), vmem_limit_bytes=VMEM_LIMIT),
        name="diff_attn",
    )(proj4, proj4, vt, lam_q1, lam_k1, lam_q2, lam_k2, subln_w)


def _hgrn_constants():
    n = HGRN_CHUNK
    t = np.arange(n)[:, None]
    s = np.arange(n)[None, :]
    ltri = (s <= t)
    masks = [(s == t)]
    fine, roles = [], []
    for lvl in range(N_LEVELS):
        hs = n >> (lvl + 1)
        blk = t // (2 * hs)
        mid = blk * 2 * hs + hs - 1
        is_q = (t % (2 * hs)) >= hs
        if lvl >= N_COARSE:
            fine.append(np.where(is_q, (s > mid) & (s <= t), (s > t) & (s <= mid)))
        masks.append((blk == (s // (2 * hs))) & is_q & ((s % (2 * hs)) < hs))
        roles.append(np.broadcast_to(np.where(is_q, 1.0, -1.0), (n, HEAD_W)))
    ltri = ltri.astype(np.float32)
    wfine = np.concatenate(fine, axis=0).astype(np.float32)
    masks = np.stack(masks).astype(np.float32)
    roles = np.stack(roles).astype(np.float32)
    return ltri, wfine, masks, roles


def _hgrn_kernel(q_ref, k_ref, v_ref, gate_ref, g_ref, ltri_ref, wfine_ref, mask_ref, role_ref,
                 nw_ref, o_ref, st_ref):
    n_hd = q_ref.shape[0]
    n_chunks = q_ref.shape[2] // HGRN_CHUNK
    st_ref[...] = jnp.zeros(st_ref.shape, F32)
    ltri = ltri_ref[...]
    wfine = wfine_ref[...]
    nw = nw_ref[...]

    def exponents(g):
        g_hi, g_lo = _split_bf16(g)
        return _dot(ltri, g_hi) + _dot(ltri, g_lo), _dot(wfine, g_hi)

    def decays(b, fine):
        def row_bcast(r, n):
            return jnp.broadcast_to(b[r:r + 1, :], (n, HEAD_W))

        e_b = jnp.exp2(b)
        e_u = jnp.exp2(row_bcast(HGRN_CHUNK - 1, HGRN_CHUNK) - b)
        z = []
        for lvl in range(N_COARSE):
            hs = HGRN_CHUNK >> (lvl + 1)
            b_mid = jnp.concatenate([row_bcast(blk * 2 * hs + hs - 1, 2 * hs)
                                     for blk in range(HGRN_CHUNK // (2 * hs))], axis=0)
            z.append(jnp.exp2((b - b_mid) * role_ref[lvl]))
        fine = jnp.exp2(fine)
        z += [fine[i * HGRN_CHUNK:(i + 1) * HGRN_CHUNK] for i in range(N_LEVELS - N_COARSE)]
        return e_b, e_u, z

    def group(gi, carry):
        base = gi * (HGRN_GROUP * HGRN_CHUNK)
        items = [(hd, pl.ds(pl.multiple_of(base + u * HGRN_CHUNK, HGRN_CHUNK), HGRN_CHUNK))
                 for u in range(HGRN_GROUP) for hd in range(n_hd)]
        q_b = [q_ref[hd, 0, rows, :] for hd, rows in items]
        k_b = [k_ref[hd, 0, rows, :] for hd, rows in items]
        v_b = [v_ref[hd, 0, rows, :] for hd, rows in items]
        expo = [exponents(g_ref[hd, 0, rows, :]) for hd, rows in items]
        dec = [decays(*e) for e in expo]

        def by_role(lvl, q, k):
            if lvl >= N_COARSE:
                return jnp.where(role_ref[lvl] > 0.0, q, k)
            hs = HGRN_CHUNK >> (lvl + 1)
            return jnp.concatenate([(q if part % 2 else k)[part * hs:(part + 1) * hs]
                                    for part in range(HGRN_CHUNK // hs)], axis=0)

        xz, q_dec, k_dec = [], [], []
        for n, (e_b, e_u, z) in enumerate(dec):
            q = q_b[n].astype(F32)
            k = k_b[n].astype(F32)
            xz.append([(by_role(lvl, q, k) * z[lvl]).astype(BF16) for lvl in range(N_LEVELS)])
            q_dec.append((q * e_b).astype(BF16))
            k_dec.append((k * e_u).astype(BF16))

        pair = [[_nt_dot(q_b[n], k_b[n])] + [_nt_dot(x, x) for x in xz[n]]
                for n in range(len(items))]
        kv = [_tn_dot(v_b[n], k_dec[n]) for n in range(len(items))]
        owns = [mask_ref[lvl] > 0.5 for lvl in range(N_LEVELS + 1)]
        att = []
        for p in pair:
            a = jnp.where(owns[0], p[0], 0.0)
            for lvl in range(1, N_LEVELS + 1):
                a = jnp.where(owns[lvl], p[lvl], a)
            att.append(a.astype(BF16))
        o = [_dot(att[n], v_b[n]) for n in range(len(items))]

        st = [st_ref[hd] for hd in range(n_hd)]
        for n, (hd, rows) in enumerate(items):
            o[n] = o[n] + _nt_dot(q_dec[n], st[hd].astype(BF16))
            st[hd] = st[hd] * dec[n][0][HGRN_CHUNK - 1:HGRN_CHUNK, :] + kv[n]
        for hd in range(n_hd):
            st_ref[hd] = st[hd]

        for n, (hd, rows) in enumerate(items):
            ms = jnp.mean(o[n] * o[n], axis=-1, keepdims=True)
            y = o[n] * lax.rsqrt(ms + EPS) * nw * gate_ref[hd, 0, rows, :].astype(F32)
            o_ref[hd, 0, rows, :] = y.astype(BF16)
        return carry

    lax.fori_loop(0, n_chunks // HGRN_GROUP, group, 0)


def _hgrn2(proj4, g4, norm_w):
    _, b, s, _ = proj4.shape
    ltri, wfine, masks, roles = _hgrn_constants()
    heads_spec = lambda sec: pl.BlockSpec((N_HEADS, 1, s, HEAD_W), lambda bi: (sec, bi, 0, 0))
    return pl.pallas_call(
        _hgrn_kernel,
        grid=(b,),
        in_specs=[
            heads_spec(SLOT_HQ), heads_spec(SLOT_HK), heads_spec(SLOT_HI), heads_spec(SLOT_HG),
            heads_spec(0),
            pl.BlockSpec(ltri.shape, lambda *_: (0, 0)),
            pl.BlockSpec(wfine.shape, lambda *_: (0, 0)),
            pl.BlockSpec(masks.shape, lambda *_: (0, 0, 0)),
            pl.BlockSpec(roles.shape, lambda *_: (0, 0, 0)),
            pl.BlockSpec((1, HEAD_W), lambda *_: (0, 0)),
        ],
        out_specs=heads_spec(0),
        out_shape=jax.ShapeDtypeStruct((N_HEADS, b, s, HEAD_W), BF16),
        scratch_shapes=[pltpu.VMEM((N_HEADS, HEAD_W, HEAD_W), F32)],
        compiler_params=pltpu.CompilerParams(
            dimension_semantics=("parallel",), vmem_limit_bytes=VMEM_LIMIT),
        name="hgrn2",
    )(proj4, proj4, proj4, proj4, g4, jnp.asarray(ltri, BF16), jnp.asarray(wfine, BF16),
      jnp.asarray(masks), jnp.asarray(roles), norm_w)


def _ff_chunks():
    chunks, c0 = [], 0
    while c0 < D_FF:
        fc = min(FF_CHUNK, D_FF - c0)
        chunks.append((c0, fc))
        c0 += fc
    return chunks


def _out_ffn_kernel(tiles_per_seq, x_ref, od_ref, oh_ref, wout_ref, ln2_ref, wup_ref, cw_ref,
                    cb_ref, wdn_ref, out_ref, tail_ref, act_ref):
    i = pl.program_id(0)
    mix = jnp.concatenate([od_ref[hd] for hd in range(N_HEADS)]
                          + [oh_ref[hd] for hd in range(N_HEADS)], axis=1)
    x1 = x_ref[...] + _dot(mix, wout_ref[...])
    ms = jnp.mean(x1 * x1, axis=-1, keepdims=True)
    h2 = (x1 * lax.rsqrt(ms + EPS) * ln2_ref[...]).astype(BF16)

    seq_start = (i % tiles_per_seq) == 0
    tm = x1.shape[0]
    for c0, fc in _ff_chunks():
        cols = slice(c0, c0 + fc)
        u = _dot(h2, wup_ref[:, cols])
        v = _dot(h2, wup_ref[:, D_FF + c0:D_FF + c0 + fc])
        tail = jnp.where(seq_start, 0.0, tail_ref[:, cols])
        tail_ref[:, cols] = u[tm - SUBLANES:, :]
        ext = jnp.concatenate([tail, u], axis=0)
        u1 = pltpu.roll(ext, 1, 0)[SUBLANES:]
        u2 = pltpu.roll(ext, 2, 0)[SUBLANES:]
        cw = cw_ref[:, cols]
        c = cb_ref[:, cols] + u2 * cw[0:1] + u1 * cw[1:2] + u * cw[2:3]
        act_ref[:, cols] = (c * _sigmoid(c) * v).astype(BF16)
    out_ref[...] = x1 + _dot(act_ref[...], wdn_ref[...])


def _out_ffn(x2d, o_diff, o_hgrn, w_out, ln2_w, w_up, conv_w, conv_b, w_down, seq_len):
    t = x2d.shape[0]
    const = lambda *_: (0, 0)
    resident = functools.partial(pl.BlockSpec, index_map=const, pipeline_mode=pl.Buffered(1))
    return pl.pallas_call(
        functools.partial(_out_ffn_kernel, seq_len // TM_FFN),
        grid=(t // TM_FFN,),
        in_specs=[
            pl.BlockSpec((TM_FFN, D_MODEL), lambda i: (i, 0)),
            pl.BlockSpec((N_HEADS, TM_FFN, HEAD_W), lambda i: (0, i, 0)),
            pl.BlockSpec((N_HEADS, TM_FFN, HEAD_W), lambda i: (0, i, 0)),
            resident((D_MODEL, D_MODEL)),
            pl.BlockSpec((1, D_MODEL), const),
            resident((D_MODEL, 2 * D_FF)),
            pl.BlockSpec((CONV_WIDTH, D_FF), const),
            pl.BlockSpec((1, D_FF), const),
            resident((D_FF, D_MODEL)),
        ],
        out_specs=pl.BlockSpec((TM_FFN, D_MODEL), lambda i: (i, 0)),
        out_shape=jax.ShapeDtypeStruct((t, D_MODEL), F32),
        scratch_shapes=[
            pltpu.VMEM((SUBLANES, D_FF), F32),
            pltpu.VMEM((TM_FFN, D_FF), BF16),
        ],
        compiler_params=pltpu.CompilerParams(
            dimension_semantics=("arbitrary",), vmem_limit_bytes=VMEM_LIMIT),
        name="out_ffn",
    )(x2d, o_diff, o_hgrn, w_out, ln2_w, w_up, conv_w, conv_b, w_down)


def kernel(x, ln1_w, w_in, q_norm_w, k_norm_w, lam_q1, lam_k1, lam_q2, lam_k2, diff_subln_w,
           hgrn_lb_logits, hgrn_norm_w, w_out, ln2_w, w_up, conv_w, conv_b, w_down):
    b, s, d = x.shape
    depth = ln1_w.shape[0]
    assert depth == 1 and d == D_MODEL and s % TM_FFN == 0 and s % TQ == 0 and N_SUB % 2 == 0
    t = b * s
    x2d = x.reshape(t, d)
    l = 0
    tile2 = lambda w: jnp.concatenate([w, w], axis=-1)[None, :]

    proj, g, vt = _in_proj(x2d, ln1_w[l][None, :], w_in[l].astype(BF16), tile2(q_norm_w[l]),
                           tile2(k_norm_w[l]), hgrn_lb_logits)
    proj4 = proj.reshape(N_SLOT * N_HEADS, b, s, HEAD_W)
    g4 = g.reshape(N_HEADS, b, s, HEAD_W)

    o_diff = _diff_attn(proj4, vt, lam_q1[l][None, :], lam_k1[l][None, :], lam_q2[l][None, :],
                        lam_k2[l][None, :], diff_subln_w[l][:, None])
    o_hgrn = _hgrn2(proj4, g4, hgrn_norm_w[l][None, :])

    out = _out_ffn(x2d, o_diff.reshape(N_HEADS, t, HEAD_W), o_hgrn.reshape(N_HEADS, t, HEAD_W),
                   w_out[l].astype(BF16), ln2_w[l][None, :], w_up[l].astype(BF16), conv_w[l],
                   conv_b[l][None, :], w_down[l].astype(BF16), s)
    return out.reshape(b, s, d)
```

```python
import functools
import math

import numpy as np
import jax
import jax.numpy as jnp
from jax import lax
from jax.experimental import pallas as pl
from jax.experimental.pallas import tpu as pltpu

F32 = jnp.float32
BF16 = jnp.bfloat16

D_MODEL = 1024
CHUNK = 64
HEAD_W = 128
DIFF_HEAD_DIM = 64
N_HEADS = 4
SEC_W = N_HEADS * HEAD_W
N_SEC = 7
IN_COLS = N_SEC * SEC_W
D_FF = 2816
CONV_WIDTH = 3
EPS = 1e-6
LAM_INIT = 0.8 - 0.6 * math.exp(-0.3 * 0)
LOG2E = math.log2(math.e)

SEC_DQ, SEC_DK, SEC_DV, SEC_HQ, SEC_HK, SEC_HI, SEC_HG = range(N_SEC)
SLOT_DQ, SLOT_DK, SLOT_HQ, SLOT_HK, SLOT_HI, SLOT_HG = range(6)
N_SLOT = 6
BF16_SUBLANES = 16
V_ROWS = HEAD_W + BF16_SUBLANES

LANES = 128
SUBLANES = 8
MXU_W = 256
TM_PROJ = 1024
ATT_HEADS = 4
TQ = 1024
TK = 256
N_SUB = TQ // TK
HQ = TK
TM_FFN = 512
FF_CHUNK = 512
VMEM_LIMIT = 56 * 1024 * 1024

HGRN_CHUNK = 64
N_LEVELS = 6
N_COARSE = 3
HGRN_GROUP = 2


def _nt_dot(a, b):
    return lax.dot_general(a, b, (((1,), (1,)), ((), ())), preferred_element_type=F32)


def _tn_dot(a, b):
    return lax.dot_general(a, b, (((0,), (0,)), ((), ())), preferred_element_type=F32)


def _dot(a, b):
    return jnp.dot(a, b, preferred_element_type=F32)


def _split_bf16(x):
    hi = x.astype(BF16)
    lo = (x - hi.astype(F32)).astype(BF16)
    return hi, lo


def _sigmoid(x):
    return 1.0 / (1.0 + jnp.exp(-x))


def _in_proj_kernel(n_later, x_ref, ln1_ref, w_ref, qw_ref, kw_ref, lbl_ref, *rest):
    later_f32, (proj_ref, g_ref, vt_ref), later_bf16 = (
        rest[:n_later], rest[n_later:n_later + 3], rest[n_later + 3:])
    for src, dst in zip(later_f32, later_bf16):
        dst[...] = src[...].astype(BF16)

    x = x_ref[...]
    ms = jnp.mean(x * x, axis=-1, keepdims=True)
    h = (x * lax.rsqrt(ms + EPS) * ln1_ref[...]).astype(BF16)

    r = lax.broadcasted_iota(jnp.int32, (MXU_W, MXU_W), 0) // DIFF_HEAD_DIM
    c = lax.broadcasted_iota(jnp.int32, (MXU_W, MXU_W), 1) // DIFF_HEAD_DIM
    grp = jnp.where(r == c, 1.0, 0.0).astype(BF16)

    def head_cols(a, hd):
        return a[:, hd * HEAD_W:(hd + 1) * HEAD_W]

    def store_heads(slot, a):
        for hd in range(N_HEADS):
            proj_ref[slot * N_HEADS + hd] = head_cols(a, hd).astype(BF16)

    def project(j):
        return _dot(h, w_ref[:, j * SEC_W:(j + 1) * SEC_W])

    acc = [None] * N_SEC
    halves = [slice(c0, c0 + MXU_W) for c0 in range(0, SEC_W, MXU_W)]
    ss = {}
    for sec in (SEC_DQ, SEC_DK):
        acc[sec] = project(sec)
        ss[sec] = [_dot((acc[sec][:, cols] * acc[sec][:, cols]).astype(BF16), grp)
                   for cols in halves]
    for sec in (SEC_HK, SEC_HG, SEC_DV, SEC_HQ, SEC_HI):
        acc[sec] = project(sec)

    q_gain = qw_ref[...] * (DIFF_HEAD_DIM ** -0.5 * LOG2E)
    for sec, slot, w_norm in ((SEC_DQ, SLOT_DQ, q_gain), (SEC_DK, SLOT_DK, kw_ref[...])):
        w2 = jnp.concatenate([w_norm] * (MXU_W // HEAD_W), axis=1)
        y = [acc[sec][:, cols] * lax.rsqrt(ss[sec][n] * (1.0 / DIFF_HEAD_DIM) + EPS) * w2
             for n, cols in enumerate(halves)]
        store_heads(slot, jnp.concatenate(y, axis=1))

    for hd in range(N_HEADS):
        vt_ref[hd, 0:HEAD_W, :] = head_cols(acc[SEC_DV], hd).T.astype(BF16)
        vt_ref[hd, HEAD_W:V_ROWS, :] = jnp.ones((V_ROWS - HEAD_W, x.shape[0]), BF16)

    store_heads(SLOT_HQ, acc[SEC_HQ])
    store_heads(SLOT_HI, acc[SEC_HI])

    lbl = lbl_ref[...]
    e = jnp.exp(lbl - jnp.max(lbl, axis=0, keepdims=True))
    lb = e[0:1] / jnp.sum(e, axis=0, keepdims=True)
    f = lb + (1.0 - lb) * _sigmoid(acc[SEC_HK])
    log2_f = jnp.log(f) * LOG2E
    for hd in range(N_HEADS):
        g_ref[hd] = head_cols(log2_f, hd)
    store_heads(SLOT_HK, 1.0 - f)

    store_heads(SLOT_HG, acc[SEC_HG] * _sigmoid(acc[SEC_HG]))


def _in_proj(x2d, ln1_w, w_in, q_norm_w, k_norm_w, lb_logits, later_weights):
    t = x2d.shape[0]
    steps = t // TM_PROJ
    const = lambda *_: (0, 0)
    sliced = [w.reshape(steps, w.shape[0] // steps, w.shape[1]) for w in later_weights]
    slice_specs = [pl.BlockSpec((1,) + w.shape[1:], lambda i: (i, 0, 0)) for w in sliced]
    outs = pl.pallas_call(
        functools.partial(_in_proj_kernel, len(sliced)),
        grid=(steps,),
        in_specs=[
            pl.BlockSpec((TM_PROJ, D_MODEL), lambda i: (i, 0)),
            pl.BlockSpec((1, D_MODEL), const),
            pl.BlockSpec((D_MODEL, IN_COLS), const, pipeline_mode=pl.Buffered(1)),
            pl.BlockSpec((1, HEAD_W), const),
            pl.BlockSpec((1, HEAD_W), const),
            pl.BlockSpec(lb_logits.shape, const),
        ] + slice_specs,
        out_specs=[
            pl.BlockSpec((N_SLOT * N_HEADS, TM_PROJ, HEAD_W), lambda i: (0, i, 0)),
            pl.BlockSpec((N_HEADS, TM_PROJ, HEAD_W), lambda i: (0, i, 0)),
            pl.BlockSpec((N_HEADS, V_ROWS, TM_PROJ), lambda i: (0, 0, i)),
        ] + slice_specs,
        out_shape=[
            jax.ShapeDtypeStruct((N_SLOT * N_HEADS, t, HEAD_W), BF16),
            jax.ShapeDtypeStruct((N_HEADS, t, HEAD_W), F32),
            jax.ShapeDtypeStruct((N_HEADS, V_ROWS, t), BF16),
        ] + [jax.ShapeDtypeStruct(w.shape, BF16) for w in sliced],
        compiler_params=pltpu.CompilerParams(
            dimension_semantics=("parallel",), vmem_limit_bytes=VMEM_LIMIT),
        name="in_proj",
    )(x2d, ln1_w, w_in, q_norm_w, k_norm_w, lb_logits, *sliced)
    rounded = [o.reshape(w.shape) for o, w in zip(outs[3:], later_weights)]
    return outs[0], outs[1], outs[2], rounded


NEG_BIG = -1e30


def _diff_attn_kernel(q_ref, k_ref, vt_ref, lq1_ref, lk1_ref, lq2_ref, lk2_ref, sw_ref,
                      o_ref, qs_ref, s_ref, m_ref, acc_ref):
    qi = pl.program_id(2)
    n_hd = q_ref.shape[0]
    heads = range(n_hd)
    lane = lax.broadcasted_iota(jnp.int32, (HQ, HEAD_W), 1)
    for hd in heads:
        for grp in range(N_SUB):
            q = q_ref[hd, 0, grp * HQ:(grp + 1) * HQ, :]
            zero = jnp.zeros_like(q)
            qs_ref[hd, (2 * grp) * HQ:(2 * grp + 1) * HQ] = jnp.where(lane < DIFF_HEAD_DIM, q, zero)
            qs_ref[hd, (2 * grp + 1) * HQ:(2 * grp + 2) * HQ] = jnp.where(lane >= DIFF_HEAD_DIM, q, zero)

    m_ref[...] = jnp.full(m_ref.shape, NEG_BIG, F32)
    acc_ref[...] = jnp.zeros(acc_ref.shape, F32)

    ck = lax.broadcasted_iota(jnp.int32, (TK, 2 * TQ), 0) // CHUNK
    col = lax.broadcasted_iota(jnp.int32, (TK, 2 * TQ), 1)
    diag_mask = ck <= jnp.where(col >= 2 * HQ, TK // CHUNK, (col % HQ) // CHUNK)

    def kv_rows(j):
        return pl.ds(pl.multiple_of(j * TK, TK), TK)

    def score_step(j, slot, cols=slice(None)):
        rows = kv_rows(j)
        for hd in heads:
            s_ref[slot, hd, :, cols] = _nt_dot(k_ref[hd, 0, rows, :], qs_ref[hd, cols])

    def softmax_step(j, slot, cols=slice(None), mask=None):
        rows = kv_rows(j)
        p_all, alpha_all = [], []
        for hd in heads:
            s = s_ref[slot, hd, :, cols]
            if mask is not None:
                s = jnp.where(mask, s, NEG_BIG)
            m_old = m_ref[hd, :, cols]
            m_new = jnp.maximum(m_old, jnp.max(s, axis=0, keepdims=True))
            m_ref[hd, :, cols] = m_new
            alpha_all.append(jnp.exp2(m_old - m_new))
            p_all.append(jnp.exp2((s - m_new).astype(BF16)))
        pv_all = [_dot(vt_ref[hd, :, rows], p_all[hd]) for hd in heads]
        for hd in heads:
            acc_ref[hd, :, cols] = alpha_all[hd] * acc_ref[hd, :, cols] + pv_all[hd]

    score_step(0, 0)

    def visible_pair(jj, carry):
        j = 2 * jj
        score_step(j + 1, 1)
        softmax_step(j, 0)
        score_step(j + 2, 0)
        softmax_step(j + 1, 1)
        return carry

    lax.fori_loop(0, qi * (N_SUB // 2), visible_pair, 0)
    j0 = N_SUB * qi
    from_group = [slice(2 * HQ * r, 2 * TQ) for r in range(N_SUB)]
    for r in range(N_SUB):
        if r + 1 < N_SUB:
            score_step(j0 + r + 1, (r + 1) % 2, from_group[r + 1])
        softmax_step(j0 + r, r % 2, from_group[r], mask=diag_mask[:, 0:2 * TQ - 2 * HQ * r])

    lam = (jnp.exp(jnp.sum(lq1_ref[...] * lk1_ref[...], axis=-1, keepdims=True))
           - jnp.exp(jnp.sum(lq2_ref[...] * lk2_ref[...], axis=-1, keepdims=True)) + LAM_INIT)
    gain = sw_ref[...] * (1.0 - LAM_INIT)
    for hd in heads:
        o = acc_ref[hd, 0:HEAD_W] / acc_ref[hd, HEAD_W:HEAD_W + 1]
        for grp in range(N_SUB):
            c0 = 2 * grp * HQ
            d = o[:, c0:c0 + HQ] - lam * o[:, c0 + HQ:c0 + 2 * HQ]
            ms = jnp.mean(d * d, axis=0, keepdims=True)
            o_ref[hd, 0, grp * HQ:(grp + 1) * HQ, :] = (
                d * lax.rsqrt(ms + EPS) * gain).T.astype(BF16)


def _diff_attn(proj4, vt, lam_q1, lam_k1, lam_q2, lam_k2, subln_w):
    _, b, s, _ = proj4.shape
    groups = N_HEADS // ATT_HEADS
    const = lambda *_: (0, 0)
    lam_spec = pl.BlockSpec((1, DIFF_HEAD_DIM), const)
    return pl.pallas_call(
        _diff_attn_kernel,
        grid=(b, groups, s // TQ),
        in_specs=[
            pl.BlockSpec((ATT_HEADS, 1, TQ, HEAD_W),
                         lambda bi, gi, qi: (SLOT_DQ * groups + gi, bi, qi, 0)),
            pl.BlockSpec((ATT_HEADS, 1, s, HEAD_W),
                         lambda bi, gi, qi: (SLOT_DK * groups + gi, bi, 0, 0)),
            pl.BlockSpec((ATT_HEADS, V_ROWS, s), lambda bi, gi, qi: (gi, 0, bi)),
            lam_spec, lam_spec, lam_spec, lam_spec,
            pl.BlockSpec((HEAD_W, 1), const),
        ],
        out_specs=pl.BlockSpec((ATT_HEADS, 1, TQ, HEAD_W), lambda bi, gi, qi: (gi, bi, qi, 0)),
        out_shape=jax.ShapeDtypeStruct((N_HEADS, b, s, HEAD_W), BF16),
        scratch_shapes=[
            pltpu.VMEM((ATT_HEADS, 2 * TQ, HEAD_W), BF16),
            pltpu.VMEM((2, ATT_HEADS, TK, 2 * TQ), F32),
            pltpu.VMEM((ATT_HEADS, 1, 2 * TQ), F32),
            pltpu.VMEM((ATT_HEADS, V_ROWS, 2 * TQ), F32),
        ],
        compiler_params=pltpu.CompilerParams(
            dimension_semantics=("parallel", "parallel", "parallel"),
            vmem_limit_bytes=VMEM_LIMIT),
        name="diff_attn",
    )(proj4, proj4, vt, lam_q1, lam_k1, lam_q2, lam_k2, subln_w)


def _hgrn_constants():
    n = HGRN_CHUNK
    t = np.arange(n)[:, None]
    s = np.arange(n)[None, :]
    ltri = (s <= t)
    masks = [(s == t)]
    fine, roles = [], []
    for lvl in range(N_LEVELS):
        hs = n >> (lvl + 1)
        blk = t // (2 * hs)
        mid = blk * 2 * hs + hs - 1
        is_q = (t % (2 * hs)) >= hs
        if lvl >= N_COARSE:
            fine.append(np.where(is_q, (s > mid) & (s <= t), (s > t) & (s <= mid)))
        masks.append((blk == (s // (2 * hs))) & is_q & ((s % (2 * hs)) < hs))
        roles.append(np.broadcast_to(np.where(is_q, 1.0, -1.0), (n, HEAD_W)))
    ltri = ltri.astype(np.float32)
    wfine = np.concatenate(fine, axis=0).astype(np.float32)
    masks = np.stack(masks).astype(np.float32)
    roles = np.stack(roles).astype(np.float32)
    return ltri, wfine, masks, roles


def _hgrn_kernel(q_ref, k_ref, v_ref, gate_ref, g_ref, ltri_ref, wfine_ref, mask_ref, role_ref,
                 nw_ref, o_ref, st_ref):
    n_hd = q_ref.shape[0]
    n_chunks = q_ref.shape[2] // HGRN_CHUNK
    st_ref[...] = jnp.zeros(st_ref.shape, F32)
    ltri = ltri_ref[...]
    wfine = wfine_ref[...]
    nw = nw_ref[...]

    def exponents(g):
        g_hi, g_lo = _split_bf16(g)
        return _dot(ltri, g_hi) + _dot(ltri, g_lo), _dot(wfine, g_hi)

    def decays(b, fine):
        def row_bcast(r, n):
            return jnp.broadcast_to(b[r:r + 1, :], (n, HEAD_W))

        e_b = jnp.exp2(b)
        e_u = jnp.exp2(row_bcast(HGRN_CHUNK - 1, HGRN_CHUNK) - b)
        z = []
        for lvl in range(N_COARSE):
            hs = HGRN_CHUNK >> (lvl + 1)
            b_mid = jnp.concatenate([row_bcast(blk * 2 * hs + hs - 1, 2 * hs)
                                     for blk in range(HGRN_CHUNK // (2 * hs))], axis=0)
            z.append(jnp.exp2((b - b_mid) * role_ref[lvl]))
        fine = jnp.exp2(fine)
        z += [fine[i * HGRN_CHUNK:(i + 1) * HGRN_CHUNK] for i in range(N_LEVELS - N_COARSE)]
        return e_b, e_u, z

    def group(gi, carry):
        base = gi * (HGRN_GROUP * HGRN_CHUNK)
        items = [(hd, pl.ds(pl.multiple_of(base + u * HGRN_CHUNK, HGRN_CHUNK), HGRN_CHUNK))
                 for u in range(HGRN_GROUP) for hd in range(n_hd)]
        q_b = [q_ref[hd, 0, rows, :] for hd, rows in items]
        k_b = [k_ref[hd, 0, rows, :] for hd, rows in items]
        v_b = [v_ref[hd, 0, rows, :] for hd, rows in items]
        expo = [exponents(g_ref[hd, 0, rows, :]) for hd, rows in items]
        dec = [decays(*e) for e in expo]

        def by_role(lvl, q, k):
            if lvl >= N_COARSE:
                return jnp.where(role_ref[lvl] > 0.0, q, k)
            hs = HGRN_CHUNK >> (lvl + 1)
            return jnp.concatenate([(q if part % 2 else k)[part * hs:(part + 1) * hs]
                                    for part in range(HGRN_CHUNK // hs)], axis=0)

        xz, q_dec, k_dec = [], [], []
        for n, (e_b, e_u, z) in enumerate(dec):
            q = q_b[n].astype(F32)
            k = k_b[n].astype(F32)
            xz.append([(by_role(lvl, q, k) * z[lvl]).astype(BF16) for lvl in range(N_LEVELS)])
            q_dec.append((q * e_b).astype(BF16))
            k_dec.append((k * e_u).astype(BF16))

        pair = [[_nt_dot(q_b[n], k_b[n])] + [_nt_dot(x, x) for x in xz[n]]
                for n in range(len(items))]
        kv = [_tn_dot(v_b[n], k_dec[n]) for n in range(len(items))]
        owns = [mask_ref[lvl] > 0.5 for lvl in range(N_LEVELS + 1)]
        att = []
        for p in pair:
            a = jnp.where(owns[0], p[0], 0.0)
            for lvl in range(1, N_LEVELS + 1):
                a = jnp.where(owns[lvl], p[lvl], a)
            att.append(a.astype(BF16))
        o = [_dot(att[n], v_b[n]) for n in range(len(items))]

        st = [st_ref[hd] for hd in range(n_hd)]
        for n, (hd, rows) in enumerate(items):
            o[n] = o[n] + _nt_dot(q_dec[n], st[hd].astype(BF16))
            st[hd] = st[hd] * dec[n][0][HGRN_CHUNK - 1:HGRN_CHUNK, :] + kv[n]
        for hd in range(n_hd):
            st_ref[hd] = st[hd]

        for n, (hd, rows) in enumerate(items):
            ms = jnp.mean(o[n] * o[n], axis=-1, keepdims=True)
            y = o[n] * lax.rsqrt(ms + EPS) * nw * gate_ref[hd, 0, rows, :].astype(F32)
            o_ref[hd, 0, rows, :] = y.astype(BF16)
        return carry

    lax.fori_loop(0, n_chunks // HGRN_GROUP, group, 0)


def _hgrn2(proj4, g4, norm_w):
    _, b, s, _ = proj4.shape
    ltri, wfine, masks, roles = _hgrn_constants()
    heads_spec = lambda sec: pl.BlockSpec((N_HEADS, 1, s, HEAD_W), lambda bi: (sec, bi, 0, 0))
    return pl.pallas_call(
        _hgrn_kernel,
        grid=(b,),
        in_specs=[
            heads_spec(SLOT_HQ), heads_spec(SLOT_HK), heads_spec(SLOT_HI), heads_spec(SLOT_HG),
            heads_spec(0),
            pl.BlockSpec(ltri.shape, lambda *_: (0, 0)),
            pl.BlockSpec(wfine.shape, lambda *_: (0, 0)),
            pl.BlockSpec(masks.shape, lambda *_: (0, 0, 0)),
            pl.BlockSpec(roles.shape, lambda *_: (0, 0, 0)),
            pl.BlockSpec((1, HEAD_W), lambda *_: (0, 0)),
        ],
        out_specs=heads_spec(0),
        out_shape=jax.ShapeDtypeStruct((N_HEADS, b, s, HEAD_W), BF16),
        scratch_shapes=[pltpu.VMEM((N_HEADS, HEAD_W, HEAD_W), F32)],
        compiler_params=pltpu.CompilerParams(
            dimension_semantics=("parallel",), vmem_limit_bytes=VMEM_LIMIT),
        name="hgrn2",
    )(proj4, proj4, proj4, proj4, g4, jnp.asarray(ltri, BF16), jnp.asarray(wfine, BF16),
      jnp.asarray(masks), jnp.asarray(roles), norm_w)


def _ff_chunks():
    chunks, c0 = [], 0
    while c0 < D_FF:
        fc = min(FF_CHUNK, D_FF - c0)
        chunks.append((c0, fc))
        c0 += fc
    return chunks


def _out_ffn_kernel(tiles_per_seq, x_ref, od_ref, oh_ref, wout_ref, ln2_ref, wup_ref, cw_ref,
                    cb_ref, wdn_ref, out_ref, tail_ref, act_ref):
    i = pl.program_id(0)
    mix = jnp.concatenate([od_ref[hd] for hd in range(N_HEADS)]
                          + [oh_ref[hd] for hd in range(N_HEADS)], axis=1)
    x1 = x_ref[...] + _dot(mix, wout_ref[...])
    ms = jnp.mean(x1 * x1, axis=-1, keepdims=True)
    h2 = (x1 * lax.rsqrt(ms + EPS) * ln2_ref[...]).astype(BF16)

    seq_start = (i % tiles_per_seq) == 0
    tm = x1.shape[0]
    for c0, fc in _ff_chunks():
        cols = slice(c0, c0 + fc)
        u = _dot(h2, wup_ref[:, cols])
        v = _dot(h2, wup_ref[:, D_FF + c0:D_FF + c0 + fc])
        tail = jnp.where(seq_start, 0.0, tail_ref[:, cols])
        tail_ref[:, cols] = u[tm - SUBLANES:, :]
        ext = jnp.concatenate([tail, u], axis=0)
        u1 = pltpu.roll(ext, 1, 0)[SUBLANES:]
        u2 = pltpu.roll(ext, 2, 0)[SUBLANES:]
        cw = cw_ref[:, cols]
        c = cb_ref[:, cols] + u2 * cw[0:1] + u1 * cw[1:2] + u * cw[2:3]
        act_ref[:, cols] = (c * _sigmoid(c) * v).astype(BF16)
    out_ref[...] = x1 + _dot(act_ref[...], wdn_ref[...])


def _out_ffn(x2d, o_diff, o_hgrn, w_out, ln2_w, w_up, conv_w, conv_b, w_down, seq_len):
    t = x2d.shape[0]
    const = lambda *_: (0, 0)
    resident = functools.partial(pl.BlockSpec, index_map=const, pipeline_mode=pl.Buffered(1))
    return pl.pallas_call(
        functools.partial(_out_ffn_kernel, seq_len // TM_FFN),
        grid=(t // TM_FFN,),
        in_specs=[
            pl.BlockSpec((TM_FFN, D_MODEL), lambda i: (i, 0)),
            pl.BlockSpec((N_HEADS, TM_FFN, HEAD_W), lambda i: (0, i, 0)),
            pl.BlockSpec((N_HEADS, TM_FFN, HEAD_W), lambda i: (0, i, 0)),
            resident((D_MODEL, D_MODEL)),
            pl.BlockSpec((1, D_MODEL), const),
            resident((D_MODEL, 2 * D_FF)),
            pl.BlockSpec((CONV_WIDTH, D_FF), const),
            pl.BlockSpec((1, D_FF), const),
            resident((D_FF, D_MODEL)),
        ],
        out_specs=pl.BlockSpec((TM_FFN, D_MODEL), lambda i: (i, 0)),
        out_shape=jax.ShapeDtypeStruct((t, D_MODEL), F32),
        scratch_shapes=[
            pltpu.VMEM((SUBLANES, D_FF), F32),
            pltpu.VMEM((TM_FFN, D_FF), BF16),
        ],
        compiler_params=pltpu.CompilerParams(
            dimension_semantics=("arbitrary",), vmem_limit_bytes=VMEM_LIMIT),
        name="out_ffn",
    )(x2d, o_diff, o_hgrn, w_out, ln2_w, w_up, conv_w, conv_b, w_down)


def kernel(x, ln1_w, w_in, q_norm_w, k_norm_w, lam_q1, lam_k1, lam_q2, lam_k2, diff_subln_w,
           hgrn_lb_logits, hgrn_norm_w, w_out, ln2_w, w_up, conv_w, conv_b, w_down):
    b, s, d = x.shape
    depth = ln1_w.shape[0]
    assert depth == 1 and d == D_MODEL and s % TM_FFN == 0 and s % TQ == 0 and N_SUB % 2 == 0
    t = b * s
    x2d = x.reshape(t, d)
    l = 0
    tile2 = lambda w: jnp.concatenate([w, w], axis=-1)[None, :]

    proj, g, vt, (w_out_b, w_up_b, w_down_b) = _in_proj(
        x2d, ln1_w[l][None, :], w_in[l].astype(BF16), tile2(q_norm_w[l]), tile2(k_norm_w[l]),
        hgrn_lb_logits, [w_out[l], w_up[l], w_down[l]])
    proj4 = proj.reshape(N_SLOT * N_HEADS, b, s, HEAD_W)
    g4 = g.reshape(N_HEADS, b, s, HEAD_W)

    o_diff = _diff_attn(proj4, vt, lam_q1[l][None, :], lam_k1[l][None, :], lam_q2[l][None, :],
                        lam_k2[l][None, :], diff_subln_w[l][:, None])
    o_hgrn = _hgrn2(proj4, g4, hgrn_norm_w[l][None, :])

    out = _out_ffn(x2d, o_diff.reshape(N_HEADS, t, HEAD_W), o_hgrn.reshape(N_HEADS, t, HEAD_W),
                   w_out_b, ln2_w[l][None, :], w_up_b, conv_w[l], conv_b[l][None, :], w_down_b, s)
    return out.reshape(b, s, d)
```

```python
import functools
import math

import numpy as np
import jax
import jax.numpy as jnp
from jax import lax
from jax.experimental import pallas as pl
from jax.experimental.pallas import tpu as pltpu

F32 = jnp.float32
BF16 = jnp.bfloat16

D_MODEL = 1024
CHUNK = 64
HEAD_W = 128
DIFF_HEAD_DIM = 64
N_HEADS = 4
SEC_W = N_HEADS * HEAD_W
N_SEC = 7
IN_COLS = N_SEC * SEC_W
D_FF = 2816
CONV_WIDTH = 3
EPS = 1e-6
LAM_INIT = 0.8 - 0.6 * math.exp(-0.3 * 0)
LOG2E = math.log2(math.e)

SEC_DQ, SEC_DK, SEC_DV, SEC_HQ, SEC_HK, SEC_HI, SEC_HG = range(N_SEC)
SLOT_DQ, SLOT_DK, SLOT_HQ, SLOT_HK, SLOT_HI, SLOT_HG = range(6)
N_SLOT = 6
BF16_SUBLANES = 16
V_ROWS = HEAD_W + BF16_SUBLANES

LANES = 128
SUBLANES = 8
MXU_W = 256
TM_PROJ = 1024
ATT_HEADS = 4
TQ = 1024
TK = 256
N_SUB = TQ // TK
HQ = TK
TM_FFN = 512
FF_CHUNK = 512
VMEM_LIMIT = 56 * 1024 * 1024

HGRN_CHUNK = 64
N_LEVELS = 6
N_COARSE = 3
HGRN_GROUP = 4


def _nt_dot(a, b):
    return lax.dot_general(a, b, (((1,), (1,)), ((), ())), preferred_element_type=F32)


def _tn_dot(a, b):
    return lax.dot_general(a, b, (((0,), (0,)), ((), ())), preferred_element_type=F32)


def _dot(a, b):
    return jnp.dot(a, b, preferred_element_type=F32)


def _split_bf16(x):
    hi = x.astype(BF16)
    lo = (x - hi.astype(F32)).astype(BF16)
    return hi, lo


def _sigmoid(x):
    return 1.0 / (1.0 + jnp.exp(-x))


def _in_proj_kernel(n_later, x_ref, ln1_ref, w_ref, qw_ref, kw_ref, lbl_ref, *rest):
    later_f32, (proj_ref, g_ref, vt_ref), later_bf16 = (
        rest[:n_later], rest[n_later:n_later + 3], rest[n_later + 3:])
    for src, dst in zip(later_f32, later_bf16):
        dst[...] = src[...].astype(BF16)

    x = x_ref[...]
    ms = jnp.mean(x * x, axis=-1, keepdims=True)
    h = (x * lax.rsqrt(ms + EPS) * ln1_ref[...]).astype(BF16)

    r = lax.broadcasted_iota(jnp.int32, (MXU_W, MXU_W), 0) // DIFF_HEAD_DIM
    c = lax.broadcasted_iota(jnp.int32, (MXU_W, MXU_W), 1) // DIFF_HEAD_DIM
    grp = jnp.where(r == c, 1.0, 0.0).astype(BF16)

    def head_cols(a, hd):
        return a[:, hd * HEAD_W:(hd + 1) * HEAD_W]

    def store_heads(slot, a):
        for hd in range(N_HEADS):
            proj_ref[slot * N_HEADS + hd] = head_cols(a, hd).astype(BF16)

    def project(j):
        return _dot(h, w_ref[:, j * SEC_W:(j + 1) * SEC_W])

    acc = [None] * N_SEC
    halves = [slice(c0, c0 + MXU_W) for c0 in range(0, SEC_W, MXU_W)]
    ss = {}
    for sec in (SEC_DQ, SEC_DK):
        acc[sec] = project(sec)
        ss[sec] = [_dot((acc[sec][:, cols] * acc[sec][:, cols]).astype(BF16), grp)
                   for cols in halves]
    for sec in (SEC_HK, SEC_HG, SEC_DV, SEC_HQ, SEC_HI):
        acc[sec] = project(sec)

    q_gain = qw_ref[...] * (DIFF_HEAD_DIM ** -0.5 * LOG2E)
    for sec, slot, w_norm in ((SEC_DQ, SLOT_DQ, q_gain), (SEC_DK, SLOT_DK, kw_ref[...])):
        w2 = jnp.concatenate([w_norm] * (MXU_W // HEAD_W), axis=1)
        y = [acc[sec][:, cols] * lax.rsqrt(ss[sec][n] * (1.0 / DIFF_HEAD_DIM) + EPS) * w2
             for n, cols in enumerate(halves)]
        store_heads(slot, jnp.concatenate(y, axis=1))

    for hd in range(N_HEADS):
        vt_ref[hd, 0:HEAD_W, :] = head_cols(acc[SEC_DV], hd).T.astype(BF16)
        vt_ref[hd, HEAD_W:V_ROWS, :] = jnp.ones((V_ROWS - HEAD_W, x.shape[0]), BF16)

    store_heads(SLOT_HQ, acc[SEC_HQ])
    store_heads(SLOT_HI, acc[SEC_HI])

    lbl = lbl_ref[...]
    e = jnp.exp(lbl - jnp.max(lbl, axis=0, keepdims=True))
    lb = e[0:1] / jnp.sum(e, axis=0, keepdims=True)
    f = lb + (1.0 - lb) * _sigmoid(acc[SEC_HK])
    log2_f = jnp.log(f) * LOG2E
    for hd in range(N_HEADS):
        g_ref[hd] = head_cols(log2_f, hd)
    store_heads(SLOT_HK, 1.0 - f)

    store_heads(SLOT_HG, acc[SEC_HG] * _sigmoid(acc[SEC_HG]))


def _in_proj(x2d, ln1_w, w_in, q_norm_w, k_norm_w, lb_logits, later_weights):
    t = x2d.shape[0]
    steps = t // TM_PROJ
    const = lambda *_: (0, 0)
    sliced = [w.reshape(steps, w.shape[0] // steps, w.shape[1]) for w in later_weights]
    slice_specs = [pl.BlockSpec((1,) + w.shape[1:], lambda i: (i, 0, 0)) for w in sliced]
    outs = pl.pallas_call(
        functools.partial(_in_proj_kernel, len(sliced)),
        grid=(steps,),
        in_specs=[
            pl.BlockSpec((TM_PROJ, D_MODEL), lambda i: (i, 0)),
            pl.BlockSpec((1, D_MODEL), const),
            pl.BlockSpec((D_MODEL, IN_COLS), const, pipeline_mode=pl.Buffered(1)),
            pl.BlockSpec((1, HEAD_W), const),
            pl.BlockSpec((1, HEAD_W), const),
            pl.BlockSpec(lb_logits.shape, const),
        ] + slice_specs,
        out_specs=[
            pl.BlockSpec((N_SLOT * N_HEADS, TM_PROJ, HEAD_W), lambda i: (0, i, 0)),
            pl.BlockSpec((N_HEADS, TM_PROJ, HEAD_W), lambda i: (0, i, 0)),
            pl.BlockSpec((N_HEADS, V_ROWS, TM_PROJ), lambda i: (0, 0, i)),
        ] + slice_specs,
        out_shape=[
            jax.ShapeDtypeStruct((N_SLOT * N_HEADS, t, HEAD_W), BF16),
            jax.ShapeDtypeStruct((N_HEADS, t, HEAD_W), F32),
            jax.ShapeDtypeStruct((N_HEADS, V_ROWS, t), BF16),
        ] + [jax.ShapeDtypeStruct(w.shape, BF16) for w in sliced],
        compiler_params=pltpu.CompilerParams(
            dimension_semantics=("parallel",), vmem_limit_bytes=VMEM_LIMIT),
        name="in_proj",
    )(x2d, ln1_w, w_in, q_norm_w, k_norm_w, lb_logits, *sliced)
    rounded = [o.reshape(w.shape) for o, w in zip(outs[3:], later_weights)]
    return outs[0], outs[1], outs[2], rounded


NEG_BIG = -1e30


def _diff_attn_kernel(q_ref, k_ref, vt_ref, lq1_ref, lk1_ref, lq2_ref, lk2_ref, sw_ref,
                      o_ref, qs_ref, s_ref, m_ref, acc_ref):
    qi = pl.program_id(2)
    n_hd = q_ref.shape[0]
    heads = range(n_hd)
    lane = lax.broadcasted_iota(jnp.int32, (HQ, HEAD_W), 1)
    for hd in heads:
        for grp in range(N_SUB):
            q = q_ref[hd, 0, grp * HQ:(grp + 1) * HQ, :]
            zero = jnp.zeros_like(q)
            qs_ref[hd, (2 * grp) * HQ:(2 * grp + 1) * HQ] = jnp.where(lane < DIFF_HEAD_DIM, q, zero)
            qs_ref[hd, (2 * grp + 1) * HQ:(2 * grp + 2) * HQ] = jnp.where(lane >= DIFF_HEAD_DIM, q, zero)

    m_ref[...] = jnp.full(m_ref.shape, NEG_BIG, F32)
    acc_ref[...] = jnp.zeros(acc_ref.shape, F32)

    ck = lax.broadcasted_iota(jnp.int32, (TK, 2 * TQ), 0) // CHUNK
    col = lax.broadcasted_iota(jnp.int32, (TK, 2 * TQ), 1)
    diag_mask = ck <= jnp.where(col >= 2 * HQ, TK // CHUNK, (col % HQ) // CHUNK)

    def kv_rows(j):
        return pl.ds(pl.multiple_of(j * TK, TK), TK)

    def score_step(j, slot, cols=slice(None)):
        rows = kv_rows(j)
        for hd in heads:
            s_ref[slot, hd, :, cols] = _nt_dot(k_ref[hd, 0, rows, :], qs_ref[hd, cols])

    def softmax_step(j, slot, cols=slice(None), mask=None):
        rows = kv_rows(j)
        p_all, alpha_all = [], []
        for hd in heads:
            s = s_ref[slot, hd, :, cols]
            if mask is not None:
                s = jnp.where(mask, s, NEG_BIG)
            m_old = m_ref[hd, :, cols]
            m_new = jnp.maximum(m_old, jnp.max(s, axis=0, keepdims=True))
            m_ref[hd, :, cols] = m_new
            alpha_all.append(jnp.exp2(m_old - m_new))
            p_all.append(jnp.exp2((s - m_new).astype(BF16)))
        pv_all = [_dot(vt_ref[hd, :, rows], p_all[hd]) for hd in heads]
        for hd in heads:
            acc_ref[hd, :, cols] = alpha_all[hd] * acc_ref[hd, :, cols] + pv_all[hd]

    score_step(0, 0)

    def visible_pair(jj, carry):
        j = 2 * jj
        score_step(j + 1, 1)
        softmax_step(j, 0)
        score_step(j + 2, 0)
        softmax_step(j + 1, 1)
        return carry

    lax.fori_loop(0, qi * (N_SUB // 2), visible_pair, 0)
    j0 = N_SUB * qi
    from_group = [slice(2 * HQ * r, 2 * TQ) for r in range(N_SUB)]
    for r in range(N_SUB):
        if r + 1 < N_SUB:
            score_step(j0 + r + 1, (r + 1) % 2, from_group[r + 1])
        softmax_step(j0 + r, r % 2, from_group[r], mask=diag_mask[:, 0:2 * TQ - 2 * HQ * r])

    lam = (jnp.exp(jnp.sum(lq1_ref[...] * lk1_ref[...], axis=-1, keepdims=True))
           - jnp.exp(jnp.sum(lq2_ref[...] * lk2_ref[...], axis=-1, keepdims=True)) + LAM_INIT)
    gain = sw_ref[...] * (1.0 - LAM_INIT)
    for hd in heads:
        o = acc_ref[hd, 0:HEAD_W] / acc_ref[hd, HEAD_W:HEAD_W + 1]
        for grp in range(N_SUB):
            c0 = 2 * grp * HQ
            d = o[:, c0:c0 + HQ] - lam * o[:, c0 + HQ:c0 + 2 * HQ]
            ms = jnp.mean(d * d, axis=0, keepdims=True)
            o_ref[hd, 0, grp * HQ:(grp + 1) * HQ, :] = (
                d * lax.rsqrt(ms + EPS) * gain).T.astype(BF16)


def _diff_attn(proj4, vt, lam_q1, lam_k1, lam_q2, lam_k2, subln_w):
    _, b, s, _ = proj4.shape
    groups = N_HEADS // ATT_HEADS
    const = lambda *_: (0, 0)
    lam_spec = pl.BlockSpec((1, DIFF_HEAD_DIM), const)
    return pl.pallas_call(
        _diff_attn_kernel,
        grid=(b, groups, s // TQ),
        in_specs=[
            pl.BlockSpec((ATT_HEADS, 1, TQ, HEAD_W),
                         lambda bi, gi, qi: (SLOT_DQ * groups + gi, bi, qi, 0)),
            pl.BlockSpec((ATT_HEADS, 1, s, HEAD_W),
                         lambda bi, gi, qi: (SLOT_DK * groups + gi, bi, 0, 0)),
            pl.BlockSpec((ATT_HEADS, V_ROWS, s), lambda bi, gi, qi: (gi, 0, bi)),
            lam_spec, lam_spec, lam_spec, lam_spec,
            pl.BlockSpec((HEAD_W, 1), const),
        ],
        out_specs=pl.BlockSpec((ATT_HEADS, 1, TQ, HEAD_W), lambda bi, gi, qi: (gi, bi, qi, 0)),
        out_shape=jax.ShapeDtypeStruct((N_HEADS, b, s, HEAD_W), BF16),
        scratch_shapes=[
            pltpu.VMEM((ATT_HEADS, 2 * TQ, HEAD_W), BF16),
            pltpu.VMEM((2, ATT_HEADS, TK, 2 * TQ), F32),
            pltpu.VMEM((ATT_HEADS, 1, 2 * TQ), F32),
            pltpu.VMEM((ATT_HEADS, V_ROWS, 2 * TQ), F32),
        ],
        compiler_params=pltpu.CompilerParams(
            dimension_semantics=("parallel", "parallel", "parallel"),
            vmem_limit_bytes=VMEM_LIMIT),
        name="diff_attn",
    )(proj4, proj4, vt, lam_q1, lam_k1, lam_q2, lam_k2, subln_w)


def _hgrn_constants():
    n = HGRN_CHUNK
    t = np.arange(n)[:, None]
    s = np.arange(n)[None, :]
    ltri = (s <= t)
    masks = [(s == t)]
    fine, roles = [], []
    for lvl in range(N_LEVELS):
        hs = n >> (lvl + 1)
        blk = t // (2 * hs)
        mid = blk * 2 * hs + hs - 1
        is_q = (t % (2 * hs)) >= hs
        if lvl >= N_COARSE:
            fine.append(np.where(is_q, (s > mid) & (s <= t), (s > t) & (s <= mid)))
        masks.append((blk == (s // (2 * hs))) & is_q & ((s % (2 * hs)) < hs))
        roles.append(np.broadcast_to(np.where(is_q, 1.0, -1.0), (n, HEAD_W)))
    ltri = ltri.astype(np.float32)
    wfine = np.concatenate(fine, axis=0).astype(np.float32)
    masks = np.stack(masks).astype(np.float32)
    roles = np.stack(roles).astype(np.float32)
    return ltri, wfine, masks, roles


def _hgrn_kernel(q_ref, k_ref, v_ref, gate_ref, g_ref, ltri_ref, wfine_ref, mask_ref, role_ref,
                 nw_ref, o_ref, st_ref):
    n_hd = q_ref.shape[0]
    n_chunks = q_ref.shape[2] // HGRN_CHUNK
    st_ref[...] = jnp.zeros(st_ref.shape, F32)
    ltri = ltri_ref[...]
    wfine = wfine_ref[...]
    nw = nw_ref[...]

    def exponents(g):
        g_hi, g_lo = _split_bf16(g)
        return _dot(ltri, g_hi) + _dot(ltri, g_lo), _dot(wfine, g_hi)

    def decays(b, fine):
        def row_bcast(r, n):
            return jnp.broadcast_to(b[r:r + 1, :], (n, HEAD_W))

        e_b = jnp.exp2(b)
        e_u = jnp.exp2(row_bcast(HGRN_CHUNK - 1, HGRN_CHUNK) - b)
        z = []
        for lvl in range(N_COARSE):
            hs = HGRN_CHUNK >> (lvl + 1)
            b_mid = jnp.concatenate([row_bcast(blk * 2 * hs + hs - 1, 2 * hs)
                                     for blk in range(HGRN_CHUNK // (2 * hs))], axis=0)
            z.append(jnp.exp2((b - b_mid) * role_ref[lvl]))
        fine = jnp.exp2(fine)
        z += [fine[i * HGRN_CHUNK:(i + 1) * HGRN_CHUNK] for i in range(N_LEVELS - N_COARSE)]
        return e_b, e_u, z

    def group(gi, carry):
        base = gi * (HGRN_GROUP * HGRN_CHUNK)
        items = [(hd, pl.ds(pl.multiple_of(base + u * HGRN_CHUNK, HGRN_CHUNK), HGRN_CHUNK))
                 for u in range(HGRN_GROUP) for hd in range(n_hd)]
        q_b = [q_ref[hd, 0, rows, :] for hd, rows in items]
        k_b = [k_ref[hd, 0, rows, :] for hd, rows in items]
        v_b = [v_ref[hd, 0, rows, :] for hd, rows in items]
        expo = [exponents(g_ref[hd, 0, rows, :]) for hd, rows in items]
        dec = [decays(*e) for e in expo]

        def by_role(lvl, q, k):
            if lvl >= N_COARSE:
                return jnp.where(role_ref[lvl] > 0.0, q, k)
            hs = HGRN_CHUNK >> (lvl + 1)
            return jnp.concatenate([(q if part % 2 else k)[part * hs:(part + 1) * hs]
                                    for part in range(HGRN_CHUNK // hs)], axis=0)

        xz, q_dec, k_dec = [], [], []
        for n, (e_b, e_u, z) in enumerate(dec):
            q = q_b[n].astype(F32)
            k = k_b[n].astype(F32)
            xz.append([(by_role(lvl, q, k) * z[lvl]).astype(BF16) for lvl in range(N_LEVELS)])
            q_dec.append((q * e_b).astype(BF16))
            k_dec.append((k * e_u).astype(BF16))

        pair = [[_nt_dot(q_b[n], k_b[n])] + [_nt_dot(x, x) for x in xz[n]]
                for n in range(len(items))]
        kv = [_tn_dot(v_b[n], k_dec[n]) for n in range(len(items))]
        owns = [mask_ref[lvl] > 0.5 for lvl in range(N_LEVELS + 1)]
        att = []
        for p in pair:
            a = jnp.where(owns[0], p[0], 0.0)
            for lvl in range(1, N_LEVELS + 1):
                a = jnp.where(owns[lvl], p[lvl], a)
            att.append(a.astype(BF16))
        o = [_dot(att[n], v_b[n]) for n in range(len(items))]

        st = [st_ref[hd] for hd in range(n_hd)]
        for n, (hd, rows) in enumerate(items):
            o[n] = o[n] + _nt_dot(q_dec[n], st[hd].astype(BF16))
            st[hd] = st[hd] * dec[n][0][HGRN_CHUNK - 1:HGRN_CHUNK, :] + kv[n]
        for hd in range(n_hd):
            st_ref[hd] = st[hd]

        for n, (hd, rows) in enumerate(items):
            ms = jnp.mean(o[n] * o[n], axis=-1, keepdims=True)
            y = o[n] * lax.rsqrt(ms + EPS) * nw * gate_ref[hd, 0, rows, :].astype(F32)
            o_ref[hd, 0, rows, :] = y.astype(BF16)
        return carry

    lax.fori_loop(0, n_chunks // HGRN_GROUP, group, 0)


def _hgrn2(proj4, g4, norm_w):
    _, b, s, _ = proj4.shape
    ltri, wfine, masks, roles = _hgrn_constants()
    heads_spec = lambda sec: pl.BlockSpec((N_HEADS, 1, s, HEAD_W), lambda bi: (sec, bi, 0, 0))
    return pl.pallas_call(
        _hgrn_kernel,
        grid=(b,),
        in_specs=[
            heads_spec(SLOT_HQ), heads_spec(SLOT_HK), heads_spec(SLOT_HI), heads_spec(SLOT_HG),
            heads_spec(0),
            pl.BlockSpec(ltri.shape, lambda *_: (0, 0)),
            pl.BlockSpec(wfine.shape, lambda *_: (0, 0)),
            pl.BlockSpec(masks.shape, lambda *_: (0, 0, 0)),
            pl.BlockSpec(roles.shape, lambda *_: (0, 0, 0)),
            pl.BlockSpec((1, HEAD_W), lambda *_: (0, 0)),
        ],
        out_specs=heads_spec(0),
        out_shape=jax.ShapeDtypeStruct((N_HEADS, b, s, HEAD_W), BF16),
        scratch_shapes=[pltpu.VMEM((N_HEADS, HEAD_W, HEAD_W), F32)],
        compiler_params=pltpu.CompilerParams(
            dimension_semantics=("parallel",), vmem_limit_bytes=VMEM_LIMIT),
        name="hgrn2",
    )(proj4, proj4, proj4, proj4, g4, jnp.asarray(ltri, BF16), jnp.asarray(wfine, BF16),
      jnp.asarray(masks), jnp.asarray(roles), norm_w)


def _ff_chunks():
    chunks, c0 = [], 0
    while c0 < D_FF:
        fc = min(FF_CHUNK, D_FF - c0)
        chunks.append((c0, fc))
        c0 += fc
    return chunks


def _out_ffn_kernel(tiles_per_seq, x_ref, od_ref, oh_ref, wout_ref, ln2_ref, wup_ref, cw_ref,
                    cb_ref, wdn_ref, out_ref, tail_ref, act_ref):
    i = pl.program_id(0)
    mix = jnp.concatenate([od_ref[hd] for hd in range(N_HEADS)]
                          + [oh_ref[hd] for hd in range(N_HEADS)], axis=1)
    x1 = x_ref[...] + _dot(mix, wout_ref[...])
    ms = jnp.mean(x1 * x1, axis=-1, keepdims=True)
    h2 = (x1 * lax.rsqrt(ms + EPS) * ln2_ref[...]).astype(BF16)

    seq_start = (i % tiles_per_seq) == 0
    tm = x1.shape[0]
    for c0, fc in _ff_chunks():
        cols = slice(c0, c0 + fc)
        u = _dot(h2, wup_ref[:, cols])
        v = _dot(h2, wup_ref[:, D_FF + c0:D_FF + c0 + fc])
        tail = jnp.where(seq_start, 0.0, tail_ref[:, cols])
        tail_ref[:, cols] = u[tm - SUBLANES:, :]
        ext = jnp.concatenate([tail, u], axis=0)
        u1 = pltpu.roll(ext, 1, 0)[SUBLANES:]
        u2 = pltpu.roll(ext, 2, 0)[SUBLANES:]
        cw = cw_ref[:, cols]
        c = cb_ref[:, cols] + u2 * cw[0:1] + u1 * cw[1:2] + u * cw[2:3]
        act_ref[:, cols] = (c * _sigmoid(c) * v).astype(BF16)
    out_ref[...] = x1 + _dot(act_ref[...], wdn_ref[...])


def _out_ffn(x2d, o_diff, o_hgrn, w_out, ln2_w, w_up, conv_w, conv_b, w_down, seq_len):
    t = x2d.shape[0]
    const = lambda *_: (0, 0)
    resident = functools.partial(pl.BlockSpec, index_map=const, pipeline_mode=pl.Buffered(1))
    return pl.pallas_call(
        functools.partial(_out_ffn_kernel, seq_len // TM_FFN),
        grid=(t // TM_FFN,),
        in_specs=[
            pl.BlockSpec((TM_FFN, D_MODEL), lambda i: (i, 0)),
            pl.BlockSpec((N_HEADS, TM_FFN, HEAD_W), lambda i: (0, i, 0)),
            pl.BlockSpec((N_HEADS, TM_FFN, HEAD_W), lambda i: (0, i, 0)),
            resident((D_MODEL, D_MODEL)),
            pl.BlockSpec((1, D_MODEL), const),
            resident((D_MODEL, 2 * D_FF)),
            pl.BlockSpec((CONV_WIDTH, D_FF), const),
            pl.BlockSpec((1, D_FF), const),
            resident((D_FF, D_MODEL)),
        ],
        out_specs=pl.BlockSpec((TM_FFN, D_MODEL), lambda i: (i, 0)),
        out_shape=jax.ShapeDtypeStruct((t, D_MODEL), F32),
        scratch_shapes=[
            pltpu.VMEM((SUBLANES, D_FF), F32),
            pltpu.VMEM((TM_FFN, D_FF), BF16),
        ],
        compiler_params=pltpu.CompilerParams(
            dimension_semantics=("arbitrary",), vmem_limit_bytes=VMEM_LIMIT),
        name="out_ffn",
    )(x2d, o_diff, o_hgrn, w_out, ln2_w, w_up, conv_w, conv_b, w_down)


def kernel(x, ln1_w, w_in, q_norm_w, k_norm_w, lam_q1, lam_k1, lam_q2, lam_k2, diff_subln_w,
           hgrn_lb_logits, hgrn_norm_w, w_out, ln2_w, w_up, conv_w, conv_b, w_down):
    b, s, d = x.shape
    depth = ln1_w.shape[0]
    assert depth == 1 and d == D_MODEL and s % TM_FFN == 0 and s % TQ == 0 and N_SUB % 2 == 0
    t = b * s
    x2d = x.reshape(t, d)
    l = 0
    tile2 = lambda w: jnp.concatenate([w, w], axis=-1)[None, :]

    proj, g, vt, (w_out_b, w_up_b, w_down_b) = _in_proj(
        x2d, ln1_w[l][None, :], w_in[l].astype(BF16), tile2(q_norm_w[l]), tile2(k_norm_w[l]),
        hgrn_lb_logits, [w_out[l], w_up[l], w_down[l]])
    proj4 = proj.reshape(N_SLOT * N_HEADS, b, s, HEAD_W)
    g4 = g.reshape(N_HEADS, b, s, HEAD_W)

    o_diff = _diff_attn(proj4, vt, lam_q1[l][None, :], lam_k1[l][None, :], lam_q2[l][None, :],
                        lam_k2[l][None, :], diff_subln_w[l][:, None])
    o_hgrn = _hgrn2(proj4, g4, hgrn_norm_w[l][None, :])

    out = _out_ffn(x2d, o_diff.reshape(N_HEADS, t, HEAD_W), o_hgrn.reshape(N_HEADS, t, HEAD_W),
                   w_out_b, ln2_w[l][None, :], w_up_b, conv_w[l], conv_b[l][None, :], w_down_b, s)
    return out.reshape(b, s, d)
```

```python
import functools
import math

import numpy as np
import jax
import jax.numpy as jnp
from jax import lax
from jax.experimental import pallas as pl
from jax.experimental.pallas import tpu as pltpu

F32 = jnp.float32
BF16 = jnp.bfloat16

D_MODEL = 1024
CHUNK = 64
HEAD_W = 128
DIFF_HEAD_DIM = 64
N_HEADS = 4
SEC_W = N_HEADS * HEAD_W
N_SEC = 7
IN_COLS = N_SEC * SEC_W
D_FF = 2816
CONV_WIDTH = 3
EPS = 1e-6
LAM_INIT = 0.8 - 0.6 * math.exp(-0.3 * 0)
LOG2E = math.log2(math.e)

SEC_DQ, SEC_DK, SEC_DV, SEC_HQ, SEC_HK, SEC_HI, SEC_HG = range(N_SEC)
SLOT_DQ, SLOT_DK, SLOT_HQ, SLOT_HK, SLOT_HI, SLOT_HG = range(6)
N_SLOT = 6
BF16_SUBLANES = 16
V_ROWS = HEAD_W + BF16_SUBLANES

LANES = 128
SUBLANES = 8
MXU_W = 256
TM_PROJ = 1024
ATT_HEADS = 4
TQ = 1024
TK = 256
N_SUB = TQ // TK
HQ = TK
TM_FFN = 512
FF_CHUNK = 768
VMEM_LIMIT = 56 * 1024 * 1024

HGRN_CHUNK = 64
N_LEVELS = 6
N_COARSE = 3
HGRN_GROUP = 4


def _nt_dot(a, b):
    return lax.dot_general(a, b, (((1,), (1,)), ((), ())), preferred_element_type=F32)


def _tn_dot(a, b):
    return lax.dot_general(a, b, (((0,), (0,)), ((), ())), preferred_element_type=F32)


def _dot(a, b):
    return jnp.dot(a, b, preferred_element_type=F32)


def _split_bf16(x):
    hi = x.astype(BF16)
    lo = (x - hi.astype(F32)).astype(BF16)
    return hi, lo


def _sigmoid(x):
    return 1.0 / (1.0 + jnp.exp(-x))


def _in_proj_kernel(n_later, x_ref, ln1_ref, w_ref, qw_ref, kw_ref, lbl_ref, *rest):
    later_f32, (proj_ref, g_ref, vt_ref), later_bf16 = (
        rest[:n_later], rest[n_later:n_later + 3], rest[n_later + 3:])
    for src, dst in zip(later_f32, later_bf16):
        dst[...] = src[...].astype(BF16)

    x = x_ref[...]
    ms = jnp.mean(x * x, axis=-1, keepdims=True)
    h = (x * lax.rsqrt(ms + EPS) * ln1_ref[...]).astype(BF16)

    r = lax.broadcasted_iota(jnp.int32, (MXU_W, MXU_W), 0) // DIFF_HEAD_DIM
    c = lax.broadcasted_iota(jnp.int32, (MXU_W, MXU_W), 1) // DIFF_HEAD_DIM
    grp = jnp.where(r == c, 1.0, 0.0).astype(BF16)

    def head_cols(a, hd):
        return a[:, hd * HEAD_W:(hd + 1) * HEAD_W]

    def store_heads(slot, a):
        for hd in range(N_HEADS):
            proj_ref[slot * N_HEADS + hd] = head_cols(a, hd).astype(BF16)

    def project(j):
        return _dot(h, w_ref[:, j * SEC_W:(j + 1) * SEC_W])

    acc = [None] * N_SEC
    halves = [slice(c0, c0 + MXU_W) for c0 in range(0, SEC_W, MXU_W)]
    ss = {}
    for sec in (SEC_DQ, SEC_DK):
        acc[sec] = project(sec)
        ss[sec] = [_dot((acc[sec][:, cols] * acc[sec][:, cols]).astype(BF16), grp)
                   for cols in halves]
    for sec in (SEC_HK, SEC_HG, SEC_DV, SEC_HQ, SEC_HI):
        acc[sec] = project(sec)

    q_gain = qw_ref[...] * (DIFF_HEAD_DIM ** -0.5 * LOG2E)
    for sec, slot, w_norm in ((SEC_DQ, SLOT_DQ, q_gain), (SEC_DK, SLOT_DK, kw_ref[...])):
        w2 = jnp.concatenate([w_norm] * (MXU_W // HEAD_W), axis=1)
        y = [acc[sec][:, cols] * lax.rsqrt(ss[sec][n] * (1.0 / DIFF_HEAD_DIM) + EPS) * w2
             for n, cols in enumerate(halves)]
        store_heads(slot, jnp.concatenate(y, axis=1))

    for hd in range(N_HEADS):
        vt_ref[hd, 0:HEAD_W, :] = head_cols(acc[SEC_DV], hd).T.astype(BF16)
        vt_ref[hd, HEAD_W:V_ROWS, :] = jnp.ones((V_ROWS - HEAD_W, x.shape[0]), BF16)

    store_heads(SLOT_HQ, acc[SEC_HQ])
    store_heads(SLOT_HI, acc[SEC_HI])

    lbl = lbl_ref[...]
    e = jnp.exp(lbl - jnp.max(lbl, axis=0, keepdims=True))
    lb = e[0:1] / jnp.sum(e, axis=0, keepdims=True)
    f = lb + (1.0 - lb) * _sigmoid(acc[SEC_HK])
    log2_f = jnp.log(f) * LOG2E
    for hd in range(N_HEADS):
        g_ref[hd] = head_cols(log2_f, hd)
    store_heads(SLOT_HK, 1.0 - f)

    store_heads(SLOT_HG, acc[SEC_HG] * _sigmoid(acc[SEC_HG]))


def _in_proj(x2d, ln1_w, w_in, q_norm_w, k_norm_w, lb_logits, later_weights):
    t = x2d.shape[0]
    steps = t // TM_PROJ
    const = lambda *_: (0, 0)
    sliced = [w.reshape(steps, w.shape[0] // steps, w.shape[1]) for w in later_weights]
    slice_specs = [pl.BlockSpec((1,) + w.shape[1:], lambda i: (i, 0, 0)) for w in sliced]
    outs = pl.pallas_call(
        functools.partial(_in_proj_kernel, len(sliced)),
        grid=(steps,),
        in_specs=[
            pl.BlockSpec((TM_PROJ, D_MODEL), lambda i: (i, 0)),
            pl.BlockSpec((1, D_MODEL), const),
            pl.BlockSpec((D_MODEL, IN_COLS), const, pipeline_mode=pl.Buffered(1)),
            pl.BlockSpec((1, HEAD_W), const),
            pl.BlockSpec((1, HEAD_W), const),
            pl.BlockSpec(lb_logits.shape, const),
        ] + slice_specs,
        out_specs=[
            pl.BlockSpec((N_SLOT * N_HEADS, TM_PROJ, HEAD_W), lambda i: (0, i, 0)),
            pl.BlockSpec((N_HEADS, TM_PROJ, HEAD_W), lambda i: (0, i, 0)),
            pl.BlockSpec((N_HEADS, V_ROWS, TM_PROJ), lambda i: (0, 0, i)),
        ] + slice_specs,
        out_shape=[
            jax.ShapeDtypeStruct((N_SLOT * N_HEADS, t, HEAD_W), BF16),
            jax.ShapeDtypeStruct((N_HEADS, t, HEAD_W), F32),
            jax.ShapeDtypeStruct((N_HEADS, V_ROWS, t), BF16),
        ] + [jax.ShapeDtypeStruct(w.shape, BF16) for w in sliced],
        compiler_params=pltpu.CompilerParams(
            dimension_semantics=("parallel",), vmem_limit_bytes=VMEM_LIMIT),
        name="in_proj",
    )(x2d, ln1_w, w_in, q_norm_w, k_norm_w, lb_logits, *sliced)
    rounded = [o.reshape(w.shape) for o, w in zip(outs[3:], later_weights)]
    return outs[0], outs[1], outs[2], rounded


NEG_BIG = -1e30


def _diff_attn_kernel(q_ref, k_ref, vt_ref, lq1_ref, lk1_ref, lq2_ref, lk2_ref, sw_ref,
                      o_ref, qs_ref, s_ref, m_ref, acc_ref):
    qi = pl.program_id(2)
    n_hd = q_ref.shape[0]
    heads = range(n_hd)
    lane = lax.broadcasted_iota(jnp.int32, (HQ, HEAD_W), 1)
    for hd in heads:
        for grp in range(N_SUB):
            q = q_ref[hd, 0, grp * HQ:(grp + 1) * HQ, :]
            zero = jnp.zeros_like(q)
            qs_ref[hd, (2 * grp) * HQ:(2 * grp + 1) * HQ] = jnp.where(lane < DIFF_HEAD_DIM, q, zero)
            qs_ref[hd, (2 * grp + 1) * HQ:(2 * grp + 2) * HQ] = jnp.where(lane >= DIFF_HEAD_DIM, q, zero)

    m_ref[...] = jnp.full(m_ref.shape, NEG_BIG, F32)
    acc_ref[...] = jnp.zeros(acc_ref.shape, F32)

    ck = lax.broadcasted_iota(jnp.int32, (TK, 2 * TQ), 0) // CHUNK
    col = lax.broadcasted_iota(jnp.int32, (TK, 2 * TQ), 1)
    diag_mask = ck <= jnp.where(col >= 2 * HQ, TK // CHUNK, (col % HQ) // CHUNK)

    def kv_rows(j):
        return pl.ds(pl.multiple_of(j * TK, TK), TK)

    def score_step(j, slot, cols=slice(None)):
        rows = kv_rows(j)
        for hd in heads:
            s_ref[slot, hd, :, cols] = _nt_dot(k_ref[hd, 0, rows, :], qs_ref[hd, cols])

    def softmax_step(j, slot, cols=slice(None), mask=None):
        rows = kv_rows(j)
        p_all, alpha_all = [], []
        for hd in heads:
            s = s_ref[slot, hd, :, cols]
            if mask is not None:
                s = jnp.where(mask, s, NEG_BIG)
            m_old = m_ref[hd, :, cols]
            m_new = jnp.maximum(m_old, jnp.max(s, axis=0, keepdims=True))
            m_ref[hd, :, cols] = m_new
            alpha_all.append(jnp.exp2(m_old - m_new))
            p_all.append(jnp.exp2((s - m_new).astype(BF16)))
        pv_all = [_dot(vt_ref[hd, :, rows], p_all[hd]) for hd in heads]
        for hd in heads:
            acc_ref[hd, :, cols] = alpha_all[hd] * acc_ref[hd, :, cols] + pv_all[hd]

    score_step(0, 0)

    def visible_pair(jj, carry):
        j = 2 * jj
        score_step(j + 1, 1)
        softmax_step(j, 0)
        score_step(j + 2, 0)
        softmax_step(j + 1, 1)
        return carry

    lax.fori_loop(0, qi * (N_SUB // 2), visible_pair, 0)
    j0 = N_SUB * qi
    from_group = [slice(2 * HQ * r, 2 * TQ) for r in range(N_SUB)]
    for r in range(N_SUB):
        if r + 1 < N_SUB:
            score_step(j0 + r + 1, (r + 1) % 2, from_group[r + 1])
        softmax_step(j0 + r, r % 2, from_group[r], mask=diag_mask[:, 0:2 * TQ - 2 * HQ * r])

    lam = (jnp.exp(jnp.sum(lq1_ref[...] * lk1_ref[...], axis=-1, keepdims=True))
           - jnp.exp(jnp.sum(lq2_ref[...] * lk2_ref[...], axis=-1, keepdims=True)) + LAM_INIT)
    gain = sw_ref[...] * (1.0 - LAM_INIT)
    for hd in heads:
        o = acc_ref[hd, 0:HEAD_W] / acc_ref[hd, HEAD_W:HEAD_W + 1]
        for grp in range(N_SUB):
            c0 = 2 * grp * HQ
            d = o[:, c0:c0 + HQ] - lam * o[:, c0 + HQ:c0 + 2 * HQ]
            ms = jnp.mean(d * d, axis=0, keepdims=True)
            o_ref[hd, 0, grp * HQ:(grp + 1) * HQ, :] = (
                d * lax.rsqrt(ms + EPS) * gain).T.astype(BF16)


def _diff_attn(proj4, vt, lam_q1, lam_k1, lam_q2, lam_k2, subln_w):
    _, b, s, _ = proj4.shape
    groups = N_HEADS // ATT_HEADS
    const = lambda *_: (0, 0)
    lam_spec = pl.BlockSpec((1, DIFF_HEAD_DIM), const)
    return pl.pallas_call(
        _diff_attn_kernel,
        grid=(b, groups, s // TQ),
        in_specs=[
            pl.BlockSpec((ATT_HEADS, 1, TQ, HEAD_W),
                         lambda bi, gi, qi: (SLOT_DQ * groups + gi, bi, qi, 0)),
            pl.BlockSpec((ATT_HEADS, 1, s, HEAD_W),
                         lambda bi, gi, qi: (SLOT_DK * groups + gi, bi, 0, 0)),
            pl.BlockSpec((ATT_HEADS, V_ROWS, s), lambda bi, gi, qi: (gi, 0, bi)),
            lam_spec, lam_spec, lam_spec, lam_spec,
            pl.BlockSpec((HEAD_W, 1), const),
        ],
        out_specs=pl.BlockSpec((ATT_HEADS, 1, TQ, HEAD_W), lambda bi, gi, qi: (gi, bi, qi, 0)),
        out_shape=jax.ShapeDtypeStruct((N_HEADS, b, s, HEAD_W), BF16),
        scratch_shapes=[
            pltpu.VMEM((ATT_HEADS, 2 * TQ, HEAD_W), BF16),
            pltpu.VMEM((2, ATT_HEADS, TK, 2 * TQ), F32),
            pltpu.VMEM((ATT_HEADS, 1, 2 * TQ), F32),
            pltpu.VMEM((ATT_HEADS, V_ROWS, 2 * TQ), F32),
        ],
        compiler_params=pltpu.CompilerParams(
            dimension_semantics=("parallel", "parallel", "parallel"),
            vmem_limit_bytes=VMEM_LIMIT),
        name="diff_attn",
    )(proj4, proj4, vt, lam_q1, lam_k1, lam_q2, lam_k2, subln_w)


def _hgrn_constants():
    n = HGRN_CHUNK
    t = np.arange(n)[:, None]
    s = np.arange(n)[None, :]
    ltri = (s <= t)
    masks = [(s == t)]
    fine, roles = [], []
    for lvl in range(N_LEVELS):
        hs = n >> (lvl + 1)
        blk = t // (2 * hs)
        mid = blk * 2 * hs + hs - 1
        is_q = (t % (2 * hs)) >= hs
        if lvl >= N_COARSE:
            fine.append(np.where(is_q, (s > mid) & (s <= t), (s > t) & (s <= mid)))
        masks.append((blk == (s // (2 * hs))) & is_q & ((s % (2 * hs)) < hs))
        roles.append(np.broadcast_to(np.where(is_q, 1.0, -1.0), (n, HEAD_W)))
    ltri = ltri.astype(np.float32)
    wfine = np.concatenate(fine, axis=0).astype(np.float32)
    masks = np.stack(masks).astype(np.float32)
    roles = np.stack(roles).astype(np.float32)
    return ltri, wfine, masks, roles


def _hgrn_kernel(q_ref, k_ref, v_ref, gate_ref, g_ref, ltri_ref, wfine_ref, mask_ref, role_ref,
                 nw_ref, o_ref, st_ref):
    n_hd = q_ref.shape[0]
    n_chunks = q_ref.shape[2] // HGRN_CHUNK
    st_ref[...] = jnp.zeros(st_ref.shape, F32)
    ltri = ltri_ref[...]
    wfine = wfine_ref[...]
    nw = nw_ref[...]

    def exponents(g):
        g_hi, g_lo = _split_bf16(g)
        return _dot(ltri, g_hi) + _dot(ltri, g_lo), _dot(wfine, g_hi)

    def decays(b, fine):
        def row_bcast(r, n):
            return jnp.broadcast_to(b[r:r + 1, :], (n, HEAD_W))

        e_b = jnp.exp2(b)
        e_u = jnp.exp2(row_bcast(HGRN_CHUNK - 1, HGRN_CHUNK) - b)
        z = []
        for lvl in range(N_COARSE):
            hs = HGRN_CHUNK >> (lvl + 1)
            b_mid = jnp.concatenate([row_bcast(blk * 2 * hs + hs - 1, 2 * hs)
                                     for blk in range(HGRN_CHUNK // (2 * hs))], axis=0)
            z.append(jnp.exp2((b - b_mid) * role_ref[lvl]))
        fine = jnp.exp2(fine)
        z += [fine[i * HGRN_CHUNK:(i + 1) * HGRN_CHUNK] for i in range(N_LEVELS - N_COARSE)]
        return e_b, e_u, z

    def group(gi, carry):
        base = gi * (HGRN_GROUP * HGRN_CHUNK)
        items = [(hd, pl.ds(pl.multiple_of(base + u * HGRN_CHUNK, HGRN_CHUNK), HGRN_CHUNK))
                 for u in range(HGRN_GROUP) for hd in range(n_hd)]
        q_b = [q_ref[hd, 0, rows, :] for hd, rows in items]
        k_b = [k_ref[hd, 0, rows, :] for hd, rows in items]
        v_b = [v_ref[hd, 0, rows, :] for hd, rows in items]
        expo = [exponents(g_ref[hd, 0, rows, :]) for hd, rows in items]
        dec = [decays(*e) for e in expo]

        def by_role(lvl, q, k):
            if lvl >= N_COARSE:
                return jnp.where(role_ref[lvl] > 0.0, q, k)
            hs = HGRN_CHUNK >> (lvl + 1)
            return jnp.concatenate([(q if part % 2 else k)[part * hs:(part + 1) * hs]
                                    for part in range(HGRN_CHUNK // hs)], axis=0)

        xz, q_dec, k_dec = [], [], []
        for n, (e_b, e_u, z) in enumerate(dec):
            q = q_b[n].astype(F32)
            k = k_b[n].astype(F32)
            xz.append([(by_role(lvl, q, k) * z[lvl]).astype(BF16) for lvl in range(N_LEVELS)])
            q_dec.append((q * e_b).astype(BF16))
            k_dec.append((k * e_u).astype(BF16))

        pair = [[_nt_dot(q_b[n], k_b[n])] + [_nt_dot(x, x) for x in xz[n]]
                for n in range(len(items))]
        kv = [_tn_dot(v_b[n], k_dec[n]) for n in range(len(items))]
        owns = [mask_ref[lvl] > 0.5 for lvl in range(N_LEVELS + 1)]
        att = []
        for p in pair:
            a = jnp.where(owns[0], p[0], 0.0)
            for lvl in range(1, N_LEVELS + 1):
                a = jnp.where(owns[lvl], p[lvl], a)
            att.append(a.astype(BF16))
        o = [_dot(att[n], v_b[n]) for n in range(len(items))]

        st = [st_ref[hd] for hd in range(n_hd)]
        for n, (hd, rows) in enumerate(items):
            o[n] = o[n] + _nt_dot(q_dec[n], st[hd].astype(BF16))
            st[hd] = st[hd] * dec[n][0][HGRN_CHUNK - 1:HGRN_CHUNK, :] + kv[n]
        for hd in range(n_hd):
            st_ref[hd] = st[hd]

        for n, (hd, rows) in enumerate(items):
            ms = jnp.mean(o[n] * o[n], axis=-1, keepdims=True)
            y = o[n] * lax.rsqrt(ms + EPS) * nw * gate_ref[hd, 0, rows, :].astype(F32)
            o_ref[hd, 0, rows, :] = y.astype(BF16)
        return carry

    lax.fori_loop(0, n_chunks // HGRN_GROUP, group, 0)


def _hgrn2(proj4, g4, norm_w):
    _, b, s, _ = proj4.shape
    ltri, wfine, masks, roles = _hgrn_constants()
    heads_spec = lambda sec: pl.BlockSpec((N_HEADS, 1, s, HEAD_W), lambda bi: (sec, bi, 0, 0))
    return pl.pallas_call(
        _hgrn_kernel,
        grid=(b,),
        in_specs=[
            heads_spec(SLOT_HQ), heads_spec(SLOT_HK), heads_spec(SLOT_HI), heads_spec(SLOT_HG),
            heads_spec(0),
            pl.BlockSpec(ltri.shape, lambda *_: (0, 0)),
            pl.BlockSpec(wfine.shape, lambda *_: (0, 0)),
            pl.BlockSpec(masks.shape, lambda *_: (0, 0, 0)),
            pl.BlockSpec(roles.shape, lambda *_: (0, 0, 0)),
            pl.BlockSpec((1, HEAD_W), lambda *_: (0, 0)),
        ],
        out_specs=heads_spec(0),
        out_shape=jax.ShapeDtypeStruct((N_HEADS, b, s, HEAD_W), BF16),
        scratch_shapes=[pltpu.VMEM((N_HEADS, HEAD_W, HEAD_W), F32)],
        compiler_params=pltpu.CompilerParams(
            dimension_semantics=("parallel",), vmem_limit_bytes=VMEM_LIMIT),
        name="hgrn2",
    )(proj4, proj4, proj4, proj4, g4, jnp.asarray(ltri, BF16), jnp.asarray(wfine, BF16),
      jnp.asarray(masks), jnp.asarray(roles), norm_w)


def _ff_chunks():
    chunks, c0 = [], 0
    while c0 < D_FF:
        fc = min(FF_CHUNK, D_FF - c0)
        chunks.append((c0, fc))
        c0 += fc
    return chunks


def _out_ffn_kernel(tiles_per_seq, x_ref, od_ref, oh_ref, wout_ref, ln2_ref, wup_ref, cw_ref,
                    cb_ref, wdn_ref, out_ref, tail_ref, act_ref):
    i = pl.program_id(0)
    mix = jnp.concatenate([od_ref[hd] for hd in range(N_HEADS)]
                          + [oh_ref[hd] for hd in range(N_HEADS)], axis=1)
    x1 = x_ref[...] + _dot(mix, wout_ref[...])
    ms = jnp.mean(x1 * x1, axis=-1, keepdims=True)
    h2 = (x1 * lax.rsqrt(ms + EPS) * ln2_ref[...]).astype(BF16)

    seq_start = (i % tiles_per_seq) == 0
    tm = x1.shape[0]
    for c0, fc in _ff_chunks():
        cols = slice(c0, c0 + fc)
        u = _dot(h2, wup_ref[:, cols])
        v = _dot(h2, wup_ref[:, D_FF + c0:D_FF + c0 + fc])
        tail = jnp.where(seq_start, 0.0, tail_ref[:, cols])
        tail_ref[:, cols] = u[tm - SUBLANES:, :]
        ext = jnp.concatenate([tail, u], axis=0)
        u1 = pltpu.roll(ext, 1, 0)[SUBLANES:]
        u2 = pltpu.roll(ext, 2, 0)[SUBLANES:]
        cw = cw_ref[:, cols]
        c = cb_ref[:, cols] + u2 * cw[0:1] + u1 * cw[1:2] + u * cw[2:3]
        act_ref[:, cols] = (c * _sigmoid(c) * v).astype(BF16)
    out_ref[...] = x1 + _dot(act_ref[...], wdn_ref[...])


def _out_ffn(x2d, o_diff, o_hgrn, w_out, ln2_w, w_up, conv_w, conv_b, w_down, seq_len):
    t = x2d.shape[0]
    const = lambda *_: (0, 0)
    resident = functools.partial(pl.BlockSpec, index_map=const, pipeline_mode=pl.Buffered(1))
    return pl.pallas_call(
        functools.partial(_out_ffn_kernel, seq_len // TM_FFN),
        grid=(t // TM_FFN,),
        in_specs=[
            pl.BlockSpec((TM_FFN, D_MODEL), lambda i: (i, 0)),
            pl.BlockSpec((N_HEADS, TM_FFN, HEAD_W), lambda i: (0, i, 0)),
            pl.BlockSpec((N_HEADS, TM_FFN, HEAD_W), lambda i: (0, i, 0)),
            resident((D_MODEL, D_MODEL)),
            pl.BlockSpec((1, D_MODEL), const),
            resident((D_MODEL, 2 * D_FF)),
            pl.BlockSpec((CONV_WIDTH, D_FF), const),
            pl.BlockSpec((1, D_FF), const),
            resident((D_FF, D_MODEL)),
        ],
        out_specs=pl.BlockSpec((TM_FFN, D_MODEL), lambda i: (i, 0)),
        out_shape=jax.ShapeDtypeStruct((t, D_MODEL), F32),
        scratch_shapes=[
            pltpu.VMEM((SUBLANES, D_FF), F32),
            pltpu.VMEM((TM_FFN, D_FF), BF16),
        ],
        compiler_params=pltpu.CompilerParams(
            dimension_semantics=("arbitrary",), vmem_limit_bytes=VMEM_LIMIT),
        name="out_ffn",
    )(x2d, o_diff, o_hgrn, w_out, ln2_w, w_up, conv_w, conv_b, w_down)


def kernel(x, ln1_w, w_in, q_norm_w, k_norm_w, lam_q1, lam_k1, lam_q2, lam_k2, diff_subln_w,
           hgrn_lb_logits, hgrn_norm_w, w_out, ln2_w, w_up, conv_w, conv_b, w_down):
    b, s, d = x.shape
    depth = ln1_w.shape[0]
    assert depth == 1 and d == D_MODEL and s % TM_FFN == 0 and s % TQ == 0 and N_SUB % 2 == 0
    t = b * s
    x2d = x.reshape(t, d)
    l = 0
    tile2 = lambda w: jnp.concatenate([w, w], axis=-1)[None, :]

    proj, g, vt, (w_out_b, w_up_b, w_down_b) = _in_proj(
        x2d, ln1_w[l][None, :], w_in[l].astype(BF16), tile2(q_norm_w[l]), tile2(k_norm_w[l]),
        hgrn_lb_logits, [w_out[l], w_up[l], w_down[l]])
    proj4 = proj.reshape(N_SLOT * N_HEADS, b, s, HEAD_W)
    g4 = g.reshape(N_HEADS, b, s, HEAD_W)

    o_diff = _diff_attn(proj4, vt, lam_q1[l][None, :], lam_k1[l][None, :], lam_q2[l][None, :],
                        lam_k2[l][None, :], diff_subln_w[l][:, None])
    o_hgrn = _hgrn2(proj4, g4, hgrn_norm_w[l][None, :])

    out = _out_ffn(x2d, o_diff.reshape(N_HEADS, t, HEAD_W), o_hgrn.reshape(N_HEADS, t, HEAD_W),
                   w_out_b, ln2_w[l][None, :], w_up_b, conv_w[l], conv_b[l][None, :], w_down_b, s)
    return out.reshape(b, s, d)
```

```python
import functools
import math

import numpy as np
import jax
import jax.numpy as jnp
from jax import lax
from jax.experimental import pallas as pl
from jax.experimental.pallas import tpu as pltpu

F32 = jnp.float32
BF16 = jnp.bfloat16

D_MODEL = 1024
CHUNK = 64
HEAD_W = 128
DIFF_HEAD_DIM = 64
N_HEADS = 4
SEC_W = N_HEADS * HEAD_W
N_SEC = 7
IN_COLS = N_SEC * SEC_W
D_FF = 2816
CONV_WIDTH = 3
EPS = 1e-6
LAM_INIT = 0.8 - 0.6 * math.exp(-0.3 * 0)
LOG2E = math.log2(math.e)

SEC_DQ, SEC_DK, SEC_DV, SEC_HQ, SEC_HK, SEC_HI, SEC_HG = range(N_SEC)
SLOT_DQ, SLOT_DK, SLOT_HQ, SLOT_HK, SLOT_HI, SLOT_HG = range(6)
N_SLOT = 6
BF16_SUBLANES = 16
V_ROWS = HEAD_W + BF16_SUBLANES

LANES = 128
SUBLANES = 8
MXU_W = 256
TM_PROJ = 1024
ATT_HEADS = 4
TQ = 1024
TK = 256
N_SUB = TQ // TK
HQ = TK
TM_FFN = 512
FF_CHUNK = 768
VMEM_LIMIT = 56 * 1024 * 1024

HGRN_CHUNK = 64
N_LEVELS = 6
N_COARSE = 3
HGRN_GROUP = 4


def _nt_dot(a, b):
    return lax.dot_general(a, b, (((1,), (1,)), ((), ())), preferred_element_type=F32)


def _tn_dot(a, b):
    return lax.dot_general(a, b, (((0,), (0,)), ((), ())), preferred_element_type=F32)


def _dot(a, b):
    return jnp.dot(a, b, preferred_element_type=F32)


def _split_bf16(x):
    hi = x.astype(BF16)
    lo = (x - hi.astype(F32)).astype(BF16)
    return hi, lo


def _sigmoid(x):
    return 1.0 / (1.0 + jnp.exp(-x))


def _in_proj_kernel(n_later, x_ref, ln1_ref, w_ref, qw_ref, kw_ref, lbl_ref, *rest):
    later_f32, (proj_ref, g_ref, vt_ref), later_bf16 = (
        rest[:n_later], rest[n_later:n_later + 3], rest[n_later + 3:])
    for src, dst in zip(later_f32, later_bf16):
        dst[...] = src[...].astype(BF16)

    x = x_ref[...]
    ms = jnp.mean(x * x, axis=-1, keepdims=True)
    h = (x * lax.rsqrt(ms + EPS) * ln1_ref[...]).astype(BF16)

    r = lax.broadcasted_iota(jnp.int32, (MXU_W, MXU_W), 0) // DIFF_HEAD_DIM
    c = lax.broadcasted_iota(jnp.int32, (MXU_W, MXU_W), 1) // DIFF_HEAD_DIM
    grp = jnp.where(r == c, 1.0, 0.0).astype(BF16)

    def head_cols(a, hd):
        return a[:, hd * HEAD_W:(hd + 1) * HEAD_W]

    def store_heads(slot, a):
        for hd in range(N_HEADS):
            proj_ref[slot * N_HEADS + hd] = head_cols(a, hd).astype(BF16)

    def project(j):
        return _dot(h, w_ref[:, j * SEC_W:(j + 1) * SEC_W])

    acc = [None] * N_SEC
    halves = [slice(c0, c0 + MXU_W) for c0 in range(0, SEC_W, MXU_W)]
    ss = {}
    for sec in (SEC_DQ, SEC_DK):
        acc[sec] = project(sec)
        ss[sec] = [_dot((acc[sec][:, cols] * acc[sec][:, cols]).astype(BF16), grp)
                   for cols in halves]
    for sec in (SEC_HK, SEC_HG, SEC_DV, SEC_HQ, SEC_HI):
        acc[sec] = project(sec)

    q_gain = qw_ref[...] * (DIFF_HEAD_DIM ** -0.5 * LOG2E)
    for sec, slot, w_norm in ((SEC_DQ, SLOT_DQ, q_gain), (SEC_DK, SLOT_DK, kw_ref[...])):
        w2 = jnp.concatenate([w_norm] * (MXU_W // HEAD_W), axis=1)
        y = [acc[sec][:, cols] * lax.rsqrt(ss[sec][n] * (1.0 / DIFF_HEAD_DIM) + EPS) * w2
             for n, cols in enumerate(halves)]
        store_heads(slot, jnp.concatenate(y, axis=1))

    for hd in range(N_HEADS):
        vt_ref[hd, 0:HEAD_W, :] = head_cols(acc[SEC_DV], hd).T.astype(BF16)
        vt_ref[hd, HEAD_W:V_ROWS, :] = jnp.ones((V_ROWS - HEAD_W, x.shape[0]), BF16)

    store_heads(SLOT_HQ, acc[SEC_HQ])
    store_heads(SLOT_HI, acc[SEC_HI])

    lbl = lbl_ref[...]
    e = jnp.exp(lbl - jnp.max(lbl, axis=0, keepdims=True))
    lb = e[0:1] / jnp.sum(e, axis=0, keepdims=True)
    f = lb + (1.0 - lb) * _sigmoid(acc[SEC_HK])
    log2_f = jnp.log(f) * LOG2E
    for hd in range(N_HEADS):
        g_ref[hd] = head_cols(log2_f, hd)
    store_heads(SLOT_HK, 1.0 - f)

    store_heads(SLOT_HG, acc[SEC_HG] * _sigmoid(acc[SEC_HG]))


def _in_proj(x2d, ln1_w, w_in, q_norm_w, k_norm_w, lb_logits, later_weights):
    t = x2d.shape[0]
    steps = t // TM_PROJ
    const = lambda *_: (0, 0)
    sliced = [w.reshape(steps, w.shape[0] // steps, w.shape[1]) for w in later_weights]
    slice_specs = [pl.BlockSpec((1,) + w.shape[1:], lambda i: (i, 0, 0)) for w in sliced]
    outs = pl.pallas_call(
        functools.partial(_in_proj_kernel, len(sliced)),
        grid=(steps,),
        in_specs=[
            pl.BlockSpec((TM_PROJ, D_MODEL), lambda i: (i, 0)),
            pl.BlockSpec((1, D_MODEL), const),
            pl.BlockSpec((D_MODEL, IN_COLS), const, pipeline_mode=pl.Buffered(1)),
            pl.BlockSpec((1, HEAD_W), const),
            pl.BlockSpec((1, HEAD_W), const),
            pl.BlockSpec(lb_logits.shape, const),
        ] + slice_specs,
        out_specs=[
            pl.BlockSpec((N_SLOT * N_HEADS, TM_PROJ, HEAD_W), lambda i: (0, i, 0)),
            pl.BlockSpec((N_HEADS, TM_PROJ, HEAD_W), lambda i: (0, i, 0)),
            pl.BlockSpec((N_HEADS, V_ROWS, TM_PROJ), lambda i: (0, 0, i)),
        ] + slice_specs,
        out_shape=[
            jax.ShapeDtypeStruct((N_SLOT * N_HEADS, t, HEAD_W), BF16),
            jax.ShapeDtypeStruct((N_HEADS, t, HEAD_W), F32),
            jax.ShapeDtypeStruct((N_HEADS, V_ROWS, t), BF16),
        ] + [jax.ShapeDtypeStruct(w.shape, BF16) for w in sliced],
        compiler_params=pltpu.CompilerParams(
            dimension_semantics=("parallel",), vmem_limit_bytes=VMEM_LIMIT),
        name="in_proj",
    )(x2d, ln1_w, w_in, q_norm_w, k_norm_w, lb_logits, *sliced)
    rounded = [o.reshape(w.shape) for o, w in zip(outs[3:], later_weights)]
    return outs[0], outs[1], outs[2], rounded


NEG_BIG = -1e30


def _diff_attn_kernel(q_ref, k_ref, vt_ref, lq1_ref, lk1_ref, lq2_ref, lk2_ref, sw_ref,
                      o_ref, qs_ref, s_ref, m_ref, acc_ref):
    qi = pl.program_id(2)
    n_hd = q_ref.shape[0]
    heads = range(n_hd)
    lane = lax.broadcasted_iota(jnp.int32, (HQ, HEAD_W), 1)
    for hd in heads:
        for grp in range(N_SUB):
            q = q_ref[hd, 0, grp * HQ:(grp + 1) * HQ, :]
            zero = jnp.zeros_like(q)
            qs_ref[hd, (2 * grp) * HQ:(2 * grp + 1) * HQ] = jnp.where(lane < DIFF_HEAD_DIM, q, zero)
            qs_ref[hd, (2 * grp + 1) * HQ:(2 * grp + 2) * HQ] = jnp.where(lane >= DIFF_HEAD_DIM, q, zero)

    m_ref[...] = jnp.full(m_ref.shape, NEG_BIG, F32)
    acc_ref[...] = jnp.zeros(acc_ref.shape, F32)

    ck = lax.broadcasted_iota(jnp.int32, (TK, 2 * TQ), 0) // CHUNK
    col = lax.broadcasted_iota(jnp.int32, (TK, 2 * TQ), 1)
    diag_mask = ck <= jnp.where(col >= 2 * HQ, TK // CHUNK, (col % HQ) // CHUNK)

    def kv_rows(j):
        return slice(j * TK, (j + 1) * TK)

    def score_step(j, slot, cols=slice(None)):
        rows = kv_rows(j)
        for hd in heads:
            s_ref[slot, hd, :, cols] = _nt_dot(k_ref[hd, 0, rows, :], qs_ref[hd, cols])

    def softmax_step(j, slot, cols=slice(None), mask=None):
        rows = kv_rows(j)
        p_all, alpha_all = [], []
        for hd in heads:
            s = s_ref[slot, hd, :, cols]
            if mask is not None:
                s = jnp.where(mask, s, NEG_BIG)
            m_old = m_ref[hd, :, cols]
            m_new = jnp.maximum(m_old, jnp.max(s, axis=0, keepdims=True))
            m_ref[hd, :, cols] = m_new
            alpha_all.append(jnp.exp2(m_old - m_new))
            p_all.append(jnp.exp2((s - m_new).astype(BF16)))
        pv_all = [_dot(vt_ref[hd, :, rows], p_all[hd]) for hd in heads]
        for hd in heads:
            acc_ref[hd, :, cols] = alpha_all[hd] * acc_ref[hd, :, cols] + pv_all[hd]

    def run(q):
        blocks = [(j, slice(None), None) for j in range(N_SUB * q)]
        blocks += [(N_SUB * q + r, slice(2 * HQ * r, 2 * TQ), diag_mask[:, 0:2 * TQ - 2 * HQ * r])
                   for r in range(N_SUB)]
        score_step(blocks[0][0], 0, blocks[0][1])
        for n, (j, cols, mask) in enumerate(blocks):
            if n + 1 < len(blocks):
                score_step(blocks[n + 1][0], (n + 1) % 2, blocks[n + 1][1])
            softmax_step(j, n % 2, cols, mask)

    for q in range(k_ref.shape[2] // TQ):
        pl.when(qi == q)(functools.partial(run, q))

    lam = (jnp.exp(jnp.sum(lq1_ref[...] * lk1_ref[...], axis=-1, keepdims=True))
           - jnp.exp(jnp.sum(lq2_ref[...] * lk2_ref[...], axis=-1, keepdims=True)) + LAM_INIT)
    gain = sw_ref[...] * (1.0 - LAM_INIT)
    for hd in heads:
        o = acc_ref[hd, 0:HEAD_W] / acc_ref[hd, HEAD_W:HEAD_W + 1]
        for grp in range(N_SUB):
            c0 = 2 * grp * HQ
            d = o[:, c0:c0 + HQ] - lam * o[:, c0 + HQ:c0 + 2 * HQ]
            ms = jnp.mean(d * d, axis=0, keepdims=True)
            o_ref[hd, 0, grp * HQ:(grp + 1) * HQ, :] = (
                d * lax.rsqrt(ms + EPS) * gain).T.astype(BF16)


def _diff_attn(proj4, vt, lam_q1, lam_k1, lam_q2, lam_k2, subln_w):
    _, b, s, _ = proj4.shape
    groups = N_HEADS // ATT_HEADS
    const = lambda *_: (0, 0)
    lam_spec = pl.BlockSpec((1, DIFF_HEAD_DIM), const)
    return pl.pallas_call(
        _diff_attn_kernel,
        grid=(b, groups, s // TQ),
        in_specs=[
            pl.BlockSpec((ATT_HEADS, 1, TQ, HEAD_W),
                         lambda bi, gi, qi: (SLOT_DQ * groups + gi, bi, qi, 0)),
            pl.BlockSpec((ATT_HEADS, 1, s, HEAD_W),
                         lambda bi, gi, qi: (SLOT_DK * groups + gi, bi, 0, 0)),
            pl.BlockSpec((ATT_HEADS, V_ROWS, s), lambda bi, gi, qi: (gi, 0, bi)),
            lam_spec, lam_spec, lam_spec, lam_spec,
            pl.BlockSpec((HEAD_W, 1), const),
        ],
        out_specs=pl.BlockSpec((ATT_HEADS, 1, TQ, HEAD_W), lambda bi, gi, qi: (gi, bi, qi, 0)),
        out_shape=jax.ShapeDtypeStruct((N_HEADS, b, s, HEAD_W), BF16),
        scratch_shapes=[
            pltpu.VMEM((ATT_HEADS, 2 * TQ, HEAD_W), BF16),
            pltpu.VMEM((2, ATT_HEADS, TK, 2 * TQ), F32),
            pltpu.VMEM((ATT_HEADS, 1, 2 * TQ), F32),
            pltpu.VMEM((ATT_HEADS, V_ROWS, 2 * TQ), F32),
        ],
        compiler_params=pltpu.CompilerParams(
            dimension_semantics=("parallel", "parallel", "parallel"),
            vmem_limit_bytes=VMEM_LIMIT),
        name="diff_attn",
    )(proj4, proj4, vt, lam_q1, lam_k1, lam_q2, lam_k2, subln_w)


def _hgrn_constants():
    n = HGRN_CHUNK
    t = np.arange(n)[:, None]
    s = np.arange(n)[None, :]
    ltri = (s <= t)
    masks = [(s == t)]
    fine, roles = [], []
    for lvl in range(N_LEVELS):
        hs = n >> (lvl + 1)
        blk = t // (2 * hs)
        mid = blk * 2 * hs + hs - 1
        is_q = (t % (2 * hs)) >= hs
        if lvl >= N_COARSE:
            fine.append(np.where(is_q, (s > mid) & (s <= t), (s > t) & (s <= mid)))
        masks.append((blk == (s // (2 * hs))) & is_q & ((s % (2 * hs)) < hs))
        roles.append(np.broadcast_to(np.where(is_q, 1.0, -1.0), (n, HEAD_W)))
    ltri = ltri.astype(np.float32)
    wfine = np.concatenate(fine, axis=0).astype(np.float32)
    masks = np.stack(masks).astype(np.float32)
    roles = np.stack(roles).astype(np.float32)
    return ltri, wfine, masks, roles


def _hgrn_kernel(q_ref, k_ref, v_ref, gate_ref, g_ref, ltri_ref, wfine_ref, mask_ref, role_ref,
                 nw_ref, o_ref, st_ref):
    n_hd = q_ref.shape[0]
    n_chunks = q_ref.shape[2] // HGRN_CHUNK
    st_ref[...] = jnp.zeros(st_ref.shape, F32)
    ltri = ltri_ref[...]
    wfine = wfine_ref[...]
    nw = nw_ref[...]

    def exponents(g):
        g_hi, g_lo = _split_bf16(g)
        return _dot(ltri, g_hi) + _dot(ltri, g_lo), _dot(wfine, g_hi)

    def decays(b, fine):
        def row_bcast(r, n):
            return jnp.broadcast_to(b[r:r + 1, :], (n, HEAD_W))

        e_b = jnp.exp2(b)
        e_u = jnp.exp2(row_bcast(HGRN_CHUNK - 1, HGRN_CHUNK) - b)
        z = []
        for lvl in range(N_COARSE):
            hs = HGRN_CHUNK >> (lvl + 1)
            b_mid = jnp.concatenate([row_bcast(blk * 2 * hs + hs - 1, 2 * hs)
                                     for blk in range(HGRN_CHUNK // (2 * hs))], axis=0)
            z.append(jnp.exp2((b - b_mid) * role_ref[lvl]))
        fine = jnp.exp2(fine)
        z += [fine[i * HGRN_CHUNK:(i + 1) * HGRN_CHUNK] for i in range(N_LEVELS - N_COARSE)]
        return e_b, e_u, z

    def group(gi, carry):
        base = gi * (HGRN_GROUP * HGRN_CHUNK)
        items = [(hd, pl.ds(pl.multiple_of(base + u * HGRN_CHUNK, HGRN_CHUNK), HGRN_CHUNK))
                 for u in range(HGRN_GROUP) for hd in range(n_hd)]
        q_b = [q_ref[hd, 0, rows, :] for hd, rows in items]
        k_b = [k_ref[hd, 0, rows, :] for hd, rows in items]
        v_b = [v_ref[hd, 0, rows, :] for hd, rows in items]
        expo = [exponents(g_ref[hd, 0, rows, :]) for hd, rows in items]
        dec = [decays(*e) for e in expo]

        def by_role(lvl, q, k):
            if lvl >= N_COARSE:
                return jnp.where(role_ref[lvl] > 0.0, q, k)
            hs = HGRN_CHUNK >> (lvl + 1)
            return jnp.concatenate([(q if part % 2 else k)[part * hs:(part + 1) * hs]
                                    for part in range(HGRN_CHUNK // hs)], axis=0)

        xz, q_dec, k_dec = [], [], []
        for n, (e_b, e_u, z) in enumerate(dec):
            q = q_b[n].astype(F32)
            k = k_b[n].astype(F32)
            xz.append([(by_role(lvl, q, k) * z[lvl]).astype(BF16) for lvl in range(N_LEVELS)])
            q_dec.append((q * e_b).astype(BF16))
            k_dec.append((k * e_u).astype(BF16))

        pair = [[_nt_dot(q_b[n], k_b[n])] + [_nt_dot(x, x) for x in xz[n]]
                for n in range(len(items))]
        kv = [_tn_dot(v_b[n], k_dec[n]) for n in range(len(items))]
        owns = [mask_ref[lvl] > 0.5 for lvl in range(N_LEVELS + 1)]
        att = []
        for p in pair:
            a = jnp.where(owns[0], p[0], 0.0)
            for lvl in range(1, N_LEVELS + 1):
                a = jnp.where(owns[lvl], p[lvl], a)
            att.append(a.astype(BF16))
        o = [_dot(att[n], v_b[n]) for n in range(len(items))]

        st = [st_ref[hd] for hd in range(n_hd)]
        for n, (hd, rows) in enumerate(items):
            o[n] = o[n] + _nt_dot(q_dec[n], st[hd].astype(BF16))
            st[hd] = st[hd] * dec[n][0][HGRN_CHUNK - 1:HGRN_CHUNK, :] + kv[n]
        for hd in range(n_hd):
            st_ref[hd] = st[hd]

        for n, (hd, rows) in enumerate(items):
            ms = jnp.mean(o[n] * o[n], axis=-1, keepdims=True)
            y = o[n] * lax.rsqrt(ms + EPS) * nw * gate_ref[hd, 0, rows, :].astype(F32)
            o_ref[hd, 0, rows, :] = y.astype(BF16)
        return carry

    lax.fori_loop(0, n_chunks // HGRN_GROUP, group, 0)


def _hgrn2(proj4, g4, norm_w):
    _, b, s, _ = proj4.shape
    ltri, wfine, masks, roles = _hgrn_constants()
    heads_spec = lambda sec: pl.BlockSpec((N_HEADS, 1, s, HEAD_W), lambda bi: (sec, bi, 0, 0))
    return pl.pallas_call(
        _hgrn_kernel,
        grid=(b,),
        in_specs=[
            heads_spec(SLOT_HQ), heads_spec(SLOT_HK), heads_spec(SLOT_HI), heads_spec(SLOT_HG),
            heads_spec(0),
            pl.BlockSpec(ltri.shape, lambda *_: (0, 0)),
            pl.BlockSpec(wfine.shape, lambda *_: (0, 0)),
            pl.BlockSpec(masks.shape, lambda *_: (0, 0, 0)),
            pl.BlockSpec(roles.shape, lambda *_: (0, 0, 0)),
            pl.BlockSpec((1, HEAD_W), lambda *_: (0, 0)),
        ],
        out_specs=heads_spec(0),
        out_shape=jax.ShapeDtypeStruct((N_HEADS, b, s, HEAD_W), BF16),
        scratch_shapes=[pltpu.VMEM((N_HEADS, HEAD_W, HEAD_W), F32)],
        compiler_params=pltpu.CompilerParams(
            dimension_semantics=("parallel",), vmem_limit_bytes=VMEM_LIMIT),
        name="hgrn2",
    )(proj4, proj4, proj4, proj4, g4, jnp.asarray(ltri, BF16), jnp.asarray(wfine, BF16),
      jnp.asarray(masks), jnp.asarray(roles), norm_w)


def _ff_chunks():
    chunks, c0 = [], 0
    while c0 < D_FF:
        fc = min(FF_CHUNK, D_FF - c0)
        chunks.append((c0, fc))
        c0 += fc
    return chunks


def _out_ffn_kernel(tiles_per_seq, x_ref, od_ref, oh_ref, wout_ref, ln2_ref, wup_ref, cw_ref,
                    cb_ref, wdn_ref, out_ref, tail_ref, act_ref):
    i = pl.program_id(0)
    mix = jnp.concatenate([od_ref[hd] for hd in range(N_HEADS)]
                          + [oh_ref[hd] for hd in range(N_HEADS)], axis=1)
    x1 = x_ref[...] + _dot(mix, wout_ref[...])
    ms = jnp.mean(x1 * x1, axis=-1, keepdims=True)
    h2 = (x1 * lax.rsqrt(ms + EPS) * ln2_ref[...]).astype(BF16)

    seq_start = (i % tiles_per_seq) == 0
    tm = x1.shape[0]
    for c0, fc in _ff_chunks():
        cols = slice(c0, c0 + fc)
        u = _dot(h2, wup_ref[:, cols])
        v = _dot(h2, wup_ref[:, D_FF + c0:D_FF + c0 + fc])
        tail = jnp.where(seq_start, 0.0, tail_ref[:, cols])
        tail_ref[:, cols] = u[tm - SUBLANES:, :]
        ext = jnp.concatenate([tail, u], axis=0)
        u1 = pltpu.roll(ext, 1, 0)[SUBLANES:]
        u2 = pltpu.roll(ext, 2, 0)[SUBLANES:]
        cw = cw_ref[:, cols]
        c = cb_ref[:, cols] + u2 * cw[0:1] + u1 * cw[1:2] + u * cw[2:3]
        act_ref[:, cols] = (c * _sigmoid(c) * v).astype(BF16)
    out_ref[...] = x1 + _dot(act_ref[...], wdn_ref[...])


def _out_ffn(x2d, o_diff, o_hgrn, w_out, ln2_w, w_up, conv_w, conv_b, w_down, seq_len):
    t = x2d.shape[0]
    const = lambda *_: (0, 0)
    resident = functools.partial(pl.BlockSpec, index_map=const, pipeline_mode=pl.Buffered(1))
    return pl.pallas_call(
        functools.partial(_out_ffn_kernel, seq_len // TM_FFN),
        grid=(t // TM_FFN,),
        in_specs=[
            pl.BlockSpec((TM_FFN, D_MODEL), lambda i: (i, 0)),
            pl.BlockSpec((N_HEADS, TM_FFN, HEAD_W), lambda i: (0, i, 0)),
            pl.BlockSpec((N_HEADS, TM_FFN, HEAD_W), lambda i: (0, i, 0)),
            resident((D_MODEL, D_MODEL)),
            pl.BlockSpec((1, D_MODEL), const),
            resident((D_MODEL, 2 * D_FF)),
            pl.BlockSpec((CONV_WIDTH, D_FF), const),
            pl.BlockSpec((1, D_FF), const),
            resident((D_FF, D_MODEL)),
        ],
        out_specs=pl.BlockSpec((TM_FFN, D_MODEL), lambda i: (i, 0)),
        out_shape=jax.ShapeDtypeStruct((t, D_MODEL), F32),
        scratch_shapes=[
            pltpu.VMEM((SUBLANES, D_FF), F32),
            pltpu.VMEM((TM_FFN, D_FF), BF16),
        ],
        compiler_params=pltpu.CompilerParams(
            dimension_semantics=("arbitrary",), vmem_limit_bytes=VMEM_LIMIT),
        name="out_ffn",
    )(x2d, o_diff, o_hgrn, w_out, ln2_w, w_up, conv_w, conv_b, w_down)


def kernel(x, ln1_w, w_in, q_norm_w, k_norm_w, lam_q1, lam_k1, lam_q2, lam_k2, diff_subln_w,
           hgrn_lb_logits, hgrn_norm_w, w_out, ln2_w, w_up, conv_w, conv_b, w_down):
    b, s, d = x.shape
    depth = ln1_w.shape[0]
    assert depth == 1 and d == D_MODEL and s % TM_FFN == 0 and s % TQ == 0 and N_SUB % 2 == 0
    t = b * s
    x2d = x.reshape(t, d)
    l = 0
    tile2 = lambda w: jnp.concatenate([w, w], axis=-1)[None, :]

    proj, g, vt, (w_out_b, w_up_b, w_down_b) = _in_proj(
        x2d, ln1_w[l][None, :], w_in[l].astype(BF16), tile2(q_norm_w[l]), tile2(k_norm_w[l]),
        hgrn_lb_logits, [w_out[l], w_up[l], w_down[l]])
    proj4 = proj.reshape(N_SLOT * N_HEADS, b, s, HEAD_W)
    g4 = g.reshape(N_HEADS, b, s, HEAD_W)

    o_diff = _diff_attn(proj4, vt, lam_q1[l][None, :], lam_k1[l][None, :], lam_q2[l][None, :],
                        lam_k2[l][None, :], diff_subln_w[l][:, None])
    o_hgrn = _hgrn2(proj4, g4, hgrn_norm_w[l][None, :])

    out = _out_ffn(x2d, o_diff.reshape(N_HEADS, t, HEAD_W), o_hgrn.reshape(N_HEADS, t, HEAD_W),
                   w_out_b, ln2_w[l][None, :], w_up_b, conv_w[l], conv_b[l][None, :], w_down_b, s)
    return out.reshape(b, s, d)
```

```python
import functools
import math

import numpy as np
import jax
import jax.numpy as jnp
from jax import lax
from jax.experimental import pallas as pl
from jax.experimental.pallas import tpu as pltpu

F32 = jnp.float32
BF16 = jnp.bfloat16

D_MODEL = 1024
CHUNK = 64
HEAD_W = 128
DIFF_HEAD_DIM = 64
N_HEADS = 4
SEC_W = N_HEADS * HEAD_W
N_SEC = 7
IN_COLS = N_SEC * SEC_W
D_FF = 2816
CONV_WIDTH = 3
EPS = 1e-6
LAM_INIT = 0.8 - 0.6 * math.exp(-0.3 * 0)
LOG2E = math.log2(math.e)

SEC_DQ, SEC_DK, SEC_DV, SEC_HQ, SEC_HK, SEC_HI, SEC_HG = range(N_SEC)
SLOT_DQ, SLOT_DK, SLOT_HQ, SLOT_HK, SLOT_HI, SLOT_HG = range(6)
N_SLOT = 6
BF16_SUBLANES = 16
V_ROWS = HEAD_W + BF16_SUBLANES

LANES = 128
SUBLANES = 8
MXU_W = 256
TM_PROJ = 1024
ATT_HEADS = 4
TQ = 1024
TK = 256
N_SUB = TQ // TK
HQ = TK
TM_FFN = 512
FF_CHUNK = 768
VMEM_LIMIT = 56 * 1024 * 1024

HGRN_CHUNK = 64
N_LEVELS = 6
N_COARSE = 3
HGRN_GROUP = 4


def _nt_dot(a, b):
    return lax.dot_general(a, b, (((1,), (1,)), ((), ())), preferred_element_type=F32)


def _tn_dot(a, b):
    return lax.dot_general(a, b, (((0,), (0,)), ((), ())), preferred_element_type=F32)


def _dot(a, b):
    return jnp.dot(a, b, preferred_element_type=F32)


def _split_bf16(x):
    hi = x.astype(BF16)
    lo = (x - hi.astype(F32)).astype(BF16)
    return hi, lo


def _sigmoid(x):
    return 1.0 / (1.0 + jnp.exp(-x))


def _in_proj_kernel(n_later, x_ref, ln1_ref, w_ref, qw_ref, kw_ref, lbl_ref, *rest):
    later_f32, (proj_ref, g_ref, vt_ref), later_bf16 = (
        rest[:n_later], rest[n_later:n_later + 3], rest[n_later + 3:])
    for src, dst in zip(later_f32, later_bf16):
        dst[...] = src[...].astype(BF16)

    x = x_ref[...]
    ms = jnp.mean(x * x, axis=-1, keepdims=True)
    h = (x * lax.rsqrt(ms + EPS) * ln1_ref[...]).astype(BF16)

    r = lax.broadcasted_iota(jnp.int32, (MXU_W, MXU_W), 0) // DIFF_HEAD_DIM
    c = lax.broadcasted_iota(jnp.int32, (MXU_W, MXU_W), 1) // DIFF_HEAD_DIM
    grp = jnp.where(r == c, 1.0, 0.0).astype(BF16)

    def head_cols(a, hd):
        return a[:, hd * HEAD_W:(hd + 1) * HEAD_W]

    def store_heads(slot, a):
        for hd in range(N_HEADS):
            proj_ref[slot * N_HEADS + hd] = head_cols(a, hd).astype(BF16)

    def project(j):
        return _dot(h, w_ref[:, j * SEC_W:(j + 1) * SEC_W])

    acc = [None] * N_SEC
    halves = [slice(c0, c0 + MXU_W) for c0 in range(0, SEC_W, MXU_W)]
    ss = {}
    for sec in (SEC_DQ, SEC_DK):
        acc[sec] = project(sec)
        ss[sec] = [_dot((acc[sec][:, cols] * acc[sec][:, cols]).astype(BF16), grp)
                   for cols in halves]
    for sec in (SEC_HK, SEC_HG, SEC_DV, SEC_HQ, SEC_HI):
        acc[sec] = project(sec)

    q_gain = qw_ref[...] * (DIFF_HEAD_DIM ** -0.5 * LOG2E)
    for sec, slot, w_norm in ((SEC_DQ, SLOT_DQ, q_gain), (SEC_DK, SLOT_DK, kw_ref[...])):
        w2 = jnp.concatenate([w_norm] * (MXU_W // HEAD_W), axis=1)
        y = [acc[sec][:, cols] * lax.rsqrt(ss[sec][n] * (1.0 / DIFF_HEAD_DIM) + EPS) * w2
             for n, cols in enumerate(halves)]
        store_heads(slot, jnp.concatenate(y, axis=1))

    for hd in range(N_HEADS):
        vt_ref[hd, 0:HEAD_W, :] = head_cols(acc[SEC_DV], hd).T.astype(BF16)
        vt_ref[hd, HEAD_W:V_ROWS, :] = jnp.ones((V_ROWS - HEAD_W, x.shape[0]), BF16)

    store_heads(SLOT_HQ, acc[SEC_HQ])
    store_heads(SLOT_HI, acc[SEC_HI])

    lbl = lbl_ref[...]
    e = jnp.exp(lbl - jnp.max(lbl, axis=0, keepdims=True))
    lb = e[0:1] / jnp.sum(e, axis=0, keepdims=True)
    f = lb + (1.0 - lb) * _sigmoid(acc[SEC_HK])
    log2_f = jnp.log(f) * LOG2E
    for hd in range(N_HEADS):
        g_ref[hd] = head_cols(log2_f, hd)
    store_heads(SLOT_HK, 1.0 - f)

    store_heads(SLOT_HG, acc[SEC_HG] * _sigmoid(acc[SEC_HG]))


def _in_proj(x2d, ln1_w, w_in, q_norm_w, k_norm_w, lb_logits, later_weights):
    t = x2d.shape[0]
    steps = t // TM_PROJ
    const = lambda *_: (0, 0)
    sliced = [w.reshape(steps, w.shape[0] // steps, w.shape[1]) for w in later_weights]
    slice_specs = [pl.BlockSpec((1,) + w.shape[1:], lambda i: (i, 0, 0)) for w in sliced]
    outs = pl.pallas_call(
        functools.partial(_in_proj_kernel, len(sliced)),
        grid=(steps,),
        in_specs=[
            pl.BlockSpec((TM_PROJ, D_MODEL), lambda i: (i, 0)),
            pl.BlockSpec((1, D_MODEL), const),
            pl.BlockSpec((D_MODEL, IN_COLS), const, pipeline_mode=pl.Buffered(1)),
            pl.BlockSpec((1, HEAD_W), const),
            pl.BlockSpec((1, HEAD_W), const),
            pl.BlockSpec(lb_logits.shape, const),
        ] + slice_specs,
        out_specs=[
            pl.BlockSpec((N_SLOT * N_HEADS, TM_PROJ, HEAD_W), lambda i: (0, i, 0)),
            pl.BlockSpec((N_HEADS, TM_PROJ, HEAD_W), lambda i: (0, i, 0)),
            pl.BlockSpec((N_HEADS, V_ROWS, TM_PROJ), lambda i: (0, 0, i)),
        ] + slice_specs,
        out_shape=[
            jax.ShapeDtypeStruct((N_SLOT * N_HEADS, t, HEAD_W), BF16),
            jax.ShapeDtypeStruct((N_HEADS, t, HEAD_W), F32),
            jax.ShapeDtypeStruct((N_HEADS, V_ROWS, t), BF16),
        ] + [jax.ShapeDtypeStruct(w.shape, BF16) for w in sliced],
        compiler_params=pltpu.CompilerParams(
            dimension_semantics=("parallel",), vmem_limit_bytes=VMEM_LIMIT),
        name="in_proj",
    )(x2d, ln1_w, w_in, q_norm_w, k_norm_w, lb_logits, *sliced)
    rounded = [o.reshape(w.shape) for o, w in zip(outs[3:], later_weights)]
    return outs[0], outs[1], outs[2], rounded


NEG_BIG = -1e30


def _diff_attn_kernel(q_ref, k_ref, vt_ref, lq1_ref, lk1_ref, lq2_ref, lk2_ref, sw_ref,
                      o_ref, qs_ref, s_ref, m_ref, acc_ref):
    qi = pl.program_id(2)
    n_hd = q_ref.shape[0]
    heads = range(n_hd)
    lane = lax.broadcasted_iota(jnp.int32, (HQ, HEAD_W), 1)
    for hd in heads:
        for grp in range(N_SUB):
            q = q_ref[hd, 0, grp * HQ:(grp + 1) * HQ, :]
            zero = jnp.zeros_like(q)
            qs_ref[hd, (2 * grp) * HQ:(2 * grp + 1) * HQ] = jnp.where(lane < DIFF_HEAD_DIM, q, zero)
            qs_ref[hd, (2 * grp + 1) * HQ:(2 * grp + 2) * HQ] = jnp.where(lane >= DIFF_HEAD_DIM, q, zero)

    m_ref[...] = jnp.full(m_ref.shape, NEG_BIG, F32)
    acc_ref[...] = jnp.zeros(acc_ref.shape, F32)

    ck = lax.broadcasted_iota(jnp.int32, (TK, 2 * TQ), 0) // CHUNK
    col = lax.broadcasted_iota(jnp.int32, (TK, 2 * TQ), 1)
    diag_mask = ck <= jnp.where(col >= 2 * HQ, TK // CHUNK, (col % HQ) // CHUNK)

    def kv_rows(j):
        return slice(j * TK, (j + 1) * TK)

    def score_step(j, slot, cols=slice(None)):
        rows = kv_rows(j)
        for hd in heads:
            s_ref[slot, hd, :, cols] = _nt_dot(k_ref[hd, 0, rows, :], qs_ref[hd, cols])

    def softmax_step(j, slot, cols=slice(None), mask=None):
        rows = kv_rows(j)
        p_all, alpha_all = [], []
        for hd in heads:
            s = s_ref[slot, hd, :, cols]
            if mask is not None:
                s = jnp.where(mask, s, NEG_BIG)
            m_old = m_ref[hd, :, cols]
            m_new = jnp.maximum(m_old, jnp.max(s, axis=0, keepdims=True))
            m_ref[hd, :, cols] = m_new
            alpha_all.append(jnp.exp2(m_old - m_new))
            p_all.append(jnp.exp2((s - m_new).astype(BF16)))
        pv_all = [_dot(vt_ref[hd, :, rows], p_all[hd]) for hd in heads]
        for hd in heads:
            acc_ref[hd, :, cols] = alpha_all[hd] * acc_ref[hd, :, cols] + pv_all[hd]

    def run(q):
        blocks = [(j, slice(None), None) for j in range(N_SUB * q)]
        blocks += [(N_SUB * q + r, slice(2 * HQ * r, 2 * TQ), diag_mask[:, 0:2 * TQ - 2 * HQ * r])
                   for r in range(N_SUB)]
        score_step(blocks[0][0], 0, blocks[0][1])
        for n, (j, cols, mask) in enumerate(blocks):
            if n + 1 < len(blocks):
                score_step(blocks[n + 1][0], (n + 1) % 2, blocks[n + 1][1])
            softmax_step(j, n % 2, cols, mask)

    for q in range(k_ref.shape[2] // TQ):
        pl.when(qi == q)(functools.partial(run, q))

    lam = (jnp.exp(jnp.sum(lq1_ref[...] * lk1_ref[...], axis=-1, keepdims=True))
           - jnp.exp(jnp.sum(lq2_ref[...] * lk2_ref[...], axis=-1, keepdims=True)) + LAM_INIT)
    gain = sw_ref[...] * (1.0 - LAM_INIT)
    for hd in heads:
        o = acc_ref[hd, 0:HEAD_W] / acc_ref[hd, HEAD_W:HEAD_W + 1]
        for grp in range(N_SUB):
            c0 = 2 * grp * HQ
            d = o[:, c0:c0 + HQ] - lam * o[:, c0 + HQ:c0 + 2 * HQ]
            ms = jnp.mean(d * d, axis=0, keepdims=True)
            o_ref[hd, 0, grp * HQ:(grp + 1) * HQ, :] = (
                d * lax.rsqrt(ms + EPS) * gain).T.astype(BF16)


def _diff_attn(proj4, vt, lam_q1, lam_k1, lam_q2, lam_k2, subln_w):
    _, b, s, _ = proj4.shape
    groups = N_HEADS // ATT_HEADS
    const = lambda *_: (0, 0)
    lam_spec = pl.BlockSpec((1, DIFF_HEAD_DIM), const)
    return pl.pallas_call(
        _diff_attn_kernel,
        grid=(b, groups, s // TQ),
        in_specs=[
            pl.BlockSpec((ATT_HEADS, 1, TQ, HEAD_W),
                         lambda bi, gi, qi: (SLOT_DQ * groups + gi, bi, qi, 0)),
            pl.BlockSpec((ATT_HEADS, 1, s, HEAD_W),
                         lambda bi, gi, qi: (SLOT_DK * groups + gi, bi, 0, 0)),
            pl.BlockSpec((ATT_HEADS, V_ROWS, s), lambda bi, gi, qi: (gi, 0, bi)),
            lam_spec, lam_spec, lam_spec, lam_spec,
            pl.BlockSpec((HEAD_W, 1), const),
        ],
        out_specs=pl.BlockSpec((ATT_HEADS, 1, TQ, HEAD_W), lambda bi, gi, qi: (gi, bi, qi, 0)),
        out_shape=jax.ShapeDtypeStruct((N_HEADS, b, s, HEAD_W), BF16),
        scratch_shapes=[
            pltpu.VMEM((ATT_HEADS, 2 * TQ, HEAD_W), BF16),
            pltpu.VMEM((2, ATT_HEADS, TK, 2 * TQ), F32),
            pltpu.VMEM((ATT_HEADS, 1, 2 * TQ), F32),
            pltpu.VMEM((ATT_HEADS, V_ROWS, 2 * TQ), F32),
        ],
        compiler_params=pltpu.CompilerParams(
            dimension_semantics=("parallel", "parallel", "parallel"),
            vmem_limit_bytes=VMEM_LIMIT),
        name="diff_attn",
    )(proj4, proj4, vt, lam_q1, lam_k1, lam_q2, lam_k2, subln_w)


def _hgrn_constants():
    n = HGRN_CHUNK
    t = np.arange(n)[:, None]
    s = np.arange(n)[None, :]
    ltri = (s <= t)
    masks = [(s == t)]
    fine, roles = [], []
    for lvl in range(N_LEVELS):
        hs = n >> (lvl + 1)
        blk = t // (2 * hs)
        mid = blk * 2 * hs + hs - 1
        is_q = (t % (2 * hs)) >= hs
        if lvl >= N_COARSE:
            fine.append(np.where(is_q, (s > mid) & (s <= t), (s > t) & (s <= mid)))
        masks.append((blk == (s // (2 * hs))) & is_q & ((s % (2 * hs)) < hs))
        roles.append(np.broadcast_to(np.where(is_q, 1.0, -1.0), (n, HEAD_W)))
    ltri = ltri.astype(np.float32)
    wfine = np.concatenate(fine, axis=0).astype(np.float32)
    masks = np.stack(masks).astype(np.float32)
    roles = np.stack(roles).astype(np.float32)
    return ltri, wfine, masks, roles


def _hgrn_kernel(q_ref, k_ref, v_ref, gate_ref, g_ref, ltri_ref, wfine_ref, mask_ref, role_ref,
                 nw_ref, o_ref, st_ref):
    n_hd = q_ref.shape[0]
    n_chunks = q_ref.shape[2] // HGRN_CHUNK
    st_ref[...] = jnp.zeros(st_ref.shape, F32)
    ltri = ltri_ref[...]
    wfine = wfine_ref[...]
    nw = nw_ref[...]

    def exponents(g):
        g_hi, g_lo = _split_bf16(g)
        return _dot(ltri, g_hi) + _dot(ltri, g_lo), _dot(wfine, g_hi)

    def decays(b, fine):
        def row_bcast(r, n):
            return jnp.broadcast_to(b[r:r + 1, :], (n, HEAD_W))

        e_b = jnp.exp2(b)
        e_u = jnp.exp2(row_bcast(HGRN_CHUNK - 1, HGRN_CHUNK) - b)
        z = []
        for lvl in range(N_COARSE):
            hs = HGRN_CHUNK >> (lvl + 1)
            b_mid = jnp.concatenate([row_bcast(blk * 2 * hs + hs - 1, 2 * hs)
                                     for blk in range(HGRN_CHUNK // (2 * hs))], axis=0)
            z.append(jnp.exp2((b - b_mid) * role_ref[lvl]))
        fine = jnp.exp2(fine)
        z += [fine[i * HGRN_CHUNK:(i + 1) * HGRN_CHUNK] for i in range(N_LEVELS - N_COARSE)]
        return e_b, e_u, z

    def group(gi, carry):
        base = gi * (HGRN_GROUP * HGRN_CHUNK)
        items = [(hd, pl.ds(pl.multiple_of(base + u * HGRN_CHUNK, HGRN_CHUNK), HGRN_CHUNK))
                 for u in range(HGRN_GROUP) for hd in range(n_hd)]
        q_b = [q_ref[hd, 0, rows, :] for hd, rows in items]
        k_b = [k_ref[hd, 0, rows, :] for hd, rows in items]
        v_b = [v_ref[hd, 0, rows, :] for hd, rows in items]
        expo = [exponents(g_ref[hd, 0, rows, :]) for hd, rows in items]
        dec = [decays(*e) for e in expo]

        def by_role(lvl, q, k):
            if lvl >= N_COARSE:
                return jnp.where(role_ref[lvl] > 0.0, q, k)
            hs = HGRN_CHUNK >> (lvl + 1)
            return jnp.concatenate([(q if part % 2 else k)[part * hs:(part + 1) * hs]
                                    for part in range(HGRN_CHUNK // hs)], axis=0)

        xz, q_dec, k_dec = [], [], []
        for n, (e_b, e_u, z) in enumerate(dec):
            q = q_b[n].astype(F32)
            k = k_b[n].astype(F32)
            xz.append([(by_role(lvl, q, k) * z[lvl]).astype(BF16) for lvl in range(N_LEVELS)])
            q_dec.append((q * e_b).astype(BF16))
            k_dec.append((k * e_u).astype(BF16))

        pair = [[_nt_dot(q_b[n], k_b[n])] + [_nt_dot(x, x) for x in xz[n]]
                for n in range(len(items))]
        kv = [_tn_dot(v_b[n], k_dec[n]) for n in range(len(items))]
        owns = [mask_ref[lvl] > 0.5 for lvl in range(N_LEVELS + 1)]
        att = []
        for p in pair:
            a = jnp.where(owns[0], p[0], 0.0)
            for lvl in range(1, N_LEVELS + 1):
                a = jnp.where(owns[lvl], p[lvl], a)
            att.append(a.astype(BF16))
        o = [_dot(att[n], v_b[n]) for n in range(len(items))]

        st = [st_ref[hd] for hd in range(n_hd)]
        for n, (hd, rows) in enumerate(items):
            o[n] = o[n] + _nt_dot(q_dec[n], st[hd].astype(BF16))
            st[hd] = st[hd] * dec[n][0][HGRN_CHUNK - 1:HGRN_CHUNK, :] + kv[n]
        for hd in range(n_hd):
            st_ref[hd] = st[hd]

        for n, (hd, rows) in enumerate(items):
            ms = jnp.mean(o[n] * o[n], axis=-1, keepdims=True)
            y = o[n] * lax.rsqrt(ms + EPS) * nw * gate_ref[hd, 0, rows, :].astype(F32)
            o_ref[hd, 0, rows, :] = y.astype(BF16)
        return carry

    lax.fori_loop(0, n_chunks // HGRN_GROUP, group, 0, unroll=2)


def _hgrn2(proj4, g4, norm_w):
    _, b, s, _ = proj4.shape
    ltri, wfine, masks, roles = _hgrn_constants()
    heads_spec = lambda sec: pl.BlockSpec((N_HEADS, 1, s, HEAD_W), lambda bi: (sec, bi, 0, 0))
    return pl.pallas_call(
        _hgrn_kernel,
        grid=(b,),
        in_specs=[
            heads_spec(SLOT_HQ), heads_spec(SLOT_HK), heads_spec(SLOT_HI), heads_spec(SLOT_HG),
            heads_spec(0),
            pl.BlockSpec(ltri.shape, lambda *_: (0, 0)),
            pl.BlockSpec(wfine.shape, lambda *_: (0, 0)),
            pl.BlockSpec(masks.shape, lambda *_: (0, 0, 0)),
            pl.BlockSpec(roles.shape, lambda *_: (0, 0, 0)),
            pl.BlockSpec((1, HEAD_W), lambda *_: (0, 0)),
        ],
        out_specs=heads_spec(0),
        out_shape=jax.ShapeDtypeStruct((N_HEADS, b, s, HEAD_W), BF16),
        scratch_shapes=[pltpu.VMEM((N_HEADS, HEAD_W, HEAD_W), F32)],
        compiler_params=pltpu.CompilerParams(
            dimension_semantics=("parallel",), vmem_limit_bytes=VMEM_LIMIT),
        name="hgrn2",
    )(proj4, proj4, proj4, proj4, g4, jnp.asarray(ltri, BF16), jnp.asarray(wfine, BF16),
      jnp.asarray(masks), jnp.asarray(roles), norm_w)


def _ff_chunks():
    chunks, c0 = [], 0
    while c0 < D_FF:
        fc = min(FF_CHUNK, D_FF - c0)
        chunks.append((c0, fc))
        c0 += fc
    return chunks


def _out_ffn_kernel(tiles_per_seq, x_ref, od_ref, oh_ref, wout_ref, ln2_ref, wup_ref, cw_ref,
                    cb_ref, wdn_ref, out_ref, tail_ref, act_ref):
    i = pl.program_id(0)
    mix = jnp.concatenate([od_ref[hd] for hd in range(N_HEADS)]
                          + [oh_ref[hd] for hd in range(N_HEADS)], axis=1)
    x1 = x_ref[...] + _dot(mix, wout_ref[...])
    ms = jnp.mean(x1 * x1, axis=-1, keepdims=True)
    h2 = (x1 * lax.rsqrt(ms + EPS) * ln2_ref[...]).astype(BF16)

    seq_start = (i % tiles_per_seq) == 0
    tm = x1.shape[0]
    for c0, fc in _ff_chunks():
        cols = slice(c0, c0 + fc)
        u = _dot(h2, wup_ref[:, cols])
        v = _dot(h2, wup_ref[:, D_FF + c0:D_FF + c0 + fc])
        tail = jnp.where(seq_start, 0.0, tail_ref[:, cols])
        tail_ref[:, cols] = u[tm - SUBLANES:, :]
        ext = jnp.concatenate([tail, u], axis=0)
        u1 = pltpu.roll(ext, 1, 0)[SUBLANES:]
        u2 = pltpu.roll(ext, 2, 0)[SUBLANES:]
        cw = cw_ref[:, cols]
        c = cb_ref[:, cols] + u2 * cw[0:1] + u1 * cw[1:2] + u * cw[2:3]
        act_ref[:, cols] = (c * _sigmoid(c) * v).astype(BF16)
    out_ref[...] = x1 + _dot(act_ref[...], wdn_ref[...])


def _out_ffn(x2d, o_diff, o_hgrn, w_out, ln2_w, w_up, conv_w, conv_b, w_down, seq_len):
    t = x2d.shape[0]
    const = lambda *_: (0, 0)
    resident = functools.partial(pl.BlockSpec, index_map=const, pipeline_mode=pl.Buffered(1))
    return pl.pallas_call(
        functools.partial(_out_ffn_kernel, seq_len // TM_FFN),
        grid=(t // TM_FFN,),
        in_specs=[
            pl.BlockSpec((TM_FFN, D_MODEL), lambda i: (i, 0)),
            pl.BlockSpec((N_HEADS, TM_FFN, HEAD_W), lambda i: (0, i, 0)),
            pl.BlockSpec((N_HEADS, TM_FFN, HEAD_W), lambda i: (0, i, 0)),
            resident((D_MODEL, D_MODEL)),
            pl.BlockSpec((1, D_MODEL), const),
            resident((D_MODEL, 2 * D_FF)),
            pl.BlockSpec((CONV_WIDTH, D_FF), const),
            pl.BlockSpec((1, D_FF), const),
            resident((D_FF, D_MODEL)),
        ],
        out_specs=pl.BlockSpec((TM_FFN, D_MODEL), lambda i: (i, 0)),
        out_shape=jax.ShapeDtypeStruct((t, D_MODEL), F32),
        scratch_shapes=[
            pltpu.VMEM((SUBLANES, D_FF), F32),
            pltpu.VMEM((TM_FFN, D_FF), BF16),
        ],
        compiler_params=pltpu.CompilerParams(
            dimension_semantics=("arbitrary",), vmem_limit_bytes=VMEM_LIMIT),
        name="out_ffn",
    )(x2d, o_diff, o_hgrn, w_out, ln2_w, w_up, conv_w, conv_b, w_down)


def kernel(x, ln1_w, w_in, q_norm_w, k_norm_w, lam_q1, lam_k1, lam_q2, lam_k2, diff_subln_w,
           hgrn_lb_logits, hgrn_norm_w, w_out, ln2_w, w_up, conv_w, conv_b, w_down):
    b, s, d = x.shape
    depth = ln1_w.shape[0]
    assert depth == 1 and d == D_MODEL and s % TM_FFN == 0 and s % TQ == 0 and N_SUB % 2 == 0
    t = b * s
    x2d = x.reshape(t, d)
    l = 0
    tile2 = lambda w: jnp.concatenate([w, w], axis=-1)[None, :]

    proj, g, vt, (w_out_b, w_up_b, w_down_b) = _in_proj(
        x2d, ln1_w[l][None, :], w_in[l].astype(BF16), tile2(q_norm_w[l]), tile2(k_norm_w[l]),
        hgrn_lb_logits, [w_out[l], w_up[l], w_down[l]])
    proj4 = proj.reshape(N_SLOT * N_HEADS, b, s, HEAD_W)
    g4 = g.reshape(N_HEADS, b, s, HEAD_W)

    o_diff = _diff_attn(proj4, vt, lam_q1[l][None, :], lam_k1[l][None, :], lam_q2[l][None, :],
                        lam_k2[l][None, :], diff_subln_w[l][:, None])
    o_hgrn = _hgrn2(proj4, g4, hgrn_norm_w[l][None, :])

    out = _out_ffn(x2d, o_diff.reshape(N_HEADS, t, HEAD_W), o_hgrn.reshape(N_HEADS, t, HEAD_W),
                   w_out_b, ln2_w[l][None, :], w_up_b, conv_w[l], conv_b[l][None, :], w_down_b, s)
    return out.reshape(b, s, d)
```

```python
import functools
import math

import numpy as np
import jax
import jax.numpy as jnp
from jax import lax
from jax.experimental import pallas as pl
from jax.experimental.pallas import tpu as pltpu

F32 = jnp.float32
BF16 = jnp.bfloat16

D_MODEL = 1024
CHUNK = 64
HEAD_W = 128
DIFF_HEAD_DIM = 64
N_HEADS = 4
SEC_W = N_HEADS * HEAD_W
N_SEC = 7
IN_COLS = N_SEC * SEC_W
D_FF = 2816
CONV_WIDTH = 3
EPS = 1e-6
LAM_INIT = 0.8 - 0.6 * math.exp(-0.3 * 0)
LOG2E = math.log2(math.e)

SEC_DQ, SEC_DK, SEC_DV, SEC_HQ, SEC_HK, SEC_HI, SEC_HG = range(N_SEC)
SLOT_DQ, SLOT_DK, SLOT_HQ, SLOT_HK, SLOT_HI, SLOT_HG = range(6)
N_SLOT = 6
BF16_SUBLANES = 16
V_ROWS = HEAD_W + BF16_SUBLANES

LANES = 128
SUBLANES = 8
MXU_W = 256
TM_PROJ = 1024
ATT_HEADS = 4
TQ = 1024
TK = 256
N_SUB = TQ // TK
HQ = TK
TM_FFN = 512
FF_CHUNK = 768
VMEM_LIMIT = 56 * 1024 * 1024

HGRN_CHUNK = 64
N_LEVELS = 6
N_COARSE = 3
HGRN_GROUP = 4


def _nt_dot(a, b):
    return lax.dot_general(a, b, (((1,), (1,)), ((), ())), preferred_element_type=F32)


def _tn_dot(a, b):
    return lax.dot_general(a, b, (((0,), (0,)), ((), ())), preferred_element_type=F32)


def _dot(a, b):
    return jnp.dot(a, b, preferred_element_type=F32)


def _split_bf16(x):
    hi = x.astype(BF16)
    lo = (x - hi.astype(F32)).astype(BF16)
    return hi, lo


def _sigmoid(x):
    return 1.0 / (1.0 + jnp.exp(-x))


def _in_proj_kernel(n_later, x_ref, ln1_ref, w_ref, qw_ref, kw_ref, lbl_ref, *rest):
    later_f32, (proj_ref, g_ref, vt_ref), later_bf16 = (
        rest[:n_later], rest[n_later:n_later + 3], rest[n_later + 3:])
    for src, dst in zip(later_f32, later_bf16):
        dst[...] = src[...].astype(BF16)

    x = x_ref[...]
    ms = jnp.mean(x * x, axis=-1, keepdims=True)
    h = (x * lax.rsqrt(ms + EPS) * ln1_ref[...]).astype(BF16)

    r = lax.broadcasted_iota(jnp.int32, (MXU_W, MXU_W), 0) // DIFF_HEAD_DIM
    c = lax.broadcasted_iota(jnp.int32, (MXU_W, MXU_W), 1) // DIFF_HEAD_DIM
    grp = jnp.where(r == c, 1.0, 0.0).astype(BF16)

    def head_cols(a, hd):
        return a[:, hd * HEAD_W:(hd + 1) * HEAD_W]

    def store_heads(slot, a):
        for hd in range(N_HEADS):
            proj_ref[slot * N_HEADS + hd] = head_cols(a, hd).astype(BF16)

    def project(j):
        return _dot(h, w_ref[:, j * SEC_W:(j + 1) * SEC_W])

    acc = [None] * N_SEC
    halves = [slice(c0, c0 + MXU_W) for c0 in range(0, SEC_W, MXU_W)]
    ss = {}
    for sec in (SEC_DQ, SEC_DK):
        acc[sec] = project(sec)
        ss[sec] = [_dot((acc[sec][:, cols] * acc[sec][:, cols]).astype(BF16), grp)
                   for cols in halves]
    for sec in (SEC_HK, SEC_HG, SEC_DV, SEC_HQ, SEC_HI):
        acc[sec] = project(sec)

    q_gain = qw_ref[...] * (DIFF_HEAD_DIM ** -0.5 * LOG2E)
    for sec, slot, w_norm in ((SEC_DQ, SLOT_DQ, q_gain), (SEC_DK, SLOT_DK, kw_ref[...])):
        w2 = jnp.concatenate([w_norm] * (MXU_W // HEAD_W), axis=1)
        y = [acc[sec][:, cols] * lax.rsqrt(ss[sec][n] * (1.0 / DIFF_HEAD_DIM) + EPS) * w2
             for n, cols in enumerate(halves)]
        store_heads(slot, jnp.concatenate(y, axis=1))

    for hd in range(N_HEADS):
        vt_ref[hd, 0:HEAD_W, :] = head_cols(acc[SEC_DV], hd).T.astype(BF16)
        vt_ref[hd, HEAD_W:V_ROWS, :] = jnp.ones((V_ROWS - HEAD_W, x.shape[0]), BF16)

    store_heads(SLOT_HQ, acc[SEC_HQ])
    store_heads(SLOT_HI, acc[SEC_HI])

    lbl = lbl_ref[...]
    e = jnp.exp(lbl - jnp.max(lbl, axis=0, keepdims=True))
    lb = e[0:1] / jnp.sum(e, axis=0, keepdims=True)
    f = lb + (1.0 - lb) * _sigmoid(acc[SEC_HK])
    log2_f = jnp.log(f) * LOG2E
    for hd in range(N_HEADS):
        g_ref[hd] = head_cols(log2_f, hd)
    store_heads(SLOT_HK, 1.0 - f)

    store_heads(SLOT_HG, acc[SEC_HG] * _sigmoid(acc[SEC_HG]))


def _in_proj(x2d, ln1_w, w_in, q_norm_w, k_norm_w, lb_logits, later_weights):
    t = x2d.shape[0]
    steps = t // TM_PROJ
    const = lambda *_: (0, 0)
    sliced = [w.reshape(steps, w.shape[0] // steps, w.shape[1]) for w in later_weights]
    slice_specs = [pl.BlockSpec((1,) + w.shape[1:], lambda i: (i, 0, 0)) for w in sliced]
    outs = pl.pallas_call(
        functools.partial(_in_proj_kernel, len(sliced)),
        grid=(steps,),
        in_specs=[
            pl.BlockSpec((TM_PROJ, D_MODEL), lambda i: (i, 0)),
            pl.BlockSpec((1, D_MODEL), const),
            pl.BlockSpec((D_MODEL, IN_COLS), const, pipeline_mode=pl.Buffered(1)),
            pl.BlockSpec((1, HEAD_W), const),
            pl.BlockSpec((1, HEAD_W), const),
            pl.BlockSpec(lb_logits.shape, const),
        ] + slice_specs,
        out_specs=[
            pl.BlockSpec((N_SLOT * N_HEADS, TM_PROJ, HEAD_W), lambda i: (0, i, 0)),
            pl.BlockSpec((N_HEADS, TM_PROJ, HEAD_W), lambda i: (0, i, 0)),
            pl.BlockSpec((N_HEADS, V_ROWS, TM_PROJ), lambda i: (0, 0, i)),
        ] + slice_specs,
        out_shape=[
            jax.ShapeDtypeStruct((N_SLOT * N_HEADS, t, HEAD_W), BF16),
            jax.ShapeDtypeStruct((N_HEADS, t, HEAD_W), F32),
            jax.ShapeDtypeStruct((N_HEADS, V_ROWS, t), BF16),
        ] + [jax.ShapeDtypeStruct(w.shape, BF16) for w in sliced],
        compiler_params=pltpu.CompilerParams(
            dimension_semantics=("parallel",), vmem_limit_bytes=VMEM_LIMIT),
        name="in_proj",
    )(x2d, ln1_w, w_in, q_norm_w, k_norm_w, lb_logits, *sliced)
    rounded = [o.reshape(w.shape) for o, w in zip(outs[3:], later_weights)]
    return outs[0], outs[1], outs[2], rounded


NEG_BIG = -1e30


def _diff_attn_kernel(q_ref, k_ref, vt_ref, lq1_ref, lk1_ref, lq2_ref, lk2_ref, sw_ref,
                      o_ref, qs_ref, s_ref, m_ref, acc_ref):
    qi = pl.program_id(2)
    n_hd = q_ref.shape[0]
    heads = range(n_hd)
    lane = lax.broadcasted_iota(jnp.int32, (HQ, HEAD_W), 1)
    for hd in heads:
        for grp in range(N_SUB):
            q = q_ref[hd, 0, grp * HQ:(grp + 1) * HQ, :]
            zero = jnp.zeros_like(q)
            qs_ref[hd, (2 * grp) * HQ:(2 * grp + 1) * HQ] = jnp.where(lane < DIFF_HEAD_DIM, q, zero)
            qs_ref[hd, (2 * grp + 1) * HQ:(2 * grp + 2) * HQ] = jnp.where(lane >= DIFF_HEAD_DIM, q, zero)

    m_ref[...] = jnp.full(m_ref.shape, NEG_BIG, F32)
    acc_ref[...] = jnp.zeros(acc_ref.shape, F32)

    ck = lax.broadcasted_iota(jnp.int32, (TK, 2 * TQ), 0) // CHUNK
    col = lax.broadcasted_iota(jnp.int32, (TK, 2 * TQ), 1)
    diag_mask = ck <= jnp.where(col >= 2 * HQ, TK // CHUNK, (col % HQ) // CHUNK)

    def kv_rows(j):
        return slice(j * TK, (j + 1) * TK)

    def score_step(j, slot, cols=slice(None)):
        rows = kv_rows(j)
        for hd in heads:
            s_ref[slot, hd, :, cols] = _nt_dot(k_ref[hd, 0, rows, :], qs_ref[hd, cols])

    def softmax_step(j, slot, cols=slice(None), mask=None):
        rows = kv_rows(j)
        p_all, alpha_all = [], []
        for hd in heads:
            s = s_ref[slot, hd, :, cols]
            if mask is not None:
                s = jnp.where(mask, s, NEG_BIG)
            m_old = m_ref[hd, :, cols]
            m_new = jnp.maximum(m_old, jnp.max(s, axis=0, keepdims=True))
            m_ref[hd, :, cols] = m_new
            alpha_all.append(jnp.exp2(m_old - m_new))
            p_all.append(jnp.exp2((s - m_new).astype(BF16)))
        pv_all = [_dot(vt_ref[hd, :, rows], p_all[hd]) for hd in heads]
        for hd in heads:
            acc_ref[hd, :, cols] = alpha_all[hd] * acc_ref[hd, :, cols] + pv_all[hd]

    def run(q):
        blocks = [(j, slice(None), None) for j in range(N_SUB * q)]
        blocks += [(N_SUB * q + r, slice(2 * HQ * r, 2 * TQ), diag_mask[:, 0:2 * TQ - 2 * HQ * r])
                   for r in range(N_SUB)]
        score_step(blocks[0][0], 0, blocks[0][1])
        for n, (j, cols, mask) in enumerate(blocks):
            if n + 1 < len(blocks):
                score_step(blocks[n + 1][0], (n + 1) % 2, blocks[n + 1][1])
            softmax_step(j, n % 2, cols, mask)

    for q in range(k_ref.shape[2] // TQ):
        pl.when(qi == q)(functools.partial(run, q))

    lam = (jnp.exp(jnp.sum(lq1_ref[...] * lk1_ref[...], axis=-1, keepdims=True))
           - jnp.exp(jnp.sum(lq2_ref[...] * lk2_ref[...], axis=-1, keepdims=True)) + LAM_INIT)
    gain = sw_ref[...] * (1.0 - LAM_INIT)
    for hd in heads:
        o = acc_ref[hd, 0:HEAD_W] / acc_ref[hd, HEAD_W:HEAD_W + 1]
        for grp in range(N_SUB):
            c0 = 2 * grp * HQ
            d = o[:, c0:c0 + HQ] - lam * o[:, c0 + HQ:c0 + 2 * HQ]
            ms = jnp.mean(d * d, axis=0, keepdims=True)
            o_ref[hd, 0, grp * HQ:(grp + 1) * HQ, :] = (
                d * lax.rsqrt(ms + EPS) * gain).T.astype(BF16)


def _diff_attn(proj4, vt, lam_q1, lam_k1, lam_q2, lam_k2, subln_w):
    _, b, s, _ = proj4.shape
    groups = N_HEADS // ATT_HEADS
    const = lambda *_: (0, 0)
    lam_spec = pl.BlockSpec((1, DIFF_HEAD_DIM), const)
    return pl.pallas_call(
        _diff_attn_kernel,
        grid=(b, groups, s // TQ),
        in_specs=[
            pl.BlockSpec((ATT_HEADS, 1, TQ, HEAD_W),
                         lambda bi, gi, qi: (SLOT_DQ * groups + gi, bi, qi, 0)),
            pl.BlockSpec((ATT_HEADS, 1, s, HEAD_W),
                         lambda bi, gi, qi: (SLOT_DK * groups + gi, bi, 0, 0)),
            pl.BlockSpec((ATT_HEADS, V_ROWS, s), lambda bi, gi, qi: (gi, 0, bi)),
            lam_spec, lam_spec, lam_spec, lam_spec,
            pl.BlockSpec((HEAD_W, 1), const),
        ],
        out_specs=pl.BlockSpec((ATT_HEADS, 1, TQ, HEAD_W), lambda bi, gi, qi: (gi, bi, qi, 0)),
        out_shape=jax.ShapeDtypeStruct((N_HEADS, b, s, HEAD_W), BF16),
        scratch_shapes=[
            pltpu.VMEM((ATT_HEADS, 2 * TQ, HEAD_W), BF16),
            pltpu.VMEM((2, ATT_HEADS, TK, 2 * TQ), F32),
            pltpu.VMEM((ATT_HEADS, 1, 2 * TQ), F32),
            pltpu.VMEM((ATT_HEADS, V_ROWS, 2 * TQ), F32),
        ],
        compiler_params=pltpu.CompilerParams(
            dimension_semantics=("parallel", "parallel", "parallel"),
            vmem_limit_bytes=VMEM_LIMIT),
        name="diff_attn",
    )(proj4, proj4, vt, lam_q1, lam_k1, lam_q2, lam_k2, subln_w)


def _hgrn_constants():
    n = HGRN_CHUNK
    t = np.arange(n)[:, None]
    s = np.arange(n)[None, :]
    ltri = (s <= t)
    masks = [(s == t)]
    fine, roles = [], []
    for lvl in range(N_LEVELS):
        hs = n >> (lvl + 1)
        blk = t // (2 * hs)
        mid = blk * 2 * hs + hs - 1
        is_q = (t % (2 * hs)) >= hs
        if lvl >= N_COARSE:
            fine.append(np.where(is_q, (s > mid) & (s <= t), (s > t) & (s <= mid)))
        masks.append((blk == (s // (2 * hs))) & is_q & ((s % (2 * hs)) < hs))
        roles.append(np.broadcast_to(np.where(is_q, 1.0, -1.0), (n, HEAD_W)))
    ltri = ltri.astype(np.float32)
    wfine = np.concatenate(fine, axis=0).astype(np.float32)
    masks = np.stack(masks).astype(np.float32)
    roles = np.stack(roles).astype(np.float32)
    return ltri, wfine, masks, roles


def _hgrn_kernel(q_ref, k_ref, v_ref, gate_ref, g_ref, ltri_ref, wfine_ref, mask_ref, role_ref,
                 nw_ref, o_ref, st_ref):
    n_hd = q_ref.shape[0]
    n_chunks = q_ref.shape[2] // HGRN_CHUNK
    st_ref[...] = jnp.zeros(st_ref.shape, F32)
    ltri = ltri_ref[...]
    wfine = wfine_ref[...]
    nw = nw_ref[...]

    def exponents(g):
        g_hi, g_lo = _split_bf16(g)
        return _dot(ltri, g_hi) + _dot(ltri, g_lo), _dot(wfine, g_hi)

    def decays(b, fine):
        def row_bcast(r, n):
            return jnp.broadcast_to(b[r:r + 1, :], (n, HEAD_W))

        e_b = jnp.exp2(b)
        e_u = jnp.exp2(row_bcast(HGRN_CHUNK - 1, HGRN_CHUNK) - b)
        z = []
        for lvl in range(N_COARSE):
            hs = HGRN_CHUNK >> (lvl + 1)
            b_mid = jnp.concatenate([row_bcast(blk * 2 * hs + hs - 1, 2 * hs)
                                     for blk in range(HGRN_CHUNK // (2 * hs))], axis=0)
            z.append(jnp.exp2((b - b_mid) * role_ref[lvl]))
        fine = jnp.exp2(fine)
        z += [fine[i * HGRN_CHUNK:(i + 1) * HGRN_CHUNK] for i in range(N_LEVELS - N_COARSE)]
        return e_b, e_u, z

    def group(gi, carry):
        base = gi * (HGRN_GROUP * HGRN_CHUNK)
        items = [(hd, pl.ds(pl.multiple_of(base + u * HGRN_CHUNK, HGRN_CHUNK), HGRN_CHUNK))
                 for u in range(HGRN_GROUP) for hd in range(n_hd)]
        q_b = [q_ref[hd, 0, rows, :] for hd, rows in items]
        k_b = [k_ref[hd, 0, rows, :] for hd, rows in items]
        v_b = [v_ref[hd, 0, rows, :] for hd, rows in items]
        expo = [exponents(g_ref[hd, 0, rows, :]) for hd, rows in items]
        dec = [decays(*e) for e in expo]

        def by_role(lvl, q, k):
            if lvl >= N_COARSE:
                return jnp.where(role_ref[lvl] > 0.0, q, k)
            hs = HGRN_CHUNK >> (lvl + 1)
            return jnp.concatenate([(q if part % 2 else k)[part * hs:(part + 1) * hs]
                                    for part in range(HGRN_CHUNK // hs)], axis=0)

        xz, q_dec, k_dec = [], [], []
        for n, (e_b, e_u, z) in enumerate(dec):
            q = q_b[n].astype(F32)
            k = k_b[n].astype(F32)
            xz.append([(by_role(lvl, q, k) * z[lvl]).astype(BF16) for lvl in range(N_LEVELS)])
            q_dec.append((q * e_b).astype(BF16))
            k_dec.append((k * e_u).astype(BF16))

        pair = [[_nt_dot(q_b[n], k_b[n])] + [_nt_dot(x, x) for x in xz[n]]
                for n in range(len(items))]
        kv = [_tn_dot(v_b[n], k_dec[n]) for n in range(len(items))]
        owns = [mask_ref[lvl] > 0.5 for lvl in range(N_LEVELS + 1)]
        att = []
        for p in pair:
            a = jnp.where(owns[0], p[0], 0.0)
            for lvl in range(1, N_LEVELS + 1):
                a = jnp.where(owns[lvl], p[lvl], a)
            att.append(a.astype(BF16))
        o = [_dot(att[n], v_b[n]) for n in range(len(items))]

        st = [st_ref[hd] for hd in range(n_hd)]
        for n, (hd, rows) in enumerate(items):
            o[n] = o[n] + _nt_dot(q_dec[n], st[hd].astype(BF16))
            st[hd] = st[hd] * dec[n][0][HGRN_CHUNK - 1:HGRN_CHUNK, :] + kv[n]
        for hd in range(n_hd):
            st_ref[hd] = st[hd]

        for n, (hd, rows) in enumerate(items):
            ms = jnp.mean(o[n] * o[n], axis=-1, keepdims=True)
            y = o[n] * lax.rsqrt(ms + EPS) * nw * gate_ref[hd, 0, rows, :].astype(F32)
            o_ref[hd, 0, rows, :] = y.astype(BF16)
        return carry

    lax.fori_loop(0, n_chunks // HGRN_GROUP, group, 0, unroll=4)


def _hgrn2(proj4, g4, norm_w):
    _, b, s, _ = proj4.shape
    ltri, wfine, masks, roles = _hgrn_constants()
    heads_spec = lambda sec: pl.BlockSpec((N_HEADS, 1, s, HEAD_W), lambda bi: (sec, bi, 0, 0))
    return pl.pallas_call(
        _hgrn_kernel,
        grid=(b,),
        in_specs=[
            heads_spec(SLOT_HQ), heads_spec(SLOT_HK), heads_spec(SLOT_HI), heads_spec(SLOT_HG),
            heads_spec(0),
            pl.BlockSpec(ltri.shape, lambda *_: (0, 0)),
            pl.BlockSpec(wfine.shape, lambda *_: (0, 0)),
            pl.BlockSpec(masks.shape, lambda *_: (0, 0, 0)),
            pl.BlockSpec(roles.shape, lambda *_: (0, 0, 0)),
            pl.BlockSpec((1, HEAD_W), lambda *_: (0, 0)),
        ],
        out_specs=heads_spec(0),
        out_shape=jax.ShapeDtypeStruct((N_HEADS, b, s, HEAD_W), BF16),
        scratch_shapes=[pltpu.VMEM((N_HEADS, HEAD_W, HEAD_W), F32)],
        compiler_params=pltpu.CompilerParams(
            dimension_semantics=("parallel",), vmem_limit_bytes=VMEM_LIMIT),
        name="hgrn2",
    )(proj4, proj4, proj4, proj4, g4, jnp.asarray(ltri, BF16), jnp.asarray(wfine, BF16),
      jnp.asarray(masks), jnp.asarray(roles), norm_w)


def _ff_chunks():
    chunks, c0 = [], 0
    while c0 < D_FF:
        fc = min(FF_CHUNK, D_FF - c0)
        chunks.append((c0, fc))
        c0 += fc
    return chunks


def _out_ffn_kernel(tiles_per_seq, x_ref, od_ref, oh_ref, wout_ref, ln2_ref, wup_ref, cw_ref,
                    cb_ref, wdn_ref, out_ref, tail_ref, act_ref):
    i = pl.program_id(0)
    mix = jnp.concatenate([od_ref[hd] for hd in range(N_HEADS)]
                          + [oh_ref[hd] for hd in range(N_HEADS)], axis=1)
    x1 = x_ref[...] + _dot(mix, wout_ref[...])
    ms = jnp.mean(x1 * x1, axis=-1, keepdims=True)
    h2 = (x1 * lax.rsqrt(ms + EPS) * ln2_ref[...]).astype(BF16)

    seq_start = (i % tiles_per_seq) == 0
    tm = x1.shape[0]
    for c0, fc in _ff_chunks():
        cols = slice(c0, c0 + fc)
        u = _dot(h2, wup_ref[:, cols])
        v = _dot(h2, wup_ref[:, D_FF + c0:D_FF + c0 + fc])
        tail = jnp.where(seq_start, 0.0, tail_ref[:, cols])
        tail_ref[:, cols] = u[tm - SUBLANES:, :]
        ext = jnp.concatenate([tail, u], axis=0)
        u1 = pltpu.roll(ext, 1, 0)[SUBLANES:]
        u2 = pltpu.roll(ext, 2, 0)[SUBLANES:]
        cw = cw_ref[:, cols]
        c = cb_ref[:, cols] + u2 * cw[0:1] + u1 * cw[1:2] + u * cw[2:3]
        act_ref[:, cols] = (c * _sigmoid(c) * v).astype(BF16)
    out_ref[...] = x1 + _dot(act_ref[...], wdn_ref[...])


def _out_ffn(x2d, o_diff, o_hgrn, w_out, ln2_w, w_up, conv_w, conv_b, w_down, seq_len):
    t = x2d.shape[0]
    const = lambda *_: (0, 0)
    resident = functools.partial(pl.BlockSpec, index_map=const, pipeline_mode=pl.Buffered(1))
    return pl.pallas_call(
        functools.partial(_out_ffn_kernel, seq_len // TM_FFN),
        grid=(t // TM_FFN,),
        in_specs=[
            pl.BlockSpec((TM_FFN, D_MODEL), lambda i: (i, 0)),
            pl.BlockSpec((N_HEADS, TM_FFN, HEAD_W), lambda i: (0, i, 0)),
            pl.BlockSpec((N_HEADS, TM_FFN, HEAD_W), lambda i: (0, i, 0)),
            resident((D_MODEL, D_MODEL)),
            pl.BlockSpec((1, D_MODEL), const),
            resident((D_MODEL, 2 * D_FF)),
            pl.BlockSpec((CONV_WIDTH, D_FF), const),
            pl.BlockSpec((1, D_FF), const),
            resident((D_FF, D_MODEL)),
        ],
        out_specs=pl.BlockSpec((TM_FFN, D_MODEL), lambda i: (i, 0)),
        out_shape=jax.ShapeDtypeStruct((t, D_MODEL), F32),
        scratch_shapes=[
            pltpu.VMEM((SUBLANES, D_FF), F32),
            pltpu.VMEM((TM_FFN, D_FF), BF16),
        ],
        compiler_params=pltpu.CompilerParams(
            dimension_semantics=("arbitrary",), vmem_limit_bytes=VMEM_LIMIT),
        name="out_ffn",
    )(x2d, o_diff, o_hgrn, w_out, ln2_w, w_up, conv_w, conv_b, w_down)


def kernel(x, ln1_w, w_in, q_norm_w, k_norm_w, lam_q1, lam_k1, lam_q2, lam_k2, diff_subln_w,
           hgrn_lb_logits, hgrn_norm_w, w_out, ln2_w, w_up, conv_w, conv_b, w_down):
    b, s, d = x.shape
    depth = ln1_w.shape[0]
    assert depth == 1 and d == D_MODEL and s % TM_FFN == 0 and s % TQ == 0 and N_SUB % 2 == 0
    t = b * s
    x2d = x.reshape(t, d)
    l = 0
    tile2 = lambda w: jnp.concatenate([w, w], axis=-1)[None, :]

    proj, g, vt, (w_out_b, w_up_b, w_down_b) = _in_proj(
        x2d, ln1_w[l][None, :], w_in[l].astype(BF16), tile2(q_norm_w[l]), tile2(k_norm_w[l]),
        hgrn_lb_logits, [w_out[l], w_up[l], w_down[l]])
    proj4 = proj.reshape(N_SLOT * N_HEADS, b, s, HEAD_W)
    g4 = g.reshape(N_HEADS, b, s, HEAD_W)

    o_diff = _diff_attn(proj4, vt, lam_q1[l][None, :], lam_k1[l][None, :], lam_q2[l][None, :],
                        lam_k2[l][None, :], diff_subln_w[l][:, None])
    o_hgrn = _hgrn2(proj4, g4, hgrn_norm_w[l][None, :])

    out = _out_ffn(x2d, o_diff.reshape(N_HEADS, t, HEAD_W), o_hgrn.reshape(N_HEADS, t, HEAD_W),
                   w_out_b, ln2_w[l][None, :], w_up_b, conv_w[l], conv_b[l][None, :], w_down_b, s)
    return out.reshape(b, s, d)
```

```python
import functools
import math

import numpy as np
import jax
import jax.numpy as jnp
from jax import lax
from jax.experimental import pallas as pl
from jax.experimental.pallas import tpu as pltpu

F32 = jnp.float32
BF16 = jnp.bfloat16

D_MODEL = 1024
CHUNK = 64
HEAD_W = 128
DIFF_HEAD_DIM = 64
N_HEADS = 4
SEC_W = N_HEADS * HEAD_W
N_SEC = 7
IN_COLS = N_SEC * SEC_W
D_FF = 2816
CONV_WIDTH = 3
EPS = 1e-6
LAM_INIT = 0.8 - 0.6 * math.exp(-0.3 * 0)
LOG2E = math.log2(math.e)

SEC_DQ, SEC_DK, SEC_DV, SEC_HQ, SEC_HK, SEC_HI, SEC_HG = range(N_SEC)
SLOT_DQ, SLOT_DK, SLOT_HQ, SLOT_HK, SLOT_HI, SLOT_HG = range(6)
N_SLOT = 6
BF16_SUBLANES = 16
V_ROWS = HEAD_W + BF16_SUBLANES

LANES = 128
SUBLANES = 8
MXU_W = 256
TM_PROJ = 1024
ATT_HEADS = 4
TQ = 1024
TK = 256
N_SUB = TQ // TK
HQ = TK
TM_FFN = 512
FF_CHUNK = 768
VMEM_LIMIT = 56 * 1024 * 1024

HGRN_CHUNK = 64
N_LEVELS = 6
N_COARSE = 3
HGRN_GROUP = 4


def _nt_dot(a, b):
    return lax.dot_general(a, b, (((1,), (1,)), ((), ())), preferred_element_type=F32)


def _tn_dot(a, b):
    return lax.dot_general(a, b, (((0,), (0,)), ((), ())), preferred_element_type=F32)


def _dot(a, b):
    return jnp.dot(a, b, preferred_element_type=F32)


def _split_bf16(x):
    hi = x.astype(BF16)
    lo = (x - hi.astype(F32)).astype(BF16)
    return hi, lo


def _sigmoid(x):
    return 1.0 / (1.0 + jnp.exp(-x))


def _in_proj_kernel(n_later, x_ref, ln1_ref, w_ref, qw_ref, kw_ref, lbl_ref, *rest):
    later_f32, (proj_ref, g_ref, vt_ref), later_bf16 = (
        rest[:n_later], rest[n_later:n_later + 3], rest[n_later + 3:])
    for src, dst in zip(later_f32, later_bf16):
        dst[...] = src[...].astype(BF16)

    x = x_ref[...]
    ms = jnp.mean(x * x, axis=-1, keepdims=True)
    h = (x * lax.rsqrt(ms + EPS) * ln1_ref[...]).astype(BF16)

    r = lax.broadcasted_iota(jnp.int32, (MXU_W, MXU_W), 0) // DIFF_HEAD_DIM
    c = lax.broadcasted_iota(jnp.int32, (MXU_W, MXU_W), 1) // DIFF_HEAD_DIM
    grp = jnp.where(r == c, 1.0, 0.0).astype(BF16)

    def head_cols(a, hd):
        return a[:, hd * HEAD_W:(hd + 1) * HEAD_W]

    def store_heads(slot, a):
        for hd in range(N_HEADS):
            proj_ref[slot * N_HEADS + hd] = head_cols(a, hd).astype(BF16)

    def project(j):
        return _dot(h, w_ref[:, j * SEC_W:(j + 1) * SEC_W])

    acc = [None] * N_SEC
    halves = [slice(c0, c0 + MXU_W) for c0 in range(0, SEC_W, MXU_W)]
    ss = {}
    for sec in (SEC_DQ, SEC_DK):
        acc[sec] = project(sec)
        ss[sec] = [_dot((acc[sec][:, cols] * acc[sec][:, cols]).astype(BF16), grp)
                   for cols in halves]
    for sec in (SEC_HK, SEC_HG, SEC_DV, SEC_HQ, SEC_HI):
        acc[sec] = project(sec)

    q_gain = qw_ref[...] * (DIFF_HEAD_DIM ** -0.5 * LOG2E)
    for sec, slot, w_norm in ((SEC_DQ, SLOT_DQ, q_gain), (SEC_DK, SLOT_DK, kw_ref[...])):
        w2 = jnp.concatenate([w_norm] * (MXU_W // HEAD_W), axis=1)
        y = [acc[sec][:, cols] * lax.rsqrt(ss[sec][n] * (1.0 / DIFF_HEAD_DIM) + EPS) * w2
             for n, cols in enumerate(halves)]
        store_heads(slot, jnp.concatenate(y, axis=1))

    for hd in range(N_HEADS):
        vt_ref[hd, 0:HEAD_W, :] = head_cols(acc[SEC_DV], hd).T.astype(BF16)
        vt_ref[hd, HEAD_W:V_ROWS, :] = jnp.ones((V_ROWS - HEAD_W, x.shape[0]), BF16)

    store_heads(SLOT_HQ, acc[SEC_HQ])
    store_heads(SLOT_HI, acc[SEC_HI])

    lbl = lbl_ref[...]
    e = jnp.exp(lbl - jnp.max(lbl, axis=0, keepdims=True))
    lb = e[0:1] / jnp.sum(e, axis=0, keepdims=True)
    f = lb + (1.0 - lb) * _sigmoid(acc[SEC_HK])
    log2_f = jnp.log(f) * LOG2E
    for hd in range(N_HEADS):
        g_ref[hd] = head_cols(log2_f, hd)
    store_heads(SLOT_HK, 1.0 - f)

    store_heads(SLOT_HG, acc[SEC_HG] * _sigmoid(acc[SEC_HG]))


def _in_proj(x2d, ln1_w, w_in, q_norm_w, k_norm_w, lb_logits, later_weights):
    t = x2d.shape[0]
    steps = t // TM_PROJ
    const = lambda *_: (0, 0)
    sliced = [w.reshape(steps, w.shape[0] // steps, w.shape[1]) for w in later_weights]
    slice_specs = [pl.BlockSpec((1,) + w.shape[1:], lambda i: (i, 0, 0)) for w in sliced]
    outs = pl.pallas_call(
        functools.partial(_in_proj_kernel, len(sliced)),
        grid=(steps,),
        in_specs=[
            pl.BlockSpec((TM_PROJ, D_MODEL), lambda i: (i, 0)),
            pl.BlockSpec((1, D_MODEL), const),
            pl.BlockSpec((D_MODEL, IN_COLS), const, pipeline_mode=pl.Buffered(1)),
            pl.BlockSpec((1, HEAD_W), const),
            pl.BlockSpec((1, HEAD_W), const),
            pl.BlockSpec(lb_logits.shape, const),
        ] + slice_specs,
        out_specs=[
            pl.BlockSpec((N_SLOT * N_HEADS, TM_PROJ, HEAD_W), lambda i: (0, i, 0)),
            pl.BlockSpec((N_HEADS, TM_PROJ, HEAD_W), lambda i: (0, i, 0)),
            pl.BlockSpec((N_HEADS, V_ROWS, TM_PROJ), lambda i: (0, 0, i)),
        ] + slice_specs,
        out_shape=[
            jax.ShapeDtypeStruct((N_SLOT * N_HEADS, t, HEAD_W), BF16),
            jax.ShapeDtypeStruct((N_HEADS, t, HEAD_W), F32),
            jax.ShapeDtypeStruct((N_HEADS, V_ROWS, t), BF16),
        ] + [jax.ShapeDtypeStruct(w.shape, BF16) for w in sliced],
        compiler_params=pltpu.CompilerParams(
            dimension_semantics=("parallel",), vmem_limit_bytes=VMEM_LIMIT),
        name="in_proj",
    )(x2d, ln1_w, w_in, q_norm_w, k_norm_w, lb_logits, *sliced)
    rounded = [o.reshape(w.shape) for o, w in zip(outs[3:], later_weights)]
    return outs[0], outs[1], outs[2], rounded


NEG_BIG = -1e30


def _diff_attn_kernel(q_ref, k_ref, vt_ref, lq1_ref, lk1_ref, lq2_ref, lk2_ref, sw_ref,
                      o_ref, qs_ref, s_ref, m_ref, acc_ref):
    qi = pl.program_id(2)
    n_hd = q_ref.shape[0]
    heads = range(n_hd)
    lane = lax.broadcasted_iota(jnp.int32, (HQ, HEAD_W), 1)
    for hd in heads:
        for grp in range(N_SUB):
            q = q_ref[hd, 0, grp * HQ:(grp + 1) * HQ, :]
            zero = jnp.zeros_like(q)
            qs_ref[hd, (2 * grp) * HQ:(2 * grp + 1) * HQ] = jnp.where(lane < DIFF_HEAD_DIM, q, zero)
            qs_ref[hd, (2 * grp + 1) * HQ:(2 * grp + 2) * HQ] = jnp.where(lane >= DIFF_HEAD_DIM, q, zero)

    m_ref[...] = jnp.full(m_ref.shape, NEG_BIG, F32)
    acc_ref[...] = jnp.zeros(acc_ref.shape, F32)

    ck = lax.broadcasted_iota(jnp.int32, (TK, 2 * TQ), 0) // CHUNK
    col = lax.broadcasted_iota(jnp.int32, (TK, 2 * TQ), 1)
    diag_mask = ck <= jnp.where(col >= 2 * HQ, TK // CHUNK, (col % HQ) // CHUNK)

    def kv_rows(j):
        return slice(j * TK, (j + 1) * TK)

    def score_step(j, slot, cols=slice(None)):
        rows = kv_rows(j)
        for hd in heads:
            s_ref[slot, hd, :, cols] = _nt_dot(k_ref[hd, 0, rows, :], qs_ref[hd, cols])

    def softmax_step(j, slot, cols=slice(None), mask=None):
        rows = kv_rows(j)
        p_all, alpha_all = [], []
        for hd in heads:
            s = s_ref[slot, hd, :, cols]
            if mask is not None:
                s = jnp.where(mask, s, NEG_BIG)
            m_old = m_ref[hd, :, cols]
            m_new = jnp.maximum(m_old, jnp.max(s, axis=0, keepdims=True))
            m_ref[hd, :, cols] = m_new
            alpha_all.append(jnp.exp2(m_old - m_new))
            p_all.append(jnp.exp2((s - m_new).astype(BF16)))
        pv_all = [_dot(vt_ref[hd, :, rows], p_all[hd]) for hd in heads]
        for hd in heads:
            acc_ref[hd, :, cols] = alpha_all[hd] * acc_ref[hd, :, cols] + pv_all[hd]

    def run(q):
        blocks = [(j, slice(None), None) for j in range(N_SUB * q)]
        blocks += [(N_SUB * q + r, slice(2 * HQ * r, 2 * TQ), diag_mask[:, 0:2 * TQ - 2 * HQ * r])
                   for r in range(N_SUB)]
        score_step(blocks[0][0], 0, blocks[0][1])
        for n, (j, cols, mask) in enumerate(blocks):
            if n + 1 < len(blocks):
                score_step(blocks[n + 1][0], (n + 1) % 2, blocks[n + 1][1])
            softmax_step(j, n % 2, cols, mask)

    for q in range(k_ref.shape[2] // TQ):
        pl.when(qi == q)(functools.partial(run, q))

    lam = (jnp.exp(jnp.sum(lq1_ref[...] * lk1_ref[...], axis=-1, keepdims=True))
           - jnp.exp(jnp.sum(lq2_ref[...] * lk2_ref[...], axis=-1, keepdims=True)) + LAM_INIT)
    gain = sw_ref[...] * (1.0 - LAM_INIT)
    for hd in heads:
        o = acc_ref[hd, 0:HEAD_W] / acc_ref[hd, HEAD_W:HEAD_W + 1]
        for grp in range(N_SUB):
            c0 = 2 * grp * HQ
            d = o[:, c0:c0 + HQ] - lam * o[:, c0 + HQ:c0 + 2 * HQ]
            ms = jnp.mean(d * d, axis=0, keepdims=True)
            o_ref[hd, 0, grp * HQ:(grp + 1) * HQ, :] = (
                d * lax.rsqrt(ms + EPS) * gain).T.astype(BF16)


def _diff_attn(proj4, vt, lam_q1, lam_k1, lam_q2, lam_k2, subln_w):
    _, b, s, _ = proj4.shape
    groups = N_HEADS // ATT_HEADS
    const = lambda *_: (0, 0)
    lam_spec = pl.BlockSpec((1, DIFF_HEAD_DIM), const)
    return pl.pallas_call(
        _diff_attn_kernel,
        grid=(b, groups, s // TQ),
        in_specs=[
            pl.BlockSpec((ATT_HEADS, 1, TQ, HEAD_W),
                         lambda bi, gi, qi: (SLOT_DQ * groups + gi, bi, qi, 0)),
            pl.BlockSpec((ATT_HEADS, 1, s, HEAD_W),
                         lambda bi, gi, qi: (SLOT_DK * groups + gi, bi, 0, 0)),
            pl.BlockSpec((ATT_HEADS, V_ROWS, s), lambda bi, gi, qi: (gi, 0, bi)),
            lam_spec, lam_spec, lam_spec, lam_spec,
            pl.BlockSpec((HEAD_W, 1), const),
        ],
        out_specs=pl.BlockSpec((ATT_HEADS, 1, TQ, HEAD_W), lambda bi, gi, qi: (gi, bi, qi, 0)),
        out_shape=jax.ShapeDtypeStruct((N_HEADS, b, s, HEAD_W), BF16),
        scratch_shapes=[
            pltpu.VMEM((ATT_HEADS, 2 * TQ, HEAD_W), BF16),
            pltpu.VMEM((2, ATT_HEADS, TK, 2 * TQ), F32),
            pltpu.VMEM((ATT_HEADS, 1, 2 * TQ), F32),
            pltpu.VMEM((ATT_HEADS, V_ROWS, 2 * TQ), F32),
        ],
        compiler_params=pltpu.CompilerParams(
            dimension_semantics=("parallel", "parallel", "parallel"),
            vmem_limit_bytes=VMEM_LIMIT),
        name="diff_attn",
    )(proj4, proj4, vt, lam_q1, lam_k1, lam_q2, lam_k2, subln_w)


def _hgrn_constants():
    n = HGRN_CHUNK
    t = np.arange(n)[:, None]
    s = np.arange(n)[None, :]
    ltri = (s <= t)
    masks = [(s == t)]
    fine, roles = [], []
    for lvl in range(N_LEVELS):
        hs = n >> (lvl + 1)
        blk = t // (2 * hs)
        mid = blk * 2 * hs + hs - 1
        is_q = (t % (2 * hs)) >= hs
        if lvl >= N_COARSE:
            fine.append(np.where(is_q, (s > mid) & (s <= t), (s > t) & (s <= mid)))
        masks.append((blk == (s // (2 * hs))) & is_q & ((s % (2 * hs)) < hs))
        roles.append(np.broadcast_to(np.where(is_q, 1.0, -1.0), (n, HEAD_W)))
    ltri = ltri.astype(np.float32)
    wfine = np.concatenate(fine, axis=0).astype(np.float32)
    masks = np.stack(masks).astype(np.float32)
    roles = np.stack(roles).astype(np.float32)
    return ltri, wfine, masks, roles


def _hgrn_kernel(q_ref, k_ref, v_ref, gate_ref, g_ref, ltri_ref, wfine_ref, mask_ref, role_ref,
                 nw_ref, o_ref, st_ref):
    n_hd = q_ref.shape[0]
    n_chunks = q_ref.shape[2] // HGRN_CHUNK
    st_ref[...] = jnp.zeros(st_ref.shape, F32)
    ltri = ltri_ref[...]
    wfine = wfine_ref[...]
    nw = nw_ref[...]

    def exponents(g):
        g_hi, g_lo = _split_bf16(g)
        return _dot(ltri, g_hi) + _dot(ltri, g_lo), _dot(wfine, g_hi)

    def decays(b, fine):
        def row_bcast(r, n):
            return jnp.broadcast_to(b[r:r + 1, :], (n, HEAD_W))

        e_b = jnp.exp2(b)
        e_u = jnp.exp2(row_bcast(HGRN_CHUNK - 1, HGRN_CHUNK) - b)
        z = []
        for lvl in range(N_COARSE):
            hs = HGRN_CHUNK >> (lvl + 1)
            b_mid = jnp.concatenate([row_bcast(blk * 2 * hs + hs - 1, 2 * hs)
                                     for blk in range(HGRN_CHUNK // (2 * hs))], axis=0)
            z.append(jnp.exp2((b - b_mid) * role_ref[lvl]))
        fine = jnp.exp2(fine)
        z += [fine[i * HGRN_CHUNK:(i + 1) * HGRN_CHUNK] for i in range(N_LEVELS - N_COARSE)]
        return e_b, e_u, z

    def group(gi, carry):
        base = gi * (HGRN_GROUP * HGRN_CHUNK)
        items = [(hd, pl.ds(pl.multiple_of(base + u * HGRN_CHUNK, HGRN_CHUNK), HGRN_CHUNK))
                 for u in range(HGRN_GROUP) for hd in range(n_hd)]
        q_b = [q_ref[hd, 0, rows, :] for hd, rows in items]
        k_b = [k_ref[hd, 0, rows, :] for hd, rows in items]
        v_b = [v_ref[hd, 0, rows, :] for hd, rows in items]
        expo = [exponents(g_ref[hd, 0, rows, :]) for hd, rows in items]
        dec = [decays(*e) for e in expo]

        def by_role(lvl, q, k):
            if lvl >= N_COARSE:
                return jnp.where(role_ref[lvl] > 0.0, q, k)
            hs = HGRN_CHUNK >> (lvl + 1)
            return jnp.concatenate([(q if part % 2 else k)[part * hs:(part + 1) * hs]
                                    for part in range(HGRN_CHUNK // hs)], axis=0)

        xz, q_dec, k_dec = [], [], []
        for n, (e_b, e_u, z) in enumerate(dec):
            q = q_b[n].astype(F32)
            k = k_b[n].astype(F32)
            xz.append([(by_role(lvl, q, k) * z[lvl]).astype(BF16) for lvl in range(N_LEVELS)])
            q_dec.append((q * e_b).astype(BF16))
            k_dec.append((k * e_u).astype(BF16))

        pair = [[_nt_dot(q_b[n], k_b[n])] + [_nt_dot(x, x) for x in xz[n]]
                for n in range(len(items))]
        kv = [_tn_dot(v_b[n], k_dec[n]) for n in range(len(items))]
        owns = [mask_ref[lvl] > 0.5 for lvl in range(N_LEVELS + 1)]
        att = []
        for p in pair:
            a = jnp.where(owns[0], p[0], 0.0)
            for lvl in range(1, N_LEVELS + 1):
                a = jnp.where(owns[lvl], p[lvl], a)
            att.append(a.astype(BF16))
        o = [_dot(att[n], v_b[n]) for n in range(len(items))]

        st = [st_ref[hd] for hd in range(n_hd)]
        for n, (hd, rows) in enumerate(items):
            o[n] = o[n] + _nt_dot(q_dec[n], st[hd].astype(BF16))
            st[hd] = st[hd] * dec[n][0][HGRN_CHUNK - 1:HGRN_CHUNK, :] + kv[n]
        for hd in range(n_hd):
            st_ref[hd] = st[hd]

        for n, (hd, rows) in enumerate(items):
            ms = jnp.mean(o[n] * o[n], axis=-1, keepdims=True)
            y = o[n] * lax.rsqrt(ms + EPS) * nw * gate_ref[hd, 0, rows, :].astype(F32)
            o_ref[hd, 0, rows, :] = y.astype(BF16)
        return carry

    lax.fori_loop(0, n_chunks // HGRN_GROUP, group, 0, unroll=True)


def _hgrn2(proj4, g4, norm_w):
    _, b, s, _ = proj4.shape
    ltri, wfine, masks, roles = _hgrn_constants()
    heads_spec = lambda sec: pl.BlockSpec((N_HEADS, 1, s, HEAD_W), lambda bi: (sec, bi, 0, 0))
    return pl.pallas_call(
        _hgrn_kernel,
        grid=(b,),
        in_specs=[
            heads_spec(SLOT_HQ), heads_spec(SLOT_HK), heads_spec(SLOT_HI), heads_spec(SLOT_HG),
            heads_spec(0),
            pl.BlockSpec(ltri.shape, lambda *_: (0, 0)),
            pl.BlockSpec(wfine.shape, lambda *_: (0, 0)),
            pl.BlockSpec(masks.shape, lambda *_: (0, 0, 0)),
            pl.BlockSpec(roles.shape, lambda *_: (0, 0, 0)),
            pl.BlockSpec((1, HEAD_W), lambda *_: (0, 0)),
        ],
        out_specs=heads_spec(0),
        out_shape=jax.ShapeDtypeStruct((N_HEADS, b, s, HEAD_W), BF16),
        scratch_shapes=[pltpu.VMEM((N_HEADS, HEAD_W, HEAD_W), F32)],
        compiler_params=pltpu.CompilerParams(
            dimension_semantics=("parallel",), vmem_limit_bytes=VMEM_LIMIT),
        name="hgrn2",
    )(proj4, proj4, proj4, proj4, g4, jnp.asarray(ltri, BF16), jnp.asarray(wfine, BF16),
      jnp.asarray(masks), jnp.asarray(roles), norm_w)


def _ff_chunks():
    chunks, c0 = [], 0
    while c0 < D_FF:
        fc = min(FF_CHUNK, D_FF - c0)
        chunks.append((c0, fc))
        c0 += fc
    return chunks


def _out_ffn_kernel(tiles_per_seq, x_ref, od_ref, oh_ref, wout_ref, ln2_ref, wup_ref, cw_ref,
                    cb_ref, wdn_ref, out_ref, tail_ref, act_ref):
    i = pl.program_id(0)
    mix = jnp.concatenate([od_ref[hd] for hd in range(N_HEADS)]
                          + [oh_ref[hd] for hd in range(N_HEADS)], axis=1)
    x1 = x_ref[...] + _dot(mix, wout_ref[...])
    ms = jnp.mean(x1 * x1, axis=-1, keepdims=True)
    h2 = (x1 * lax.rsqrt(ms + EPS) * ln2_ref[...]).astype(BF16)

    seq_start = (i % tiles_per_seq) == 0
    tm = x1.shape[0]
    for c0, fc in _ff_chunks():
        cols = slice(c0, c0 + fc)
        u = _dot(h2, wup_ref[:, cols])
        v = _dot(h2, wup_ref[:, D_FF + c0:D_FF + c0 + fc])
        tail = jnp.where(seq_start, 0.0, tail_ref[:, cols])
        tail_ref[:, cols] = u[tm - SUBLANES:, :]
        ext = jnp.concatenate([tail, u], axis=0)
        u1 = pltpu.roll(ext, 1, 0)[SUBLANES:]
        u2 = pltpu.roll(ext, 2, 0)[SUBLANES:]
        cw = cw_ref[:, cols]
        c = cb_ref[:, cols] + u2 * cw[0:1] + u1 * cw[1:2] + u * cw[2:3]
        act_ref[:, cols] = (c * _sigmoid(c) * v).astype(BF16)
    out_ref[...] = x1 + _dot(act_ref[...], wdn_ref[...])


def _out_ffn(x2d, o_diff, o_hgrn, w_out, ln2_w, w_up, conv_w, conv_b, w_down, seq_len):
    t = x2d.shape[0]
    const = lambda *_: (0, 0)
    resident = functools.partial(pl.BlockSpec, index_map=const, pipeline_mode=pl.Buffered(1))
    return pl.pallas_call(
        functools.partial(_out_ffn_kernel, seq_len // TM_FFN),
        grid=(t // TM_FFN,),
        in_specs=[
            pl.BlockSpec((TM_FFN, D_MODEL), lambda i: (i, 0)),
            pl.BlockSpec((N_HEADS, TM_FFN, HEAD_W), lambda i: (0, i, 0)),
            pl.BlockSpec((N_HEADS, TM_FFN, HEAD_W), lambda i: (0, i, 0)),
            resident((D_MODEL, D_MODEL)),
            pl.BlockSpec((1, D_MODEL), const),
            resident((D_MODEL, 2 * D_FF)),
            pl.BlockSpec((CONV_WIDTH, D_FF), const),
            pl.BlockSpec((1, D_FF), const),
            resident((D_FF, D_MODEL)),
        ],
        out_specs=pl.BlockSpec((TM_FFN, D_MODEL), lambda i: (i, 0)),
        out_shape=jax.ShapeDtypeStruct((t, D_MODEL), F32),
        scratch_shapes=[
            pltpu.VMEM((SUBLANES, D_FF), F32),
            pltpu.VMEM((TM_FFN, D_FF), BF16),
        ],
        compiler_params=pltpu.CompilerParams(
            dimension_semantics=("arbitrary",), vmem_limit_bytes=VMEM_LIMIT),
        name="out_ffn",
    )(x2d, o_diff, o_hgrn, w_out, ln2_w, w_up, conv_w, conv_b, w_down)


def kernel(x, ln1_w, w_in, q_norm_w, k_norm_w, lam_q1, lam_k1, lam_q2, lam_k2, diff_subln_w,
           hgrn_lb_logits, hgrn_norm_w, w_out, ln2_w, w_up, conv_w, conv_b, w_down):
    b, s, d = x.shape
    depth = ln1_w.shape[0]
    assert depth == 1 and d == D_MODEL and s % TM_FFN == 0 and s % TQ == 0 and N_SUB % 2 == 0
    t = b * s
    x2d = x.reshape(t, d)
    l = 0
    tile2 = lambda w: jnp.concatenate([w, w], axis=-1)[None, :]

    proj, g, vt, (w_out_b, w_up_b, w_down_b) = _in_proj(
        x2d, ln1_w[l][None, :], w_in[l].astype(BF16), tile2(q_norm_w[l]), tile2(k_norm_w[l]),
        hgrn_lb_logits, [w_out[l], w_up[l], w_down[l]])
    proj4 = proj.reshape(N_SLOT * N_HEADS, b, s, HEAD_W)
    g4 = g.reshape(N_HEADS, b, s, HEAD_W)

    o_diff = _diff_attn(proj4, vt, lam_q1[l][None, :], lam_k1[l][None, :], lam_q2[l][None, :],
                        lam_k2[l][None, :], diff_subln_w[l][:, None])
    o_hgrn = _hgrn2(proj4, g4, hgrn_norm_w[l][None, :])

    out = _out_ffn(x2d, o_diff.reshape(N_HEADS, t, HEAD_W), o_hgrn.reshape(N_HEADS, t, HEAD_W),
                   w_out_b, ln2_w[l][None, :], w_up_b, conv_w[l], conv_b[l][None, :], w_down_b, s)
    return out.reshape(b, s, d)
```

```python
import functools
import math

import numpy as np
import jax
import jax.numpy as jnp
from jax import lax
from jax.experimental import pallas as pl
from jax.experimental.pallas import tpu as pltpu

F32 = jnp.float32
BF16 = jnp.bfloat16

D_MODEL = 1024
CHUNK = 64
HEAD_W = 128
DIFF_HEAD_DIM = 64
N_HEADS = 4
SEC_W = N_HEADS * HEAD_W
N_SEC = 7
IN_COLS = N_SEC * SEC_W
D_FF = 2816
CONV_WIDTH = 3
EPS = 1e-6
LAM_INIT = 0.8 - 0.6 * math.exp(-0.3 * 0)
LOG2E = math.log2(math.e)

SEC_DQ, SEC_DK, SEC_DV, SEC_HQ, SEC_HK, SEC_HI, SEC_HG = range(N_SEC)
SLOT_DQ, SLOT_DK, SLOT_HQ, SLOT_HK, SLOT_HI, SLOT_HG = range(6)
N_SLOT = 6
BF16_SUBLANES = 16
V_ROWS = HEAD_W + BF16_SUBLANES

LANES = 128
SUBLANES = 8
MXU_W = 256
TM_PROJ = 1024
ATT_HEADS = 4
TQ = 1024
TK = 256
N_SUB = TQ // TK
HQ = TK
TM_FFN = 512
FF_CHUNK = 768
VMEM_LIMIT = 56 * 1024 * 1024

HGRN_CHUNK = 64
N_LEVELS = 6
N_COARSE = 3
HGRN_GROUP = 2


def _nt_dot(a, b):
    return lax.dot_general(a, b, (((1,), (1,)), ((), ())), preferred_element_type=F32)


def _tn_dot(a, b):
    return lax.dot_general(a, b, (((0,), (0,)), ((), ())), preferred_element_type=F32)


def _dot(a, b):
    return jnp.dot(a, b, preferred_element_type=F32)


def _split_bf16(x):
    hi = x.astype(BF16)
    lo = (x - hi.astype(F32)).astype(BF16)
    return hi, lo


def _sigmoid(x):
    return 1.0 / (1.0 + jnp.exp(-x))


def _in_proj_kernel(n_later, x_ref, ln1_ref, w_ref, qw_ref, kw_ref, lbl_ref, *rest):
    later_f32, (proj_ref, g_ref, vt_ref), later_bf16 = (
        rest[:n_later], rest[n_later:n_later + 3], rest[n_later + 3:])
    for src, dst in zip(later_f32, later_bf16):
        dst[...] = src[...].astype(BF16)

    x = x_ref[...]
    ms = jnp.mean(x * x, axis=-1, keepdims=True)
    h = (x * lax.rsqrt(ms + EPS) * ln1_ref[...]).astype(BF16)

    r = lax.broadcasted_iota(jnp.int32, (MXU_W, MXU_W), 0) // DIFF_HEAD_DIM
    c = lax.broadcasted_iota(jnp.int32, (MXU_W, MXU_W), 1) // DIFF_HEAD_DIM
    grp = jnp.where(r == c, 1.0, 0.0).astype(BF16)

    def head_cols(a, hd):
        return a[:, hd * HEAD_W:(hd + 1) * HEAD_W]

    def store_heads(slot, a):
        for hd in range(N_HEADS):
            proj_ref[slot * N_HEADS + hd] = head_cols(a, hd).astype(BF16)

    def project(j):
        return _dot(h, w_ref[:, j * SEC_W:(j + 1) * SEC_W])

    acc = [None] * N_SEC
    halves = [slice(c0, c0 + MXU_W) for c0 in range(0, SEC_W, MXU_W)]
    ss = {}
    for sec in (SEC_DQ, SEC_DK):
        acc[sec] = project(sec)
        ss[sec] = [_dot((acc[sec][:, cols] * acc[sec][:, cols]).astype(BF16), grp)
                   for cols in halves]
    for sec in (SEC_HK, SEC_HG, SEC_DV, SEC_HQ, SEC_HI):
        acc[sec] = project(sec)

    q_gain = qw_ref[...] * (DIFF_HEAD_DIM ** -0.5 * LOG2E)
    for sec, slot, w_norm in ((SEC_DQ, SLOT_DQ, q_gain), (SEC_DK, SLOT_DK, kw_ref[...])):
        w2 = jnp.concatenate([w_norm] * (MXU_W // HEAD_W), axis=1)
        y = [acc[sec][:, cols] * lax.rsqrt(ss[sec][n] * (1.0 / DIFF_HEAD_DIM) + EPS) * w2
             for n, cols in enumerate(halves)]
        store_heads(slot, jnp.concatenate(y, axis=1))

    for hd in range(N_HEADS):
        vt_ref[hd, 0:HEAD_W, :] = head_cols(acc[SEC_DV], hd).T.astype(BF16)
        vt_ref[hd, HEAD_W:V_ROWS, :] = jnp.ones((V_ROWS - HEAD_W, x.shape[0]), BF16)

    store_heads(SLOT_HQ, acc[SEC_HQ])
    store_heads(SLOT_HI, acc[SEC_HI])

    lbl = lbl_ref[...]
    e = jnp.exp(lbl - jnp.max(lbl, axis=0, keepdims=True))
    lb = e[0:1] / jnp.sum(e, axis=0, keepdims=True)
    f = lb + (1.0 - lb) * _sigmoid(acc[SEC_HK])
    log2_f = jnp.log(f) * LOG2E
    for hd in range(N_HEADS):
        g_ref[hd] = head_cols(log2_f, hd)
    store_heads(SLOT_HK, 1.0 - f)

    store_heads(SLOT_HG, acc[SEC_HG] * _sigmoid(acc[SEC_HG]))


def _in_proj(x2d, ln1_w, w_in, q_norm_w, k_norm_w, lb_logits, later_weights):
    t = x2d.shape[0]
    steps = t // TM_PROJ
    const = lambda *_: (0, 0)
    sliced = [w.reshape(steps, w.shape[0] // steps, w.shape[1]) for w in later_weights]
    slice_specs = [pl.BlockSpec((1,) + w.shape[1:], lambda i: (i, 0, 0)) for w in sliced]
    outs = pl.pallas_call(
        functools.partial(_in_proj_kernel, len(sliced)),
        grid=(steps,),
        in_specs=[
            pl.BlockSpec((TM_PROJ, D_MODEL), lambda i: (i, 0)),
            pl.BlockSpec((1, D_MODEL), const),
            pl.BlockSpec((D_MODEL, IN_COLS), const, pipeline_mode=pl.Buffered(1)),
            pl.BlockSpec((1, HEAD_W), const),
            pl.BlockSpec((1, HEAD_W), const),
            pl.BlockSpec(lb_logits.shape, const),
        ] + slice_specs,
        out_specs=[
            pl.BlockSpec((N_SLOT * N_HEADS, TM_PROJ, HEAD_W), lambda i: (0, i, 0)),
            pl.BlockSpec((N_HEADS, TM_PROJ, HEAD_W), lambda i: (0, i, 0)),
            pl.BlockSpec((N_HEADS, V_ROWS, TM_PROJ), lambda i: (0, 0, i)),
        ] + slice_specs,
        out_shape=[
            jax.ShapeDtypeStruct((N_SLOT * N_HEADS, t, HEAD_W), BF16),
            jax.ShapeDtypeStruct((N_HEADS, t, HEAD_W), F32),
            jax.ShapeDtypeStruct((N_HEADS, V_ROWS, t), BF16),
        ] + [jax.ShapeDtypeStruct(w.shape, BF16) for w in sliced],
        compiler_params=pltpu.CompilerParams(
            dimension_semantics=("parallel",), vmem_limit_bytes=VMEM_LIMIT),
        name="in_proj",
    )(x2d, ln1_w, w_in, q_norm_w, k_norm_w, lb_logits, *sliced)
    rounded = [o.reshape(w.shape) for o, w in zip(outs[3:], later_weights)]
    return outs[0], outs[1], outs[2], rounded


NEG_BIG = -1e30


def _diff_attn_kernel(q_ref, k_ref, vt_ref, lq1_ref, lk1_ref, lq2_ref, lk2_ref, sw_ref,
                      o_ref, qs_ref, s_ref, m_ref, acc_ref):
    qi = pl.program_id(2)
    n_hd = q_ref.shape[0]
    heads = range(n_hd)
    lane = lax.broadcasted_iota(jnp.int32, (HQ, HEAD_W), 1)
    for hd in heads:
        for grp in range(N_SUB):
            q = q_ref[hd, 0, grp * HQ:(grp + 1) * HQ, :]
            zero = jnp.zeros_like(q)
            qs_ref[hd, (2 * grp) * HQ:(2 * grp + 1) * HQ] = jnp.where(lane < DIFF_HEAD_DIM, q, zero)
            qs_ref[hd, (2 * grp + 1) * HQ:(2 * grp + 2) * HQ] = jnp.where(lane >= DIFF_HEAD_DIM, q, zero)

    m_ref[...] = jnp.full(m_ref.shape, NEG_BIG, F32)
    acc_ref[...] = jnp.zeros(acc_ref.shape, F32)

    ck = lax.broadcasted_iota(jnp.int32, (TK, 2 * TQ), 0) // CHUNK
    col = lax.broadcasted_iota(jnp.int32, (TK, 2 * TQ), 1)
    diag_mask = ck <= jnp.where(col >= 2 * HQ, TK // CHUNK, (col % HQ) // CHUNK)

    def kv_rows(j):
        return slice(j * TK, (j + 1) * TK)

    def score_step(j, slot, cols=slice(None)):
        rows = kv_rows(j)
        for hd in heads:
            s_ref[slot, hd, :, cols] = _nt_dot(k_ref[hd, 0, rows, :], qs_ref[hd, cols])

    def softmax_step(j, slot, cols=slice(None), mask=None):
        rows = kv_rows(j)
        p_all, alpha_all = [], []
        for hd in heads:
            s = s_ref[slot, hd, :, cols]
            if mask is not None:
                s = jnp.where(mask, s, NEG_BIG)
            m_old = m_ref[hd, :, cols]
            m_new = jnp.maximum(m_old, jnp.max(s, axis=0, keepdims=True))
            m_ref[hd, :, cols] = m_new
            alpha_all.append(jnp.exp2(m_old - m_new))
            p_all.append(jnp.exp2((s - m_new).astype(BF16)))
        pv_all = [_dot(vt_ref[hd, :, rows], p_all[hd]) for hd in heads]
        for hd in heads:
            acc_ref[hd, :, cols] = alpha_all[hd] * acc_ref[hd, :, cols] + pv_all[hd]

    def run(q):
        blocks = [(j, slice(None), None) for j in range(N_SUB * q)]
        blocks += [(N_SUB * q + r, slice(2 * HQ * r, 2 * TQ), diag_mask[:, 0:2 * TQ - 2 * HQ * r])
                   for r in range(N_SUB)]
        score_step(blocks[0][0], 0, blocks[0][1])
        for n, (j, cols, mask) in enumerate(blocks):
            if n + 1 < len(blocks):
                score_step(blocks[n + 1][0], (n + 1) % 2, blocks[n + 1][1])
            softmax_step(j, n % 2, cols, mask)

    for q in range(k_ref.shape[2] // TQ):
        pl.when(qi == q)(functools.partial(run, q))

    lam = (jnp.exp(jnp.sum(lq1_ref[...] * lk1_ref[...], axis=-1, keepdims=True))
           - jnp.exp(jnp.sum(lq2_ref[...] * lk2_ref[...], axis=-1, keepdims=True)) + LAM_INIT)
    gain = sw_ref[...] * (1.0 - LAM_INIT)
    for hd in heads:
        o = acc_ref[hd, 0:HEAD_W] / acc_ref[hd, HEAD_W:HEAD_W + 1]
        for grp in range(N_SUB):
            c0 = 2 * grp * HQ
            d = o[:, c0:c0 + HQ] - lam * o[:, c0 + HQ:c0 + 2 * HQ]
            ms = jnp.mean(d * d, axis=0, keepdims=True)
            o_ref[hd, 0, grp * HQ:(grp + 1) * HQ, :] = (
                d * lax.rsqrt(ms + EPS) * gain).T.astype(BF16)


def _diff_attn(proj4, vt, lam_q1, lam_k1, lam_q2, lam_k2, subln_w):
    _, b, s, _ = proj4.shape
    groups = N_HEADS // ATT_HEADS
    const = lambda *_: (0, 0)
    lam_spec = pl.BlockSpec((1, DIFF_HEAD_DIM), const)
    return pl.pallas_call(
        _diff_attn_kernel,
        grid=(b, groups, s // TQ),
        in_specs=[
            pl.BlockSpec((ATT_HEADS, 1, TQ, HEAD_W),
                         lambda bi, gi, qi: (SLOT_DQ * groups + gi, bi, qi, 0)),
            pl.BlockSpec((ATT_HEADS, 1, s, HEAD_W),
                         lambda bi, gi, qi: (SLOT_DK * groups + gi, bi, 0, 0)),
            pl.BlockSpec((ATT_HEADS, V_ROWS, s), lambda bi, gi, qi: (gi, 0, bi)),
            lam_spec, lam_spec, lam_spec, lam_spec,
            pl.BlockSpec((HEAD_W, 1), const),
        ],
        out_specs=pl.BlockSpec((ATT_HEADS, 1, TQ, HEAD_W), lambda bi, gi, qi: (gi, bi, qi, 0)),
        out_shape=jax.ShapeDtypeStruct((N_HEADS, b, s, HEAD_W), BF16),
        scratch_shapes=[
            pltpu.VMEM((ATT_HEADS, 2 * TQ, HEAD_W), BF16),
            pltpu.VMEM((2, ATT_HEADS, TK, 2 * TQ), F32),
            pltpu.VMEM((ATT_HEADS, 1, 2 * TQ), F32),
            pltpu.VMEM((ATT_HEADS, V_ROWS, 2 * TQ), F32),
        ],
        compiler_params=pltpu.CompilerParams(
            dimension_semantics=("parallel", "parallel", "parallel"),
            vmem_limit_bytes=VMEM_LIMIT),
        name="diff_attn",
    )(proj4, proj4, vt, lam_q1, lam_k1, lam_q2, lam_k2, subln_w)


def _hgrn_constants():
    n = HGRN_CHUNK
    t = np.arange(n)[:, None]
    s = np.arange(n)[None, :]
    ltri = (s <= t)
    masks = [(s == t)]
    fine, roles = [], []
    for lvl in range(N_LEVELS):
        hs = n >> (lvl + 1)
        blk = t // (2 * hs)
        mid = blk * 2 * hs + hs - 1
        is_q = (t % (2 * hs)) >= hs
        if lvl >= N_COARSE:
            fine.append(np.where(is_q, (s > mid) & (s <= t), (s > t) & (s <= mid)))
        masks.append((blk == (s // (2 * hs))) & is_q & ((s % (2 * hs)) < hs))
        roles.append(np.broadcast_to(np.where(is_q, 1.0, -1.0), (n, HEAD_W)))
    ltri = ltri.astype(np.float32)
    wfine = np.concatenate(fine, axis=0).astype(np.float32)
    masks = np.stack(masks).astype(np.float32)
    roles = np.stack(roles).astype(np.float32)
    return ltri, wfine, masks, roles


def _hgrn_kernel(q_ref, k_ref, v_ref, gate_ref, g_ref, ltri_ref, wfine_ref, mask_ref, role_ref,
                 nw_ref, o_ref, st_ref):
    n_hd = q_ref.shape[0]
    n_chunks = q_ref.shape[2] // HGRN_CHUNK
    st_ref[...] = jnp.zeros(st_ref.shape, F32)
    ltri = ltri_ref[...]
    wfine = wfine_ref[...]
    nw = nw_ref[...]

    def exponents(g):
        g_hi, g_lo = _split_bf16(g)
        return _dot(ltri, g_hi) + _dot(ltri, g_lo), _dot(wfine, g_hi)

    def decays(b, fine):
        def row_bcast(r, n):
            return jnp.broadcast_to(b[r:r + 1, :], (n, HEAD_W))

        e_b = jnp.exp2(b)
        e_u = jnp.exp2(row_bcast(HGRN_CHUNK - 1, HGRN_CHUNK) - b)
        z = []
        for lvl in range(N_COARSE):
            hs = HGRN_CHUNK >> (lvl + 1)
            b_mid = jnp.concatenate([row_bcast(blk * 2 * hs + hs - 1, 2 * hs)
                                     for blk in range(HGRN_CHUNK // (2 * hs))], axis=0)
            z.append(jnp.exp2((b - b_mid) * role_ref[lvl]))
        fine = jnp.exp2(fine)
        z += [fine[i * HGRN_CHUNK:(i + 1) * HGRN_CHUNK] for i in range(N_LEVELS - N_COARSE)]
        return e_b, e_u, z

    def group(gi, carry):
        base = gi * (HGRN_GROUP * HGRN_CHUNK)
        items = [(hd, pl.ds(pl.multiple_of(base + u * HGRN_CHUNK, HGRN_CHUNK), HGRN_CHUNK))
                 for u in range(HGRN_GROUP) for hd in range(n_hd)]
        q_b = [q_ref[hd, 0, rows, :] for hd, rows in items]
        k_b = [k_ref[hd, 0, rows, :] for hd, rows in items]
        v_b = [v_ref[hd, 0, rows, :] for hd, rows in items]
        expo = [exponents(g_ref[hd, 0, rows, :]) for hd, rows in items]
        dec = [decays(*e) for e in expo]

        def by_role(lvl, q, k):
            if lvl >= N_COARSE:
                return jnp.where(role_ref[lvl] > 0.0, q, k)
            hs = HGRN_CHUNK >> (lvl + 1)
            return jnp.concatenate([(q if part % 2 else k)[part * hs:(part + 1) * hs]
                                    for part in range(HGRN_CHUNK // hs)], axis=0)

        xz, q_dec, k_dec = [], [], []
        for n, (e_b, e_u, z) in enumerate(dec):
            q = q_b[n].astype(F32)
            k = k_b[n].astype(F32)
            xz.append([(by_role(lvl, q, k) * z[lvl]).astype(BF16) for lvl in range(N_LEVELS)])
            q_dec.append((q * e_b).astype(BF16))
            k_dec.append((k * e_u).astype(BF16))

        pair = [[_nt_dot(q_b[n], k_b[n])] + [_nt_dot(x, x) for x in xz[n]]
                for n in range(len(items))]
        kv = [_tn_dot(v_b[n], k_dec[n]) for n in range(len(items))]
        owns = [mask_ref[lvl] > 0.5 for lvl in range(N_LEVELS + 1)]
        att = []
        for p in pair:
            a = jnp.where(owns[0], p[0], 0.0)
            for lvl in range(1, N_LEVELS + 1):
                a = jnp.where(owns[lvl], p[lvl], a)
            att.append(a.astype(BF16))
        o = [_dot(att[n], v_b[n]) for n in range(len(items))]

        st = [st_ref[hd] for hd in range(n_hd)]
        for n, (hd, rows) in enumerate(items):
            o[n] = o[n] + _nt_dot(q_dec[n], st[hd].astype(BF16))
            st[hd] = st[hd] * dec[n][0][HGRN_CHUNK - 1:HGRN_CHUNK, :] + kv[n]
        for hd in range(n_hd):
            st_ref[hd] = st[hd]

        for n, (hd, rows) in enumerate(items):
            ms = jnp.mean(o[n] * o[n], axis=-1, keepdims=True)
            y = o[n] * lax.rsqrt(ms + EPS) * nw * gate_ref[hd, 0, rows, :].astype(F32)
            o_ref[hd, 0, rows, :] = y.astype(BF16)
        return carry

    lax.fori_loop(0, n_chunks // HGRN_GROUP, group, 0, unroll=True)


def _hgrn2(proj4, g4, norm_w):
    _, b, s, _ = proj4.shape
    ltri, wfine, masks, roles = _hgrn_constants()
    heads_spec = lambda sec: pl.BlockSpec((N_HEADS, 1, s, HEAD_W), lambda bi: (sec, bi, 0, 0))
    return pl.pallas_call(
        _hgrn_kernel,
        grid=(b,),
        in_specs=[
            heads_spec(SLOT_HQ), heads_spec(SLOT_HK), heads_spec(SLOT_HI), heads_spec(SLOT_HG),
            heads_spec(0),
            pl.BlockSpec(ltri.shape, lambda *_: (0, 0)),
            pl.BlockSpec(wfine.shape, lambda *_: (0, 0)),
            pl.BlockSpec(masks.shape, lambda *_: (0, 0, 0)),
            pl.BlockSpec(roles.shape, lambda *_: (0, 0, 0)),
            pl.BlockSpec((1, HEAD_W), lambda *_: (0, 0)),
        ],
        out_specs=heads_spec(0),
        out_shape=jax.ShapeDtypeStruct((N_HEADS, b, s, HEAD_W), BF16),
        scratch_shapes=[pltpu.VMEM((N_HEADS, HEAD_W, HEAD_W), F32)],
        compiler_params=pltpu.CompilerParams(
            dimension_semantics=("parallel",), vmem_limit_bytes=VMEM_LIMIT),
        name="hgrn2",
    )(proj4, proj4, proj4, proj4, g4, jnp.asarray(ltri, BF16), jnp.asarray(wfine, BF16),
      jnp.asarray(masks), jnp.asarray(roles), norm_w)


def _ff_chunks():
    chunks, c0 = [], 0
    while c0 < D_FF:
        fc = min(FF_CHUNK, D_FF - c0)
        chunks.append((c0, fc))
        c0 += fc
    return chunks


def _out_ffn_kernel(tiles_per_seq, x_ref, od_ref, oh_ref, wout_ref, ln2_ref, wup_ref, cw_ref,
                    cb_ref, wdn_ref, out_ref, tail_ref, act_ref):
    i = pl.program_id(0)
    mix = jnp.concatenate([od_ref[hd] for hd in range(N_HEADS)]
                          + [oh_ref[hd] for hd in range(N_HEADS)], axis=1)
    x1 = x_ref[...] + _dot(mix, wout_ref[...])
    ms = jnp.mean(x1 * x1, axis=-1, keepdims=True)
    h2 = (x1 * lax.rsqrt(ms + EPS) * ln2_ref[...]).astype(BF16)

    seq_start = (i % tiles_per_seq) == 0
    tm = x1.shape[0]
    for c0, fc in _ff_chunks():
        cols = slice(c0, c0 + fc)
        u = _dot(h2, wup_ref[:, cols])
        v = _dot(h2, wup_ref[:, D_FF + c0:D_FF + c0 + fc])
        tail = jnp.where(seq_start, 0.0, tail_ref[:, cols])
        tail_ref[:, cols] = u[tm - SUBLANES:, :]
        ext = jnp.concatenate([tail, u], axis=0)
        u1 = pltpu.roll(ext, 1, 0)[SUBLANES:]
        u2 = pltpu.roll(ext, 2, 0)[SUBLANES:]
        cw = cw_ref[:, cols]
        c = cb_ref[:, cols] + u2 * cw[0:1] + u1 * cw[1:2] + u * cw[2:3]
        act_ref[:, cols] = (c * _sigmoid(c) * v).astype(BF16)
    out_ref[...] = x1 + _dot(act_ref[...], wdn_ref[...])


def _out_ffn(x2d, o_diff, o_hgrn, w_out, ln2_w, w_up, conv_w, conv_b, w_down, seq_len):
    t = x2d.shape[0]
    const = lambda *_: (0, 0)
    resident = functools.partial(pl.BlockSpec, index_map=const, pipeline_mode=pl.Buffered(1))
    return pl.pallas_call(
        functools.partial(_out_ffn_kernel, seq_len // TM_FFN),
        grid=(t // TM_FFN,),
        in_specs=[
            pl.BlockSpec((TM_FFN, D_MODEL), lambda i: (i, 0)),
            pl.BlockSpec((N_HEADS, TM_FFN, HEAD_W), lambda i: (0, i, 0)),
            pl.BlockSpec((N_HEADS, TM_FFN, HEAD_W), lambda i: (0, i, 0)),
            resident((D_MODEL, D_MODEL)),
            pl.BlockSpec((1, D_MODEL), const),
            resident((D_MODEL, 2 * D_FF)),
            pl.BlockSpec((CONV_WIDTH, D_FF), const),
            pl.BlockSpec((1, D_FF), const),
            resident((D_FF, D_MODEL)),
        ],
        out_specs=pl.BlockSpec((TM_FFN, D_MODEL), lambda i: (i, 0)),
        out_shape=jax.ShapeDtypeStruct((t, D_MODEL), F32),
        scratch_shapes=[
            pltpu.VMEM((SUBLANES, D_FF), F32),
            pltpu.VMEM((TM_FFN, D_FF), BF16),
        ],
        compiler_params=pltpu.CompilerParams(
            dimension_semantics=("arbitrary",), vmem_limit_bytes=VMEM_LIMIT),
        name="out_ffn",
    )(x2d, o_diff, o_hgrn, w_out, ln2_w, w_up, conv_w, conv_b, w_down)


def kernel(x, ln1_w, w_in, q_norm_w, k_norm_w, lam_q1, lam_k1, lam_q2, lam_k2, diff_subln_w,
           hgrn_lb_logits, hgrn_norm_w, w_out, ln2_w, w_up, conv_w, conv_b, w_down):
    b, s, d = x.shape
    depth = ln1_w.shape[0]
    assert depth == 1 and d == D_MODEL and s % TM_FFN == 0 and s % TQ == 0 and N_SUB % 2 == 0
    t = b * s
    x2d = x.reshape(t, d)
    l = 0
    tile2 = lambda w: jnp.concatenate([w, w], axis=-1)[None, :]

    proj, g, vt, (w_out_b, w_up_b, w_down_b) = _in_proj(
        x2d, ln1_w[l][None, :], w_in[l].astype(BF16), tile2(q_norm_w[l]), tile2(k_norm_w[l]),
        hgrn_lb_logits, [w_out[l], w_up[l], w_down[l]])
    proj4 = proj.reshape(N_SLOT * N_HEADS, b, s, HEAD_W)
    g4 = g.reshape(N_HEADS, b, s, HEAD_W)

    o_diff = _diff_attn(proj4, vt, lam_q1[l][None, :], lam_k1[l][None, :], lam_q2[l][None, :],
                        lam_k2[l][None, :], diff_subln_w[l][:, None])
    o_hgrn = _hgrn2(proj4, g4, hgrn_norm_w[l][None, :])

    out = _out_ffn(x2d, o_diff.reshape(N_HEADS, t, HEAD_W), o_hgrn.reshape(N_HEADS, t, HEAD_W),
                   w_out_b, ln2_w[l][None, :], w_up_b, conv_w[l], conv_b[l][None, :], w_down_b, s)
    return out.reshape(b, s, d)
```

```python
import functools
import math

import numpy as np
import jax
import jax.numpy as jnp
from jax import lax
from jax.experimental import pallas as pl
from jax.experimental.pallas import tpu as pltpu

F32 = jnp.float32
BF16 = jnp.bfloat16

D_MODEL = 1024
CHUNK = 64
HEAD_W = 128
DIFF_HEAD_DIM = 64
N_HEADS = 4
SEC_W = N_HEADS * HEAD_W
N_SEC = 7
IN_COLS = N_SEC * SEC_W
D_FF = 2816
CONV_WIDTH = 3
EPS = 1e-6
LAM_INIT = 0.8 - 0.6 * math.exp(-0.3 * 0)
LOG2E = math.log2(math.e)

SEC_DQ, SEC_DK, SEC_DV, SEC_HQ, SEC_HK, SEC_HI, SEC_HG = range(N_SEC)
SLOT_DQ, SLOT_DK, SLOT_HQ, SLOT_HK, SLOT_HI, SLOT_HG = range(6)
N_SLOT = 6
BF16_SUBLANES = 16
V_ROWS = HEAD_W + BF16_SUBLANES

LANES = 128
SUBLANES = 8
MXU_W = 256
TM_PROJ = 1024
ATT_HEADS = 4
TQ = 1024
TK = 256
N_SUB = TQ // TK
HQ = TK
TM_FFN = 512
FF_CHUNK = 768
VMEM_LIMIT = 56 * 1024 * 1024

HGRN_CHUNK = 64
N_LEVELS = 6
N_COARSE = 3
HGRN_GROUP = 8


def _nt_dot(a, b):
    return lax.dot_general(a, b, (((1,), (1,)), ((), ())), preferred_element_type=F32)


def _tn_dot(a, b):
    return lax.dot_general(a, b, (((0,), (0,)), ((), ())), preferred_element_type=F32)


def _dot(a, b):
    return jnp.dot(a, b, preferred_element_type=F32)


def _split_bf16(x):
    hi = x.astype(BF16)
    lo = (x - hi.astype(F32)).astype(BF16)
    return hi, lo


def _sigmoid(x):
    return 1.0 / (1.0 + jnp.exp(-x))


def _in_proj_kernel(n_later, x_ref, ln1_ref, w_ref, qw_ref, kw_ref, lbl_ref, *rest):
    later_f32, (proj_ref, g_ref, vt_ref), later_bf16 = (
        rest[:n_later], rest[n_later:n_later + 3], rest[n_later + 3:])
    for src, dst in zip(later_f32, later_bf16):
        dst[...] = src[...].astype(BF16)

    x = x_ref[...]
    ms = jnp.mean(x * x, axis=-1, keepdims=True)
    h = (x * lax.rsqrt(ms + EPS) * ln1_ref[...]).astype(BF16)

    r = lax.broadcasted_iota(jnp.int32, (MXU_W, MXU_W), 0) // DIFF_HEAD_DIM
    c = lax.broadcasted_iota(jnp.int32, (MXU_W, MXU_W), 1) // DIFF_HEAD_DIM
    grp = jnp.where(r == c, 1.0, 0.0).astype(BF16)

    def head_cols(a, hd):
        return a[:, hd * HEAD_W:(hd + 1) * HEAD_W]

    def store_heads(slot, a):
        for hd in range(N_HEADS):
            proj_ref[slot * N_HEADS + hd] = head_cols(a, hd).astype(BF16)

    def project(j):
        return _dot(h, w_ref[:, j * SEC_W:(j + 1) * SEC_W])

    acc = [None] * N_SEC
    halves = [slice(c0, c0 + MXU_W) for c0 in range(0, SEC_W, MXU_W)]
    ss = {}
    for sec in (SEC_DQ, SEC_DK):
        acc[sec] = project(sec)
        ss[sec] = [_dot((acc[sec][:, cols] * acc[sec][:, cols]).astype(BF16), grp)
                   for cols in halves]
    for sec in (SEC_HK, SEC_HG, SEC_DV, SEC_HQ, SEC_HI):
        acc[sec] = project(sec)

    q_gain = qw_ref[...] * (DIFF_HEAD_DIM ** -0.5 * LOG2E)
    for sec, slot, w_norm in ((SEC_DQ, SLOT_DQ, q_gain), (SEC_DK, SLOT_DK, kw_ref[...])):
        w2 = jnp.concatenate([w_norm] * (MXU_W // HEAD_W), axis=1)
        y = [acc[sec][:, cols] * lax.rsqrt(ss[sec][n] * (1.0 / DIFF_HEAD_DIM) + EPS) * w2
             for n, cols in enumerate(halves)]
        store_heads(slot, jnp.concatenate(y, axis=1))

    for hd in range(N_HEADS):
        vt_ref[hd, 0:HEAD_W, :] = head_cols(acc[SEC_DV], hd).T.astype(BF16)
        vt_ref[hd, HEAD_W:V_ROWS, :] = jnp.ones((V_ROWS - HEAD_W, x.shape[0]), BF16)

    store_heads(SLOT_HQ, acc[SEC_HQ])
    store_heads(SLOT_HI, acc[SEC_HI])

    lbl = lbl_ref[...]
    e = jnp.exp(lbl - jnp.max(lbl, axis=0, keepdims=True))
    lb = e[0:1] / jnp.sum(e, axis=0, keepdims=True)
    f = lb + (1.0 - lb) * _sigmoid(acc[SEC_HK])
    log2_f = jnp.log(f) * LOG2E
    for hd in range(N_HEADS):
        g_ref[hd] = head_cols(log2_f, hd)
    store_heads(SLOT_HK, 1.0 - f)

    store_heads(SLOT_HG, acc[SEC_HG] * _sigmoid(acc[SEC_HG]))


def _in_proj(x2d, ln1_w, w_in, q_norm_w, k_norm_w, lb_logits, later_weights):
    t = x2d.shape[0]
    steps = t // TM_PROJ
    const = lambda *_: (0, 0)
    sliced = [w.reshape(steps, w.shape[0] // steps, w.shape[1]) for w in later_weights]
    slice_specs = [pl.BlockSpec((1,) + w.shape[1:], lambda i: (i, 0, 0)) for w in sliced]
    outs = pl.pallas_call(
        functools.partial(_in_proj_kernel, len(sliced)),
        grid=(steps,),
        in_specs=[
            pl.BlockSpec((TM_PROJ, D_MODEL), lambda i: (i, 0)),
            pl.BlockSpec((1, D_MODEL), const),
            pl.BlockSpec((D_MODEL, IN_COLS), const, pipeline_mode=pl.Buffered(1)),
            pl.BlockSpec((1, HEAD_W), const),
            pl.BlockSpec((1, HEAD_W), const),
            pl.BlockSpec(lb_logits.shape, const),
        ] + slice_specs,
        out_specs=[
            pl.BlockSpec((N_SLOT * N_HEADS, TM_PROJ, HEAD_W), lambda i: (0, i, 0)),
            pl.BlockSpec((N_HEADS, TM_PROJ, HEAD_W), lambda i: (0, i, 0)),
            pl.BlockSpec((N_HEADS, V_ROWS, TM_PROJ), lambda i: (0, 0, i)),
        ] + slice_specs,
        out_shape=[
            jax.ShapeDtypeStruct((N_SLOT * N_HEADS, t, HEAD_W), BF16),
            jax.ShapeDtypeStruct((N_HEADS, t, HEAD_W), F32),
            jax.ShapeDtypeStruct((N_HEADS, V_ROWS, t), BF16),
        ] + [jax.ShapeDtypeStruct(w.shape, BF16) for w in sliced],
        compiler_params=pltpu.CompilerParams(
            dimension_semantics=("parallel",), vmem_limit_bytes=VMEM_LIMIT),
        name="in_proj",
    )(x2d, ln1_w, w_in, q_norm_w, k_norm_w, lb_logits, *sliced)
    rounded = [o.reshape(w.shape) for o, w in zip(outs[3:], later_weights)]
    return outs[0], outs[1], outs[2], rounded


NEG_BIG = -1e30


def _diff_attn_kernel(q_ref, k_ref, vt_ref, lq1_ref, lk1_ref, lq2_ref, lk2_ref, sw_ref,
                      o_ref, qs_ref, s_ref, m_ref, acc_ref):
    qi = pl.program_id(2)
    n_hd = q_ref.shape[0]
    heads = range(n_hd)
    lane = lax.broadcasted_iota(jnp.int32, (HQ, HEAD_W), 1)
    for hd in heads:
        for grp in range(N_SUB):
            q = q_ref[hd, 0, grp * HQ:(grp + 1) * HQ, :]
            zero = jnp.zeros_like(q)
            qs_ref[hd, (2 * grp) * HQ:(2 * grp + 1) * HQ] = jnp.where(lane < DIFF_HEAD_DIM, q, zero)
            qs_ref[hd, (2 * grp + 1) * HQ:(2 * grp + 2) * HQ] = jnp.where(lane >= DIFF_HEAD_DIM, q, zero)

    m_ref[...] = jnp.full(m_ref.shape, NEG_BIG, F32)
    acc_ref[...] = jnp.zeros(acc_ref.shape, F32)

    ck = lax.broadcasted_iota(jnp.int32, (TK, 2 * TQ), 0) // CHUNK
    col = lax.broadcasted_iota(jnp.int32, (TK, 2 * TQ), 1)
    diag_mask = ck <= jnp.where(col >= 2 * HQ, TK // CHUNK, (col % HQ) // CHUNK)

    def kv_rows(j):
        return slice(j * TK, (j + 1) * TK)

    def score_step(j, slot, cols=slice(None)):
        rows = kv_rows(j)
        for hd in heads:
            s_ref[slot, hd, :, cols] = _nt_dot(k_ref[hd, 0, rows, :], qs_ref[hd, cols])

    def softmax_step(j, slot, cols=slice(None), mask=None):
        rows = kv_rows(j)
        p_all, alpha_all = [], []
        for hd in heads:
            s = s_ref[slot, hd, :, cols]
            if mask is not None:
                s = jnp.where(mask, s, NEG_BIG)
            m_old = m_ref[hd, :, cols]
            m_new = jnp.maximum(m_old, jnp.max(s, axis=0, keepdims=True))
            m_ref[hd, :, cols] = m_new
            alpha_all.append(jnp.exp2(m_old - m_new))
            p_all.append(jnp.exp2((s - m_new).astype(BF16)))
        pv_all = [_dot(vt_ref[hd, :, rows], p_all[hd]) for hd in heads]
        for hd in heads:
            acc_ref[hd, :, cols] = alpha_all[hd] * acc_ref[hd, :, cols] + pv_all[hd]

    def run(q):
        blocks = [(j, slice(None), None) for j in range(N_SUB * q)]
        blocks += [(N_SUB * q + r, slice(2 * HQ * r, 2 * TQ), diag_mask[:, 0:2 * TQ - 2 * HQ * r])
                   for r in range(N_SUB)]
        score_step(blocks[0][0], 0, blocks[0][1])
        for n, (j, cols, mask) in enumerate(blocks):
            if n + 1 < len(blocks):
                score_step(blocks[n + 1][0], (n + 1) % 2, blocks[n + 1][1])
            softmax_step(j, n % 2, cols, mask)

    for q in range(k_ref.shape[2] // TQ):
        pl.when(qi == q)(functools.partial(run, q))

    lam = (jnp.exp(jnp.sum(lq1_ref[...] * lk1_ref[...], axis=-1, keepdims=True))
           - jnp.exp(jnp.sum(lq2_ref[...] * lk2_ref[...], axis=-1, keepdims=True)) + LAM_INIT)
    gain = sw_ref[...] * (1.0 - LAM_INIT)
    for hd in heads:
        o = acc_ref[hd, 0:HEAD_W] / acc_ref[hd, HEAD_W:HEAD_W + 1]
        for grp in range(N_SUB):
            c0 = 2 * grp * HQ
            d = o[:, c0:c0 + HQ] - lam * o[:, c0 + HQ:c0 + 2 * HQ]
            ms = jnp.mean(d * d, axis=0, keepdims=True)
            o_ref[hd, 0, grp * HQ:(grp + 1) * HQ, :] = (
                d * lax.rsqrt(ms + EPS) * gain).T.astype(BF16)


def _diff_attn(proj4, vt, lam_q1, lam_k1, lam_q2, lam_k2, subln_w):
    _, b, s, _ = proj4.shape
    groups = N_HEADS // ATT_HEADS
    const = lambda *_: (0, 0)
    lam_spec = pl.BlockSpec((1, DIFF_HEAD_DIM), const)
    return pl.pallas_call(
        _diff_attn_kernel,
        grid=(b, groups, s // TQ),
        in_specs=[
            pl.BlockSpec((ATT_HEADS, 1, TQ, HEAD_W),
                         lambda bi, gi, qi: (SLOT_DQ * groups + gi, bi, qi, 0)),
            pl.BlockSpec((ATT_HEADS, 1, s, HEAD_W),
                         lambda bi, gi, qi: (SLOT_DK * groups + gi, bi, 0, 0)),
            pl.BlockSpec((ATT_HEADS, V_ROWS, s), lambda bi, gi, qi: (gi, 0, bi)),
            lam_spec, lam_spec, lam_spec, lam_spec,
            pl.BlockSpec((HEAD_W, 1), const),
        ],
        out_specs=pl.BlockSpec((ATT_HEADS, 1, TQ, HEAD_W), lambda bi, gi, qi: (gi, bi, qi, 0)),
        out_shape=jax.ShapeDtypeStruct((N_HEADS, b, s, HEAD_W), BF16),
        scratch_shapes=[
            pltpu.VMEM((ATT_HEADS, 2 * TQ, HEAD_W), BF16),
            pltpu.VMEM((2, ATT_HEADS, TK, 2 * TQ), F32),
            pltpu.VMEM((ATT_HEADS, 1, 2 * TQ), F32),
            pltpu.VMEM((ATT_HEADS, V_ROWS, 2 * TQ), F32),
        ],
        compiler_params=pltpu.CompilerParams(
            dimension_semantics=("parallel", "parallel", "parallel"),
            vmem_limit_bytes=VMEM_LIMIT),
        name="diff_attn",
    )(proj4, proj4, vt, lam_q1, lam_k1, lam_q2, lam_k2, subln_w)


def _hgrn_constants():
    n = HGRN_CHUNK
    t = np.arange(n)[:, None]
    s = np.arange(n)[None, :]
    ltri = (s <= t)
    masks = [(s == t)]
    fine, roles = [], []
    for lvl in range(N_LEVELS):
        hs = n >> (lvl + 1)
        blk = t // (2 * hs)
        mid = blk * 2 * hs + hs - 1
        is_q = (t % (2 * hs)) >= hs
        if lvl >= N_COARSE:
            fine.append(np.where(is_q, (s > mid) & (s <= t), (s > t) & (s <= mid)))
        masks.append((blk == (s // (2 * hs))) & is_q & ((s % (2 * hs)) < hs))
        roles.append(np.broadcast_to(np.where(is_q, 1.0, -1.0), (n, HEAD_W)))
    ltri = ltri.astype(np.float32)
    wfine = np.concatenate(fine, axis=0).astype(np.float32)
    masks = np.stack(masks).astype(np.float32)
    roles = np.stack(roles).astype(np.float32)
    return ltri, wfine, masks, roles


def _hgrn_kernel(q_ref, k_ref, v_ref, gate_ref, g_ref, ltri_ref, wfine_ref, mask_ref, role_ref,
                 nw_ref, o_ref, st_ref):
    n_hd = q_ref.shape[0]
    n_chunks = q_ref.shape[2] // HGRN_CHUNK
    st_ref[...] = jnp.zeros(st_ref.shape, F32)
    ltri = ltri_ref[...]
    wfine = wfine_ref[...]
    nw = nw_ref[...]

    def exponents(g):
        g_hi, g_lo = _split_bf16(g)
        return _dot(ltri, g_hi) + _dot(ltri, g_lo), _dot(wfine, g_hi)

    def decays(b, fine):
        def row_bcast(r, n):
            return jnp.broadcast_to(b[r:r + 1, :], (n, HEAD_W))

        e_b = jnp.exp2(b)
        e_u = jnp.exp2(row_bcast(HGRN_CHUNK - 1, HGRN_CHUNK) - b)
        z = []
        for lvl in range(N_COARSE):
            hs = HGRN_CHUNK >> (lvl + 1)
            b_mid = jnp.concatenate([row_bcast(blk * 2 * hs + hs - 1, 2 * hs)
                                     for blk in range(HGRN_CHUNK // (2 * hs))], axis=0)
            z.append(jnp.exp2((b - b_mid) * role_ref[lvl]))
        fine = jnp.exp2(fine)
        z += [fine[i * HGRN_CHUNK:(i + 1) * HGRN_CHUNK] for i in range(N_LEVELS - N_COARSE)]
        return e_b, e_u, z

    def group(gi, carry):
        base = gi * (HGRN_GROUP * HGRN_CHUNK)
        items = [(hd, pl.ds(pl.multiple_of(base + u * HGRN_CHUNK, HGRN_CHUNK), HGRN_CHUNK))
                 for u in range(HGRN_GROUP) for hd in range(n_hd)]
        q_b = [q_ref[hd, 0, rows, :] for hd, rows in items]
        k_b = [k_ref[hd, 0, rows, :] for hd, rows in items]
        v_b = [v_ref[hd, 0, rows, :] for hd, rows in items]
        expo = [exponents(g_ref[hd, 0, rows, :]) for hd, rows in items]
        dec = [decays(*e) for e in expo]

        def by_role(lvl, q, k):
            if lvl >= N_COARSE:
                return jnp.where(role_ref[lvl] > 0.0, q, k)
            hs = HGRN_CHUNK >> (lvl + 1)
            return jnp.concatenate([(q if part % 2 else k)[part * hs:(part + 1) * hs]
                                    for part in range(HGRN_CHUNK // hs)], axis=0)

        xz, q_dec, k_dec = [], [], []
        for n, (e_b, e_u, z) in enumerate(dec):
            q = q_b[n].astype(F32)
            k = k_b[n].astype(F32)
            xz.append([(by_role(lvl, q, k) * z[lvl]).astype(BF16) for lvl in range(N_LEVELS)])
            q_dec.append((q * e_b).astype(BF16))
            k_dec.append((k * e_u).astype(BF16))

        pair = [[_nt_dot(q_b[n], k_b[n])] + [_nt_dot(x, x) for x in xz[n]]
                for n in range(len(items))]
        kv = [_tn_dot(v_b[n], k_dec[n]) for n in range(len(items))]
        owns = [mask_ref[lvl] > 0.5 for lvl in range(N_LEVELS + 1)]
        att = []
        for p in pair:
            a = jnp.where(owns[0], p[0], 0.0)
            for lvl in range(1, N_LEVELS + 1):
                a = jnp.where(owns[lvl], p[lvl], a)
            att.append(a.astype(BF16))
        o = [_dot(att[n], v_b[n]) for n in range(len(items))]

        st = [st_ref[hd] for hd in range(n_hd)]
        for n, (hd, rows) in enumerate(items):
            o[n] = o[n] + _nt_dot(q_dec[n], st[hd].astype(BF16))
            st[hd] = st[hd] * dec[n][0][HGRN_CHUNK - 1:HGRN_CHUNK, :] + kv[n]
        for hd in range(n_hd):
            st_ref[hd] = st[hd]

        for n, (hd, rows) in enumerate(items):
            ms = jnp.mean(o[n] * o[n], axis=-1, keepdims=True)
            y = o[n] * lax.rsqrt(ms + EPS) * nw * gate_ref[hd, 0, rows, :].astype(F32)
            o_ref[hd, 0, rows, :] = y.astype(BF16)
        return carry

    lax.fori_loop(0, n_chunks // HGRN_GROUP, group, 0, unroll=True)


def _hgrn2(proj4, g4, norm_w):
    _, b, s, _ = proj4.shape
    ltri, wfine, masks, roles = _hgrn_constants()
    heads_spec = lambda sec: pl.BlockSpec((N_HEADS, 1, s, HEAD_W), lambda bi: (sec, bi, 0, 0))
    return pl.pallas_call(
        _hgrn_kernel,
        grid=(b,),
        in_specs=[
            heads_spec(SLOT_HQ), heads_spec(SLOT_HK), heads_spec(SLOT_HI), heads_spec(SLOT_HG),
            heads_spec(0),
            pl.BlockSpec(ltri.shape, lambda *_: (0, 0)),
            pl.BlockSpec(wfine.shape, lambda *_: (0, 0)),
            pl.BlockSpec(masks.shape, lambda *_: (0, 0, 0)),
            pl.BlockSpec(roles.shape, lambda *_: (0, 0, 0)),
            pl.BlockSpec((1, HEAD_W), lambda *_: (0, 0)),
        ],
        out_specs=heads_spec(0),
        out_shape=jax.ShapeDtypeStruct((N_HEADS, b, s, HEAD_W), BF16),
        scratch_shapes=[pltpu.VMEM((N_HEADS, HEAD_W, HEAD_W), F32)],
        compiler_params=pltpu.CompilerParams(
            dimension_semantics=("parallel",), vmem_limit_bytes=VMEM_LIMIT),
        name="hgrn2",
    )(proj4, proj4, proj4, proj4, g4, jnp.asarray(ltri, BF16), jnp.asarray(wfine, BF16),
      jnp.asarray(masks), jnp.asarray(roles), norm_w)


def _ff_chunks():
    chunks, c0 = [], 0
    while c0 < D_FF:
        fc = min(FF_CHUNK, D_FF - c0)
        chunks.append((c0, fc))
        c0 += fc
    return chunks


def _out_ffn_kernel(tiles_per_seq, x_ref, od_ref, oh_ref, wout_ref, ln2_ref, wup_ref, cw_ref,
                    cb_ref, wdn_ref, out_ref, tail_ref, act_ref):
    i = pl.program_id(0)
    mix = jnp.concatenate([od_ref[hd] for hd in range(N_HEADS)]
                          + [oh_ref[hd] for hd in range(N_HEADS)], axis=1)
    x1 = x_ref[...] + _dot(mix, wout_ref[...])
    ms = jnp.mean(x1 * x1, axis=-1, keepdims=True)
    h2 = (x1 * lax.rsqrt(ms + EPS) * ln2_ref[...]).astype(BF16)

    seq_start = (i % tiles_per_seq) == 0
    tm = x1.shape[0]
    for c0, fc in _ff_chunks():
        cols = slice(c0, c0 + fc)
        u = _dot(h2, wup_ref[:, cols])
        v = _dot(h2, wup_ref[:, D_FF + c0:D_FF + c0 + fc])
        tail = jnp.where(seq_start, 0.0, tail_ref[:, cols])
        tail_ref[:, cols] = u[tm - SUBLANES:, :]
        ext = jnp.concatenate([tail, u], axis=0)
        u1 = pltpu.roll(ext, 1, 0)[SUBLANES:]
        u2 = pltpu.roll(ext, 2, 0)[SUBLANES:]
        cw = cw_ref[:, cols]
        c = cb_ref[:, cols] + u2 * cw[0:1] + u1 * cw[1:2] + u * cw[2:3]
        act_ref[:, cols] = (c * _sigmoid(c) * v).astype(BF16)
    out_ref[...] = x1 + _dot(act_ref[...], wdn_ref[...])


def _out_ffn(x2d, o_diff, o_hgrn, w_out, ln2_w, w_up, conv_w, conv_b, w_down, seq_len):
    t = x2d.shape[0]
    const = lambda *_: (0, 0)
    resident = functools.partial(pl.BlockSpec, index_map=const, pipeline_mode=pl.Buffered(1))
    return pl.pallas_call(
        functools.partial(_out_ffn_kernel, seq_len // TM_FFN),
        grid=(t // TM_FFN,),
        in_specs=[
            pl.BlockSpec((TM_FFN, D_MODEL), lambda i: (i, 0)),
            pl.BlockSpec((N_HEADS, TM_FFN, HEAD_W), lambda i: (0, i, 0)),
            pl.BlockSpec((N_HEADS, TM_FFN, HEAD_W), lambda i: (0, i, 0)),
            resident((D_MODEL, D_MODEL)),
            pl.BlockSpec((1, D_MODEL), const),
            resident((D_MODEL, 2 * D_FF)),
            pl.BlockSpec((CONV_WIDTH, D_FF), const),
            pl.BlockSpec((1, D_FF), const),
            resident((D_FF, D_MODEL)),
        ],
        out_specs=pl.BlockSpec((TM_FFN, D_MODEL), lambda i: (i, 0)),
        out_shape=jax.ShapeDtypeStruct((t, D_MODEL), F32),
        scratch_shapes=[
            pltpu.VMEM((SUBLANES, D_FF), F32),
            pltpu.VMEM((TM_FFN, D_FF), BF16),
        ],
        compiler_params=pltpu.CompilerParams(
            dimension_semantics=("arbitrary",), vmem_limit_bytes=VMEM_LIMIT),
        name="out_ffn",
    )(x2d, o_diff, o_hgrn, w_out, ln2_w, w_up, conv_w, conv_b, w_down)


def kernel(x, ln1_w, w_in, q_norm_w, k_norm_w, lam_q1, lam_k1, lam_q2, lam_k2, diff_subln_w,
           hgrn_lb_logits, hgrn_norm_w, w_out, ln2_w, w_up, conv_w, conv_b, w_down):
    b, s, d = x.shape
    depth = ln1_w.shape[0]
    assert depth == 1 and d == D_MODEL and s % TM_FFN == 0 and s % TQ == 0 and N_SUB % 2 == 0
    t = b * s
    x2d = x.reshape(t, d)
    l = 0
    tile2 = lambda w: jnp.concatenate([w, w], axis=-1)[None, :]

    proj, g, vt, (w_out_b, w_up_b, w_down_b) = _in_proj(
        x2d, ln1_w[l][None, :], w_in[l].astype(BF16), tile2(q_norm_w[l]), tile2(k_norm_w[l]),
        hgrn_lb_logits, [w_out[l], w_up[l], w_down[l]])
    proj4 = proj.reshape(N_SLOT * N_HEADS, b, s, HEAD_W)
    g4 = g.reshape(N_HEADS, b, s, HEAD_W)

    o_diff = _diff_attn(proj4, vt, lam_q1[l][None, :], lam_k1[l][None, :], lam_q2[l][None, :],
                        lam_k2[l][None, :], diff_subln_w[l][:, None])
    o_hgrn = _hgrn2(proj4, g4, hgrn_norm_w[l][None, :])

    out = _out_ffn(x2d, o_diff.reshape(N_HEADS, t, HEAD_W), o_hgrn.reshape(N_HEADS, t, HEAD_W),
                   w_out_b, ln2_w[l][None, :], w_up_b, conv_w[l], conv_b[l][None, :], w_down_b, s)
    return out.reshape(b, s, d)
```

```python
import functools
import math

import numpy as np
import jax
import jax.numpy as jnp
from jax import lax
from jax.experimental import pallas as pl
from jax.experimental.pallas import tpu as pltpu

F32 = jnp.float32
BF16 = jnp.bfloat16

D_MODEL = 1024
CHUNK = 64
HEAD_W = 128
DIFF_HEAD_DIM = 64
N_HEADS = 4
SEC_W = N_HEADS * HEAD_W
N_SEC = 7
IN_COLS = N_SEC * SEC_W
D_FF = 2816
CONV_WIDTH = 3
EPS = 1e-6
LAM_INIT = 0.8 - 0.6 * math.exp(-0.3 * 0)
LOG2E = math.log2(math.e)

SEC_DQ, SEC_DK, SEC_DV, SEC_HQ, SEC_HK, SEC_HI, SEC_HG = range(N_SEC)
SLOT_DQ, SLOT_DK, SLOT_HQ, SLOT_HK, SLOT_HI, SLOT_HG = range(6)
N_SLOT = 6
BF16_SUBLANES = 16
V_ROWS = HEAD_W + BF16_SUBLANES

LANES = 128
SUBLANES = 8
MXU_W = 256
TM_PROJ = 1024
ATT_HEADS = 4
TQ = 1024
TK = 256
N_SUB = TQ // TK
HQ = TK
TM_FFN = 512
FF_CHUNK = 768
VMEM_LIMIT = 56 * 1024 * 1024

HGRN_CHUNK = 64
N_LEVELS = 6
N_COARSE = 3
HGRN_GROUP = 4


def _nt_dot(a, b):
    return lax.dot_general(a, b, (((1,), (1,)), ((), ())), preferred_element_type=F32)


def _tn_dot(a, b):
    return lax.dot_general(a, b, (((0,), (0,)), ((), ())), preferred_element_type=F32)


def _dot(a, b):
    return jnp.dot(a, b, preferred_element_type=F32)


def _split_bf16(x):
    hi = x.astype(BF16)
    lo = (x - hi.astype(F32)).astype(BF16)
    return hi, lo


def _sigmoid(x):
    return 1.0 / (1.0 + jnp.exp(-x))


def _in_proj_kernel(n_later, x_ref, ln1_ref, w_ref, qw_ref, kw_ref, lbl_ref, *rest):
    later_f32, (proj_ref, g_ref, vt_ref), later_bf16 = (
        rest[:n_later], rest[n_later:n_later + 3], rest[n_later + 3:])
    for src, dst in zip(later_f32, later_bf16):
        dst[...] = src[...].astype(BF16)

    x = x_ref[...]
    ms = jnp.mean(x * x, axis=-1, keepdims=True)
    h = (x * lax.rsqrt(ms + EPS) * ln1_ref[...]).astype(BF16)

    r = lax.broadcasted_iota(jnp.int32, (MXU_W, MXU_W), 0) // DIFF_HEAD_DIM
    c = lax.broadcasted_iota(jnp.int32, (MXU_W, MXU_W), 1) // DIFF_HEAD_DIM
    grp = jnp.where(r == c, 1.0, 0.0).astype(BF16)

    def head_cols(a, hd):
        return a[:, hd * HEAD_W:(hd + 1) * HEAD_W]

    def store_heads(slot, a):
        for hd in range(N_HEADS):
            proj_ref[slot * N_HEADS + hd] = head_cols(a, hd).astype(BF16)

    def project(j):
        return _dot(h, w_ref[:, j * SEC_W:(j + 1) * SEC_W])

    acc = [None] * N_SEC
    halves = [slice(c0, c0 + MXU_W) for c0 in range(0, SEC_W, MXU_W)]
    ss = {}
    for sec in (SEC_DQ, SEC_DK):
        acc[sec] = project(sec)
        ss[sec] = [_dot((acc[sec][:, cols] * acc[sec][:, cols]).astype(BF16), grp)
                   for cols in halves]
    for sec in (SEC_HK, SEC_HG, SEC_DV, SEC_HQ, SEC_HI):
        acc[sec] = project(sec)

    q_gain = qw_ref[...] * (DIFF_HEAD_DIM ** -0.5 * LOG2E)
    for sec, slot, w_norm in ((SEC_DQ, SLOT_DQ, q_gain), (SEC_DK, SLOT_DK, kw_ref[...])):
        w2 = jnp.concatenate([w_norm] * (MXU_W // HEAD_W), axis=1)
        y = [acc[sec][:, cols] * lax.rsqrt(ss[sec][n] * (1.0 / DIFF_HEAD_DIM) + EPS) * w2
             for n, cols in enumerate(halves)]
        store_heads(slot, jnp.concatenate(y, axis=1))

    for hd in range(N_HEADS):
        vt_ref[hd, 0:HEAD_W, :] = head_cols(acc[SEC_DV], hd).T.astype(BF16)
        vt_ref[hd, HEAD_W:V_ROWS, :] = jnp.ones((V_ROWS - HEAD_W, x.shape[0]), BF16)

    store_heads(SLOT_HQ, acc[SEC_HQ])
    store_heads(SLOT_HI, acc[SEC_HI])

    lbl = lbl_ref[...]
    e = jnp.exp(lbl - jnp.max(lbl, axis=0, keepdims=True))
    lb = e[0:1] / jnp.sum(e, axis=0, keepdims=True)
    f = lb + (1.0 - lb) * _sigmoid(acc[SEC_HK])
    log2_f = jnp.log(f) * LOG2E
    for hd in range(N_HEADS):
        g_ref[hd] = head_cols(log2_f, hd)
    store_heads(SLOT_HK, 1.0 - f)

    store_heads(SLOT_HG, acc[SEC_HG] * _sigmoid(acc[SEC_HG]))


def _in_proj(x2d, ln1_w, w_in, q_norm_w, k_norm_w, lb_logits, later_weights):
    t = x2d.shape[0]
    steps = t // TM_PROJ
    const = lambda *_: (0, 0)
    sliced = [w.reshape(steps, w.shape[0] // steps, w.shape[1]) for w in later_weights]
    slice_specs = [pl.BlockSpec((1,) + w.shape[1:], lambda i: (i, 0, 0)) for w in sliced]
    outs = pl.pallas_call(
        functools.partial(_in_proj_kernel, len(sliced)),
        grid=(steps,),
        in_specs=[
            pl.BlockSpec((TM_PROJ, D_MODEL), lambda i: (i, 0)),
            pl.BlockSpec((1, D_MODEL), const),
            pl.BlockSpec((D_MODEL, IN_COLS), const, pipeline_mode=pl.Buffered(1)),
            pl.BlockSpec((1, HEAD_W), const),
            pl.BlockSpec((1, HEAD_W), const),
            pl.BlockSpec(lb_logits.shape, const),
        ] + slice_specs,
        out_specs=[
            pl.BlockSpec((N_SLOT * N_HEADS, TM_PROJ, HEAD_W), lambda i: (0, i, 0)),
            pl.BlockSpec((N_HEADS, TM_PROJ, HEAD_W), lambda i: (0, i, 0)),
            pl.BlockSpec((N_HEADS, V_ROWS, TM_PROJ), lambda i: (0, 0, i)),
        ] + slice_specs,
        out_shape=[
            jax.ShapeDtypeStruct((N_SLOT * N_HEADS, t, HEAD_W), BF16),
            jax.ShapeDtypeStruct((N_HEADS, t, HEAD_W), F32),
            jax.ShapeDtypeStruct((N_HEADS, V_ROWS, t), BF16),
        ] + [jax.ShapeDtypeStruct(w.shape, BF16) for w in sliced],
        compiler_params=pltpu.CompilerParams(
            dimension_semantics=("parallel",), vmem_limit_bytes=VMEM_LIMIT),
        name="in_proj",
    )(x2d, ln1_w, w_in, q_norm_w, k_norm_w, lb_logits, *sliced)
    rounded = [o.reshape(w.shape) for o, w in zip(outs[3:], later_weights)]
    return outs[0], outs[1], outs[2], rounded


NEG_BIG = -1e30


def _diff_attn_kernel(q_ref, k_ref, vt_ref, lq1_ref, lk1_ref, lq2_ref, lk2_ref, sw_ref,
                      o_ref, qs_ref, s_ref, m_ref, acc_ref):
    qi = pl.program_id(2)
    n_hd = q_ref.shape[0]
    heads = range(n_hd)
    lane = lax.broadcasted_iota(jnp.int32, (HQ, HEAD_W), 1)
    for hd in heads:
        for grp in range(N_SUB):
            q = q_ref[hd, 0, grp * HQ:(grp + 1) * HQ, :]
            zero = jnp.zeros_like(q)
            qs_ref[hd, (2 * grp) * HQ:(2 * grp + 1) * HQ] = jnp.where(lane < DIFF_HEAD_DIM, q, zero)
            qs_ref[hd, (2 * grp + 1) * HQ:(2 * grp + 2) * HQ] = jnp.where(lane >= DIFF_HEAD_DIM, q, zero)

    ck = lax.broadcasted_iota(jnp.int32, (TK, 2 * TQ), 0) // CHUNK
    col = lax.broadcasted_iota(jnp.int32, (TK, 2 * TQ), 1)
    diag_mask = ck <= jnp.where(col >= 2 * HQ, TK // CHUNK, (col % HQ) // CHUNK)

    def kv_rows(j):
        return slice(j * TK, (j + 1) * TK)

    def score_step(j, slot, cols=slice(None)):
        rows = kv_rows(j)
        for hd in heads:
            s_ref[slot, hd, :, cols] = _nt_dot(k_ref[hd, 0, rows, :], qs_ref[hd, cols])

    def softmax_step(j, slot, cols=slice(None), mask=None, first=False):
        rows = kv_rows(j)
        p_all, alpha_all = [], []
        for hd in heads:
            s = s_ref[slot, hd, :, cols]
            if mask is not None:
                s = jnp.where(mask, s, NEG_BIG)
            m_new = jnp.max(s, axis=0, keepdims=True)
            if not first:
                m_old = m_ref[hd, :, cols]
                m_new = jnp.maximum(m_old, m_new)
                alpha_all.append(jnp.exp2(m_old - m_new))
            m_ref[hd, :, cols] = m_new
            p_all.append(jnp.exp2((s - m_new).astype(BF16)))
        pv_all = [_dot(vt_ref[hd, :, rows], p_all[hd]) for hd in heads]
        for hd in heads:
            if first:
                acc_ref[hd, :, cols] = pv_all[hd]
            else:
                acc_ref[hd, :, cols] = alpha_all[hd] * acc_ref[hd, :, cols] + pv_all[hd]

    def run(q):
        blocks = [(j, slice(None), None) for j in range(N_SUB * q)]
        blocks += [(N_SUB * q + r, slice(2 * HQ * r, 2 * TQ), diag_mask[:, 0:2 * TQ - 2 * HQ * r])
                   for r in range(N_SUB)]
        score_step(blocks[0][0], 0, blocks[0][1])
        for n, (j, cols, mask) in enumerate(blocks):
            if n + 1 < len(blocks):
                score_step(blocks[n + 1][0], (n + 1) % 2, blocks[n + 1][1])
            softmax_step(j, n % 2, cols, mask, first=(n == 0))

    for q in range(k_ref.shape[2] // TQ):
        pl.when(qi == q)(functools.partial(run, q))

    lam = (jnp.exp(jnp.sum(lq1_ref[...] * lk1_ref[...], axis=-1, keepdims=True))
           - jnp.exp(jnp.sum(lq2_ref[...] * lk2_ref[...], axis=-1, keepdims=True)) + LAM_INIT)
    gain = sw_ref[...] * (1.0 - LAM_INIT)
    for hd in heads:
        o = acc_ref[hd, 0:HEAD_W] / acc_ref[hd, HEAD_W:HEAD_W + 1]
        for grp in range(N_SUB):
            c0 = 2 * grp * HQ
            d = o[:, c0:c0 + HQ] - lam * o[:, c0 + HQ:c0 + 2 * HQ]
            ms = jnp.mean(d * d, axis=0, keepdims=True)
            o_ref[hd, 0, grp * HQ:(grp + 1) * HQ, :] = (
                d * lax.rsqrt(ms + EPS) * gain).T.astype(BF16)


def _diff_attn(proj4, vt, lam_q1, lam_k1, lam_q2, lam_k2, subln_w):
    _, b, s, _ = proj4.shape
    groups = N_HEADS // ATT_HEADS
    const = lambda *_: (0, 0)
    lam_spec = pl.BlockSpec((1, DIFF_HEAD_DIM), const)
    return pl.pallas_call(
        _diff_attn_kernel,
        grid=(b, groups, s // TQ),
        in_specs=[
            pl.BlockSpec((ATT_HEADS, 1, TQ, HEAD_W),
                         lambda bi, gi, qi: (SLOT_DQ * groups + gi, bi, qi, 0)),
            pl.BlockSpec((ATT_HEADS, 1, s, HEAD_W),
                         lambda bi, gi, qi: (SLOT_DK * groups + gi, bi, 0, 0)),
            pl.BlockSpec((ATT_HEADS, V_ROWS, s), lambda bi, gi, qi: (gi, 0, bi)),
            lam_spec, lam_spec, lam_spec, lam_spec,
            pl.BlockSpec((HEAD_W, 1), const),
        ],
        out_specs=pl.BlockSpec((ATT_HEADS, 1, TQ, HEAD_W), lambda bi, gi, qi: (gi, bi, qi, 0)),
        out_shape=jax.ShapeDtypeStruct((N_HEADS, b, s, HEAD_W), BF16),
        scratch_shapes=[
            pltpu.VMEM((ATT_HEADS, 2 * TQ, HEAD_W), BF16),
            pltpu.VMEM((2, ATT_HEADS, TK, 2 * TQ), F32),
            pltpu.VMEM((ATT_HEADS, 1, 2 * TQ), F32),
            pltpu.VMEM((ATT_HEADS, V_ROWS, 2 * TQ), F32),
        ],
        compiler_params=pltpu.CompilerParams(
            dimension_semantics=("parallel", "parallel", "parallel"),
            vmem_limit_bytes=VMEM_LIMIT),
        name="diff_attn",
    )(proj4, proj4, vt, lam_q1, lam_k1, lam_q2, lam_k2, subln_w)


def _hgrn_constants():
    n = HGRN_CHUNK
    t = np.arange(n)[:, None]
    s = np.arange(n)[None, :]
    ltri = (s <= t)
    masks = [(s == t)]
    fine, roles = [], []
    for lvl in range(N_LEVELS):
        hs = n >> (lvl + 1)
        blk = t // (2 * hs)
        mid = blk * 2 * hs + hs - 1
        is_q = (t % (2 * hs)) >= hs
        if lvl >= N_COARSE:
            fine.append(np.where(is_q, (s > mid) & (s <= t), (s > t) & (s <= mid)))
        masks.append((blk == (s // (2 * hs))) & is_q & ((s % (2 * hs)) < hs))
        roles.append(np.broadcast_to(np.where(is_q, 1.0, -1.0), (n, HEAD_W)))
    ltri = ltri.astype(np.float32)
    wfine = np.concatenate(fine, axis=0).astype(np.float32)
    masks = np.stack(masks).astype(np.float32)
    roles = np.stack(roles).astype(np.float32)
    return ltri, wfine, masks, roles


def _hgrn_kernel(q_ref, k_ref, v_ref, gate_ref, g_ref, ltri_ref, wfine_ref, mask_ref, role_ref,
                 nw_ref, o_ref, st_ref):
    n_hd = q_ref.shape[0]
    n_chunks = q_ref.shape[2] // HGRN_CHUNK
    st_ref[...] = jnp.zeros(st_ref.shape, F32)
    ltri = ltri_ref[...]
    wfine = wfine_ref[...]
    nw = nw_ref[...]

    def exponents(g):
        g_hi, g_lo = _split_bf16(g)
        return _dot(ltri, g_hi) + _dot(ltri, g_lo), _dot(wfine, g_hi)

    def decays(b, fine):
        def row_bcast(r, n):
            return jnp.broadcast_to(b[r:r + 1, :], (n, HEAD_W))

        e_b = jnp.exp2(b)
        e_u = jnp.exp2(row_bcast(HGRN_CHUNK - 1, HGRN_CHUNK) - b)
        z = []
        for lvl in range(N_COARSE):
            hs = HGRN_CHUNK >> (lvl + 1)
            b_mid = jnp.concatenate([row_bcast(blk * 2 * hs + hs - 1, 2 * hs)
                                     for blk in range(HGRN_CHUNK // (2 * hs))], axis=0)
            z.append(jnp.exp2((b - b_mid) * role_ref[lvl]))
        fine = jnp.exp2(fine)
        z += [fine[i * HGRN_CHUNK:(i + 1) * HGRN_CHUNK] for i in range(N_LEVELS - N_COARSE)]
        return e_b, e_u, z

    def group(gi, carry):
        base = gi * (HGRN_GROUP * HGRN_CHUNK)
        items = [(hd, pl.ds(pl.multiple_of(base + u * HGRN_CHUNK, HGRN_CHUNK), HGRN_CHUNK))
                 for u in range(HGRN_GROUP) for hd in range(n_hd)]
        q_b = [q_ref[hd, 0, rows, :] for hd, rows in items]
        k_b = [k_ref[hd, 0, rows, :] for hd, rows in items]
        v_b = [v_ref[hd, 0, rows, :] for hd, rows in items]
        expo = [exponents(g_ref[hd, 0, rows, :]) for hd, rows in items]
        dec = [decays(*e) for e in expo]

        def by_role(lvl, q, k):
            if lvl >= N_COARSE:
                return jnp.where(role_ref[lvl] > 0.0, q, k)
            hs = HGRN_CHUNK >> (lvl + 1)
            return jnp.concatenate([(q if part % 2 else k)[part * hs:(part + 1) * hs]
                                    for part in range(HGRN_CHUNK // hs)], axis=0)

        xz, q_dec, k_dec = [], [], []
        for n, (e_b, e_u, z) in enumerate(dec):
            q = q_b[n].astype(F32)
            k = k_b[n].astype(F32)
            xz.append([(by_role(lvl, q, k) * z[lvl]).astype(BF16) for lvl in range(N_LEVELS)])
            q_dec.append((q * e_b).astype(BF16))
            k_dec.append((k * e_u).astype(BF16))

        pair = [[_nt_dot(q_b[n], k_b[n])] + [_nt_dot(x, x) for x in xz[n]]
                for n in range(len(items))]
        kv = [_tn_dot(v_b[n], k_dec[n]) for n in range(len(items))]
        owns = [mask_ref[lvl] > 0.5 for lvl in range(N_LEVELS + 1)]
        att = []
        for p in pair:
            a = jnp.where(owns[0], p[0], 0.0)
            for lvl in range(1, N_LEVELS + 1):
                a = jnp.where(owns[lvl], p[lvl], a)
            att.append(a.astype(BF16))
        o = [_dot(att[n], v_b[n]) for n in range(len(items))]

        st = [st_ref[hd] for hd in range(n_hd)]
        for n, (hd, rows) in enumerate(items):
            o[n] = o[n] + _nt_dot(q_dec[n], st[hd].astype(BF16))
            st[hd] = st[hd] * dec[n][0][HGRN_CHUNK - 1:HGRN_CHUNK, :] + kv[n]
        for hd in range(n_hd):
            st_ref[hd] = st[hd]

        for n, (hd, rows) in enumerate(items):
            ms = jnp.mean(o[n] * o[n], axis=-1, keepdims=True)
            y = o[n] * lax.rsqrt(ms + EPS) * nw * gate_ref[hd, 0, rows, :].astype(F32)
            o_ref[hd, 0, rows, :] = y.astype(BF16)
        return carry

    lax.fori_loop(0, n_chunks // HGRN_GROUP, group, 0, unroll=True)


def _hgrn2(proj4, g4, norm_w):
    _, b, s, _ = proj4.shape
    ltri, wfine, masks, roles = _hgrn_constants()
    heads_spec = lambda sec: pl.BlockSpec((N_HEADS, 1, s, HEAD_W), lambda bi: (sec, bi, 0, 0))
    return pl.pallas_call(
        _hgrn_kernel,
        grid=(b,),
        in_specs=[
            heads_spec(SLOT_HQ), heads_spec(SLOT_HK), heads_spec(SLOT_HI), heads_spec(SLOT_HG),
            heads_spec(0),
            pl.BlockSpec(ltri.shape, lambda *_: (0, 0)),
            pl.BlockSpec(wfine.shape, lambda *_: (0, 0)),
            pl.BlockSpec(masks.shape, lambda *_: (0, 0, 0)),
            pl.BlockSpec(roles.shape, lambda *_: (0, 0, 0)),
            pl.BlockSpec((1, HEAD_W), lambda *_: (0, 0)),
        ],
        out_specs=heads_spec(0),
        out_shape=jax.ShapeDtypeStruct((N_HEADS, b, s, HEAD_W), BF16),
        scratch_shapes=[pltpu.VMEM((N_HEADS, HEAD_W, HEAD_W), F32)],
        compiler_params=pltpu.CompilerParams(
            dimension_semantics=("parallel",), vmem_limit_bytes=VMEM_LIMIT),
        name="hgrn2",
    )(proj4, proj4, proj4, proj4, g4, jnp.asarray(ltri, BF16), jnp.asarray(wfine, BF16),
      jnp.asarray(masks), jnp.asarray(roles), norm_w)


def _ff_chunks():
    chunks, c0 = [], 0
    while c0 < D_FF:
        fc = min(FF_CHUNK, D_FF - c0)
        chunks.append((c0, fc))
        c0 += fc
    return chunks


def _out_ffn_kernel(tiles_per_seq, x_ref, od_ref, oh_ref, wout_ref, ln2_ref, wup_ref, cw_ref,
                    cb_ref, wdn_ref, out_ref, tail_ref, act_ref):
    i = pl.program_id(0)
    mix = jnp.concatenate([od_ref[hd] for hd in range(N_HEADS)]
                          + [oh_ref[hd] for hd in range(N_HEADS)], axis=1)
    x1 = x_ref[...] + _dot(mix, wout_ref[...])
    ms = jnp.mean(x1 * x1, axis=-1, keepdims=True)
    h2 = (x1 * lax.rsqrt(ms + EPS) * ln2_ref[...]).astype(BF16)

    seq_start = (i % tiles_per_seq) == 0
    tm = x1.shape[0]
    for c0, fc in _ff_chunks():
        cols = slice(c0, c0 + fc)
        u = _dot(h2, wup_ref[:, cols])
        v = _dot(h2, wup_ref[:, D_FF + c0:D_FF + c0 + fc])
        tail = jnp.where(seq_start, 0.0, tail_ref[:, cols])
        tail_ref[:, cols] = u[tm - SUBLANES:, :]
        ext = jnp.concatenate([tail, u], axis=0)
        u1 = pltpu.roll(ext, 1, 0)[SUBLANES:]
        u2 = pltpu.roll(ext, 2, 0)[SUBLANES:]
        cw = cw_ref[:, cols]
        c = cb_ref[:, cols] + u2 * cw[0:1] + u1 * cw[1:2] + u * cw[2:3]
        act_ref[:, cols] = (c * _sigmoid(c) * v).astype(BF16)
    out_ref[...] = x1 + _dot(act_ref[...], wdn_ref[...])


def _out_ffn(x2d, o_diff, o_hgrn, w_out, ln2_w, w_up, conv_w, conv_b, w_down, seq_len):
    t = x2d.shape[0]
    const = lambda *_: (0, 0)
    resident = functools.partial(pl.BlockSpec, index_map=const, pipeline_mode=pl.Buffered(1))
    return pl.pallas_call(
        functools.partial(_out_ffn_kernel, seq_len // TM_FFN),
        grid=(t // TM_FFN,),
        in_specs=[
            pl.BlockSpec((TM_FFN, D_MODEL), lambda i: (i, 0)),
            pl.BlockSpec((N_HEADS, TM_FFN, HEAD_W), lambda i: (0, i, 0)),
            pl.BlockSpec((N_HEADS, TM_FFN, HEAD_W), lambda i: (0, i, 0)),
            resident((D_MODEL, D_MODEL)),
            pl.BlockSpec((1, D_MODEL), const),
            resident((D_MODEL, 2 * D_FF)),
            pl.BlockSpec((CONV_WIDTH, D_FF), const),
            pl.BlockSpec((1, D_FF), const),
            resident((D_FF, D_MODEL)),
        ],
        out_specs=pl.BlockSpec((TM_FFN, D_MODEL), lambda i: (i, 0)),
        out_shape=jax.ShapeDtypeStruct((t, D_MODEL), F32),
        scratch_shapes=[
            pltpu.VMEM((SUBLANES, D_FF), F32),
            pltpu.VMEM((TM_FFN, D_FF), BF16),
        ],
        compiler_params=pltpu.CompilerParams(
            dimension_semantics=("arbitrary",), vmem_limit_bytes=VMEM_LIMIT),
        name="out_ffn",
    )(x2d, o_diff, o_hgrn, w_out, ln2_w, w_up, conv_w, conv_b, w_down)


def kernel(x, ln1_w, w_in, q_norm_w, k_norm_w, lam_q1, lam_k1, lam_q2, lam_k2, diff_subln_w,
           hgrn_lb_logits, hgrn_norm_w, w_out, ln2_w, w_up, conv_w, conv_b, w_down):
    b, s, d = x.shape
    depth = ln1_w.shape[0]
    assert depth == 1 and d == D_MODEL and s % TM_FFN == 0 and s % TQ == 0 and N_SUB % 2 == 0
    t = b * s
    x2d = x.reshape(t, d)
    l = 0
    tile2 = lambda w: jnp.concatenate([w, w], axis=-1)[None, :]

    proj, g, vt, (w_out_b, w_up_b, w_down_b) = _in_proj(
        x2d, ln1_w[l][None, :], w_in[l].astype(BF16), tile2(q_norm_w[l]), tile2(k_norm_w[l]),
        hgrn_lb_logits, [w_out[l], w_up[l], w_down[l]])
    proj4 = proj.reshape(N_SLOT * N_HEADS, b, s, HEAD_W)
    g4 = g.reshape(N_HEADS, b, s, HEAD_W)

    o_diff = _diff_attn(proj4, vt, lam_q1[l][None, :], lam_k1[l][None, :], lam_q2[l][None, :],
                        lam_k2[l][None, :], diff_subln_w[l][:, None])
    o_hgrn = _hgrn2(proj4, g4, hgrn_norm_w[l][None, :])

    out = _out_ffn(x2d, o_diff.reshape(N_HEADS, t, HEAD_W), o_hgrn.reshape(N_HEADS, t, HEAD_W),
                   w_out_b, ln2_w[l][None, :], w_up_b, conv_w[l], conv_b[l][None, :], w_down_b, s)
    return out.reshape(b, s, d)
```

```python
import functools
import math

import numpy as np
import jax
import jax.numpy as jnp
from jax import lax
from jax.experimental import pallas as pl
from jax.experimental.pallas import tpu as pltpu

F32 = jnp.float32
BF16 = jnp.bfloat16

D_MODEL = 1024
CHUNK = 64
HEAD_W = 128
DIFF_HEAD_DIM = 64
N_HEADS = 4
SEC_W = N_HEADS * HEAD_W
N_SEC = 7
IN_COLS = N_SEC * SEC_W
D_FF = 2816
CONV_WIDTH = 3
EPS = 1e-6
LAM_INIT = 0.8 - 0.6 * math.exp(-0.3 * 0)
LOG2E = math.log2(math.e)

SEC_DQ, SEC_DK, SEC_DV, SEC_HQ, SEC_HK, SEC_HI, SEC_HG = range(N_SEC)
SLOT_DQ, SLOT_DK, SLOT_HQ, SLOT_HK, SLOT_HI, SLOT_HG = range(6)
N_SLOT = 6
BF16_SUBLANES = 16
V_ROWS = HEAD_W + BF16_SUBLANES

LANES = 128
SUBLANES = 8
MXU_W = 256
TM_PROJ = 1024
ATT_HEADS = 4
TQ = 1024
TK = 256
N_SUB = TQ // TK
HQ = TK
TM_FFN = 512
FF_CHUNK = 768
VMEM_LIMIT = 56 * 1024 * 1024

HGRN_CHUNK = 64
N_LEVELS = 6
N_COARSE = 3
HGRN_GROUP = 4


def _nt_dot(a, b):
    return lax.dot_general(a, b, (((1,), (1,)), ((), ())), preferred_element_type=F32)


def _tn_dot(a, b):
    return lax.dot_general(a, b, (((0,), (0,)), ((), ())), preferred_element_type=F32)


def _dot(a, b):
    return jnp.dot(a, b, preferred_element_type=F32)


def _split_bf16(x):
    hi = x.astype(BF16)
    lo = (x - hi.astype(F32)).astype(BF16)
    return hi, lo


def _sigmoid(x):
    return 1.0 / (1.0 + jnp.exp(-x))


def _in_proj_kernel(n_later, x_ref, ln1_ref, w_ref, qw_ref, kw_ref, lbl_ref, *rest):
    later_f32, (proj_ref, g_ref, vt_ref), later_bf16 = (
        rest[:n_later], rest[n_later:n_later + 3], rest[n_later + 3:])
    for src, dst in zip(later_f32, later_bf16):
        dst[...] = src[...].astype(BF16)

    x = x_ref[...]
    ms = jnp.mean(x * x, axis=-1, keepdims=True)
    h = (x * lax.rsqrt(ms + EPS) * ln1_ref[...]).astype(BF16)

    r = lax.broadcasted_iota(jnp.int32, (MXU_W, MXU_W), 0) // DIFF_HEAD_DIM
    c = lax.broadcasted_iota(jnp.int32, (MXU_W, MXU_W), 1) // DIFF_HEAD_DIM
    grp = jnp.where(r == c, 1.0, 0.0).astype(BF16)

    def head_cols(a, hd):
        return a[:, hd * HEAD_W:(hd + 1) * HEAD_W]

    def store_heads(slot, a):
        for hd in range(N_HEADS):
            proj_ref[slot * N_HEADS + hd] = head_cols(a, hd).astype(BF16)

    def project(j):
        return _dot(h, w_ref[:, j * SEC_W:(j + 1) * SEC_W])

    acc = [None] * N_SEC
    halves = [slice(c0, c0 + MXU_W) for c0 in range(0, SEC_W, MXU_W)]
    ss = {}
    for sec in (SEC_DQ, SEC_DK):
        acc[sec] = project(sec)
        ss[sec] = [_dot((acc[sec][:, cols] * acc[sec][:, cols]).astype(BF16), grp)
                   for cols in halves]
    for sec in (SEC_HK, SEC_HG, SEC_DV, SEC_HQ, SEC_HI):
        acc[sec] = project(sec)

    q_gain = qw_ref[...] * (DIFF_HEAD_DIM ** -0.5 * LOG2E)
    for sec, slot, w_norm in ((SEC_DQ, SLOT_DQ, q_gain), (SEC_DK, SLOT_DK, kw_ref[...])):
        w2 = jnp.concatenate([w_norm] * (MXU_W // HEAD_W), axis=1)
        y = [acc[sec][:, cols] * lax.rsqrt(ss[sec][n] * (1.0 / DIFF_HEAD_DIM) + EPS) * w2
             for n, cols in enumerate(halves)]
        store_heads(slot, jnp.concatenate(y, axis=1))

    for hd in range(N_HEADS):
        vt_ref[hd, 0:HEAD_W, :] = head_cols(acc[SEC_DV], hd).T.astype(BF16)
        vt_ref[hd, HEAD_W:V_ROWS, :] = jnp.ones((V_ROWS - HEAD_W, x.shape[0]), BF16)

    store_heads(SLOT_HQ, acc[SEC_HQ])
    store_heads(SLOT_HI, acc[SEC_HI])

    lbl = lbl_ref[...]
    e = jnp.exp(lbl - jnp.max(lbl, axis=0, keepdims=True))
    lb = e[0:1] / jnp.sum(e, axis=0, keepdims=True)
    f = lb + (1.0 - lb) * _sigmoid(acc[SEC_HK])
    log2_f = jnp.log(f) * LOG2E
    for hd in range(N_HEADS):
        g_ref[hd] = head_cols(log2_f, hd)
    store_heads(SLOT_HK, 1.0 - f)

    store_heads(SLOT_HG, acc[SEC_HG] * _sigmoid(acc[SEC_HG]))


def _in_proj(x2d, ln1_w, w_in, q_norm_w, k_norm_w, lb_logits, later_weights):
    t = x2d.shape[0]
    steps = t // TM_PROJ
    const = lambda *_: (0, 0)
    sliced = [w.reshape(steps, w.shape[0] // steps, w.shape[1]) for w in later_weights]
    slice_specs = [pl.BlockSpec((1,) + w.shape[1:], lambda i: (i, 0, 0)) for w in sliced]
    outs = pl.pallas_call(
        functools.partial(_in_proj_kernel, len(sliced)),
        grid=(steps,),
        in_specs=[
            pl.BlockSpec((TM_PROJ, D_MODEL), lambda i: (i, 0)),
            pl.BlockSpec((1, D_MODEL), const),
            pl.BlockSpec((D_MODEL, IN_COLS), const, pipeline_mode=pl.Buffered(1)),
            pl.BlockSpec((1, HEAD_W), const),
            pl.BlockSpec((1, HEAD_W), const),
            pl.BlockSpec(lb_logits.shape, const),
        ] + slice_specs,
        out_specs=[
            pl.BlockSpec((N_SLOT * N_HEADS, TM_PROJ, HEAD_W), lambda i: (0, i, 0)),
            pl.BlockSpec((N_HEADS, TM_PROJ, HEAD_W), lambda i: (0, i, 0)),
            pl.BlockSpec((N_HEADS, V_ROWS, TM_PROJ), lambda i: (0, 0, i)),
        ] + slice_specs,
        out_shape=[
            jax.ShapeDtypeStruct((N_SLOT * N_HEADS, t, HEAD_W), BF16),
            jax.ShapeDtypeStruct((N_HEADS, t, HEAD_W), F32),
            jax.ShapeDtypeStruct((N_HEADS, V_ROWS, t), BF16),
        ] + [jax.ShapeDtypeStruct(w.shape, BF16) for w in sliced],
        compiler_params=pltpu.CompilerParams(
            dimension_semantics=("parallel",), vmem_limit_bytes=VMEM_LIMIT),
        name="in_proj",
    )(x2d, ln1_w, w_in, q_norm_w, k_norm_w, lb_logits, *sliced)
    rounded = [o.reshape(w.shape) for o, w in zip(outs[3:], later_weights)]
    return outs[0], outs[1], outs[2], rounded


NEG_BIG = -1e30


def _diff_attn_kernel(q_ref, k_ref, vt_ref, lq1_ref, lk1_ref, lq2_ref, lk2_ref, sw_ref,
                      o_ref, qs_ref, s_ref, smax_ref, m_ref, acc_ref):
    qi = pl.program_id(2)
    n_hd = q_ref.shape[0]
    heads = range(n_hd)
    lane = lax.broadcasted_iota(jnp.int32, (HQ, HEAD_W), 1)
    for hd in heads:
        for grp in range(N_SUB):
            q = q_ref[hd, 0, grp * HQ:(grp + 1) * HQ, :]
            zero = jnp.zeros_like(q)
            qs_ref[hd, (2 * grp) * HQ:(2 * grp + 1) * HQ] = jnp.where(lane < DIFF_HEAD_DIM, q, zero)
            qs_ref[hd, (2 * grp + 1) * HQ:(2 * grp + 2) * HQ] = jnp.where(lane >= DIFF_HEAD_DIM, q, zero)

    ck = lax.broadcasted_iota(jnp.int32, (TK, 2 * TQ), 0) // CHUNK
    col = lax.broadcasted_iota(jnp.int32, (TK, 2 * TQ), 1)
    diag_mask = ck <= jnp.where(col >= 2 * HQ, TK // CHUNK, (col % HQ) // CHUNK)

    def kv_rows(j):
        return slice(j * TK, (j + 1) * TK)

    def score_step(j, slot, cols=slice(None), mask=None):
        rows = kv_rows(j)
        for hd in heads:
            s = _nt_dot(k_ref[hd, 0, rows, :], qs_ref[hd, cols])
            if mask is not None:
                s = jnp.where(mask, s, NEG_BIG)
            s_ref[slot, hd, :, cols] = s
            smax_ref[slot, hd, :, cols] = jnp.max(s, axis=0, keepdims=True)

    def softmax_step(j, slot, cols=slice(None), first=False):
        rows = kv_rows(j)
        p_all, alpha_all = [], []
        for hd in heads:
            s = s_ref[slot, hd, :, cols]
            m_new = smax_ref[slot, hd, :, cols]
            if not first:
                m_old = m_ref[hd, :, cols]
                m_new = jnp.maximum(m_old, m_new)
                alpha_all.append(jnp.exp2(m_old - m_new))
            m_ref[hd, :, cols] = m_new
            p_all.append(jnp.exp2((s - m_new).astype(BF16)))
        pv_all = [_dot(vt_ref[hd, :, rows], p_all[hd]) for hd in heads]
        for hd in heads:
            if first:
                acc_ref[hd, :, cols] = pv_all[hd]
            else:
                acc_ref[hd, :, cols] = alpha_all[hd] * acc_ref[hd, :, cols] + pv_all[hd]

    def run(q):
        blocks = [(j, slice(None), None) for j in range(N_SUB * q)]
        blocks += [(N_SUB * q + r, slice(2 * HQ * r, 2 * TQ), diag_mask[:, 0:2 * TQ - 2 * HQ * r])
                   for r in range(N_SUB)]
        score_step(blocks[0][0], 0, blocks[0][1], blocks[0][2])
        for n, (j, cols, mask) in enumerate(blocks):
            if n + 1 < len(blocks):
                nxt = blocks[n + 1]
                score_step(nxt[0], (n + 1) % 2, nxt[1], nxt[2])
            softmax_step(j, n % 2, cols, first=(n == 0))

    for q in range(k_ref.shape[2] // TQ):
        pl.when(qi == q)(functools.partial(run, q))

    lam = (jnp.exp(jnp.sum(lq1_ref[...] * lk1_ref[...], axis=-1, keepdims=True))
           - jnp.exp(jnp.sum(lq2_ref[...] * lk2_ref[...], axis=-1, keepdims=True)) + LAM_INIT)
    gain = sw_ref[...] * (1.0 - LAM_INIT)
    for hd in heads:
        o = acc_ref[hd, 0:HEAD_W] / acc_ref[hd, HEAD_W:HEAD_W + 1]
        for grp in range(N_SUB):
            c0 = 2 * grp * HQ
            d = o[:, c0:c0 + HQ] - lam * o[:, c0 + HQ:c0 + 2 * HQ]
            ms = jnp.mean(d * d, axis=0, keepdims=True)
            o_ref[hd, 0, grp * HQ:(grp + 1) * HQ, :] = (
                d * lax.rsqrt(ms + EPS) * gain).T.astype(BF16)


def _diff_attn(proj4, vt, lam_q1, lam_k1, lam_q2, lam_k2, subln_w):
    _, b, s, _ = proj4.shape
    groups = N_HEADS // ATT_HEADS
    const = lambda *_: (0, 0)
    lam_spec = pl.BlockSpec((1, DIFF_HEAD_DIM), const)
    return pl.pallas_call(
        _diff_attn_kernel,
        grid=(b, groups, s // TQ),
        in_specs=[
            pl.BlockSpec((ATT_HEADS, 1, TQ, HEAD_W),
                         lambda bi, gi, qi: (SLOT_DQ * groups + gi, bi, qi, 0)),
            pl.BlockSpec((ATT_HEADS, 1, s, HEAD_W),
                         lambda bi, gi, qi: (SLOT_DK * groups + gi, bi, 0, 0)),
            pl.BlockSpec((ATT_HEADS, V_ROWS, s), lambda bi, gi, qi: (gi, 0, bi)),
            lam_spec, lam_spec, lam_spec, lam_spec,
            pl.BlockSpec((HEAD_W, 1), const),
        ],
        out_specs=pl.BlockSpec((ATT_HEADS, 1, TQ, HEAD_W), lambda bi, gi, qi: (gi, bi, qi, 0)),
        out_shape=jax.ShapeDtypeStruct((N_HEADS, b, s, HEAD_W), BF16),
        scratch_shapes=[
            pltpu.VMEM((ATT_HEADS, 2 * TQ, HEAD_W), BF16),
            pltpu.VMEM((2, ATT_HEADS, TK, 2 * TQ), F32),
            pltpu.VMEM((2, ATT_HEADS, 1, 2 * TQ), F32),
            pltpu.VMEM((ATT_HEADS, 1, 2 * TQ), F32),
            pltpu.VMEM((ATT_HEADS, V_ROWS, 2 * TQ), F32),
        ],
        compiler_params=pltpu.CompilerParams(
            dimension_semantics=("parallel", "parallel", "parallel"),
            vmem_limit_bytes=VMEM_LIMIT),
        name="diff_attn",
    )(proj4, proj4, vt, lam_q1, lam_k1, lam_q2, lam_k2, subln_w)


def _hgrn_constants():
    n = HGRN_CHUNK
    t = np.arange(n)[:, None]
    s = np.arange(n)[None, :]
    ltri = (s <= t)
    masks = [(s == t)]
    fine, roles = [], []
    for lvl in range(N_LEVELS):
        hs = n >> (lvl + 1)
        blk = t // (2 * hs)
        mid = blk * 2 * hs + hs - 1
        is_q = (t % (2 * hs)) >= hs
        if lvl >= N_COARSE:
            fine.append(np.where(is_q, (s > mid) & (s <= t), (s > t) & (s <= mid)))
        masks.append((blk == (s // (2 * hs))) & is_q & ((s % (2 * hs)) < hs))
        roles.append(np.broadcast_to(np.where(is_q, 1.0, -1.0), (n, HEAD_W)))
    ltri = ltri.astype(np.float32)
    wfine = np.concatenate(fine, axis=0).astype(np.float32)
    masks = np.stack(masks).astype(np.float32)
    roles = np.stack(roles).astype(np.float32)
    return ltri, wfine, masks, roles


def _hgrn_kernel(q_ref, k_ref, v_ref, gate_ref, g_ref, ltri_ref, wfine_ref, mask_ref, role_ref,
                 nw_ref, o_ref, st_ref):
    n_hd = q_ref.shape[0]
    n_chunks = q_ref.shape[2] // HGRN_CHUNK
    st_ref[...] = jnp.zeros(st_ref.shape, F32)
    ltri = ltri_ref[...]
    wfine = wfine_ref[...]
    nw = nw_ref[...]

    def exponents(g):
        g_hi, g_lo = _split_bf16(g)
        return _dot(ltri, g_hi) + _dot(ltri, g_lo), _dot(wfine, g_hi)

    def decays(b, fine):
        def row_bcast(r, n):
            return jnp.broadcast_to(b[r:r + 1, :], (n, HEAD_W))

        e_b = jnp.exp2(b)
        e_u = jnp.exp2(row_bcast(HGRN_CHUNK - 1, HGRN_CHUNK) - b)
        z = []
        for lvl in range(N_COARSE):
            hs = HGRN_CHUNK >> (lvl + 1)
            b_mid = jnp.concatenate([row_bcast(blk * 2 * hs + hs - 1, 2 * hs)
                                     for blk in range(HGRN_CHUNK // (2 * hs))], axis=0)
            z.append(jnp.exp2((b - b_mid) * role_ref[lvl]))
        fine = jnp.exp2(fine)
        z += [fine[i * HGRN_CHUNK:(i + 1) * HGRN_CHUNK] for i in range(N_LEVELS - N_COARSE)]
        return e_b, e_u, z

    def group(gi, carry):
        base = gi * (HGRN_GROUP * HGRN_CHUNK)
        items = [(hd, pl.ds(pl.multiple_of(base + u * HGRN_CHUNK, HGRN_CHUNK), HGRN_CHUNK))
                 for u in range(HGRN_GROUP) for hd in range(n_hd)]
        q_b = [q_ref[hd, 0, rows, :] for hd, rows in items]
        k_b = [k_ref[hd, 0, rows, :] for hd, rows in items]
        v_b = [v_ref[hd, 0, rows, :] for hd, rows in items]
        expo = [exponents(g_ref[hd, 0, rows, :]) for hd, rows in items]
        dec = [decays(*e) for e in expo]

        def by_role(lvl, q, k):
            if lvl >= N_COARSE:
                return jnp.where(role_ref[lvl] > 0.0, q, k)
            hs = HGRN_CHUNK >> (lvl + 1)
            return jnp.concatenate([(q if part % 2 else k)[part * hs:(part + 1) * hs]
                                    for part in range(HGRN_CHUNK // hs)], axis=0)

        xz, q_dec, k_dec = [], [], []
        for n, (e_b, e_u, z) in enumerate(dec):
            q = q_b[n].astype(F32)
            k = k_b[n].astype(F32)
            xz.append([(by_role(lvl, q, k) * z[lvl]).astype(BF16) for lvl in range(N_LEVELS)])
            q_dec.append((q * e_b).astype(BF16))
            k_dec.append((k * e_u).astype(BF16))

        pair = [[_nt_dot(q_b[n], k_b[n])] + [_nt_dot(x, x) for x in xz[n]]
                for n in range(len(items))]
        kv = [_tn_dot(v_b[n], k_dec[n]) for n in range(len(items))]
        owns = [mask_ref[lvl] > 0.5 for lvl in range(N_LEVELS + 1)]
        att = []
        for p in pair:
            a = jnp.where(owns[0], p[0], 0.0)
            for lvl in range(1, N_LEVELS + 1):
                a = jnp.where(owns[lvl], p[lvl], a)
            att.append(a.astype(BF16))
        o = [_dot(att[n], v_b[n]) for n in range(len(items))]

        st = [st_ref[hd] for hd in range(n_hd)]
        for n, (hd, rows) in enumerate(items):
            o[n] = o[n] + _nt_dot(q_dec[n], st[hd].astype(BF16))
            st[hd] = st[hd] * dec[n][0][HGRN_CHUNK - 1:HGRN_CHUNK, :] + kv[n]
        for hd in range(n_hd):
            st_ref[hd] = st[hd]

        for n, (hd, rows) in enumerate(items):
            ms = jnp.mean(o[n] * o[n], axis=-1, keepdims=True)
            y = o[n] * lax.rsqrt(ms + EPS) * nw * gate_ref[hd, 0, rows, :].astype(F32)
            o_ref[hd, 0, rows, :] = y.astype(BF16)
        return carry

    lax.fori_loop(0, n_chunks // HGRN_GROUP, group, 0, unroll=True)


def _hgrn2(proj4, g4, norm_w):
    _, b, s, _ = proj4.shape
    ltri, wfine, masks, roles = _hgrn_constants()
    heads_spec = lambda sec: pl.BlockSpec((N_HEADS, 1, s, HEAD_W), lambda bi: (sec, bi, 0, 0))
    return pl.pallas_call(
        _hgrn_kernel,
        grid=(b,),
        in_specs=[
            heads_spec(SLOT_HQ), heads_spec(SLOT_HK), heads_spec(SLOT_HI), heads_spec(SLOT_HG),
            heads_spec(0),
            pl.BlockSpec(ltri.shape, lambda *_: (0, 0)),
            pl.BlockSpec(wfine.shape, lambda *_: (0, 0)),
            pl.BlockSpec(masks.shape, lambda *_: (0, 0, 0)),
            pl.BlockSpec(roles.shape, lambda *_: (0, 0, 0)),
            pl.BlockSpec((1, HEAD_W), lambda *_: (0, 0)),
        ],
        out_specs=heads_spec(0),
        out_shape=jax.ShapeDtypeStruct((N_HEADS, b, s, HEAD_W), BF16),
        scratch_shapes=[pltpu.VMEM((N_HEADS, HEAD_W, HEAD_W), F32)],
        compiler_params=pltpu.CompilerParams(
            dimension_semantics=("parallel",), vmem_limit_bytes=VMEM_LIMIT),
        name="hgrn2",
    )(proj4, proj4, proj4, proj4, g4, jnp.asarray(ltri, BF16), jnp.asarray(wfine, BF16),
      jnp.asarray(masks), jnp.asarray(roles), norm_w)


def _ff_chunks():
    chunks, c0 = [], 0
    while c0 < D_FF:
        fc = min(FF_CHUNK, D_FF - c0)
        chunks.append((c0, fc))
        c0 += fc
    return chunks


def _out_ffn_kernel(tiles_per_seq, x_ref, od_ref, oh_ref, wout_ref, ln2_ref, wup_ref, cw_ref,
                    cb_ref, wdn_ref, out_ref, tail_ref, act_ref):
    i = pl.program_id(0)
    mix = jnp.concatenate([od_ref[hd] for hd in range(N_HEADS)]
                          + [oh_ref[hd] for hd in range(N_HEADS)], axis=1)
    x1 = x_ref[...] + _dot(mix, wout_ref[...])
    ms = jnp.mean(x1 * x1, axis=-1, keepdims=True)
    h2 = (x1 * lax.rsqrt(ms + EPS) * ln2_ref[...]).astype(BF16)

    seq_start = (i % tiles_per_seq) == 0
    tm = x1.shape[0]
    for c0, fc in _ff_chunks():
        cols = slice(c0, c0 + fc)
        u = _dot(h2, wup_ref[:, cols])
        v = _dot(h2, wup_ref[:, D_FF + c0:D_FF + c0 + fc])
        tail = jnp.where(seq_start, 0.0, tail_ref[:, cols])
        tail_ref[:, cols] = u[tm - SUBLANES:, :]
        ext = jnp.concatenate([tail, u], axis=0)
        u1 = pltpu.roll(ext, 1, 0)[SUBLANES:]
        u2 = pltpu.roll(ext, 2, 0)[SUBLANES:]
        cw = cw_ref[:, cols]
        c = cb_ref[:, cols] + u2 * cw[0:1] + u1 * cw[1:2] + u * cw[2:3]
        act_ref[:, cols] = (c * _sigmoid(c) * v).astype(BF16)
    out_ref[...] = x1 + _dot(act_ref[...], wdn_ref[...])


def _out_ffn(x2d, o_diff, o_hgrn, w_out, ln2_w, w_up, conv_w, conv_b, w_down, seq_len):
    t = x2d.shape[0]
    const = lambda *_: (0, 0)
    resident = functools.partial(pl.BlockSpec, index_map=const, pipeline_mode=pl.Buffered(1))
    return pl.pallas_call(
        functools.partial(_out_ffn_kernel, seq_len // TM_FFN),
        grid=(t // TM_FFN,),
        in_specs=[
            pl.BlockSpec((TM_FFN, D_MODEL), lambda i: (i, 0)),
            pl.BlockSpec((N_HEADS, TM_FFN, HEAD_W), lambda i: (0, i, 0)),
            pl.BlockSpec((N_HEADS, TM_FFN, HEAD_W), lambda i: (0, i, 0)),
            resident((D_MODEL, D_MODEL)),
            pl.BlockSpec((1, D_MODEL), const),
            resident((D_MODEL, 2 * D_FF)),
            pl.BlockSpec((CONV_WIDTH, D_FF), const),
            pl.BlockSpec((1, D_FF), const),
            resident((D_FF, D_MODEL)),
        ],
        out_specs=pl.BlockSpec((TM_FFN, D_MODEL), lambda i: (i, 0)),
        out_shape=jax.ShapeDtypeStruct((t, D_MODEL), F32),
        scratch_shapes=[
            pltpu.VMEM((SUBLANES, D_FF), F32),
            pltpu.VMEM((TM_FFN, D_FF), BF16),
        ],
        compiler_params=pltpu.CompilerParams(
            dimension_semantics=("arbitrary",), vmem_limit_bytes=VMEM_LIMIT),
        name="out_ffn",
    )(x2d, o_diff, o_hgrn, w_out, ln2_w, w_up, conv_w, conv_b, w_down)


def kernel(x, ln1_w, w_in, q_norm_w, k_norm_w, lam_q1, lam_k1, lam_q2, lam_k2, diff_subln_w,
           hgrn_lb_logits, hgrn_norm_w, w_out, ln2_w, w_up, conv_w, conv_b, w_down):
    b, s, d = x.shape
    depth = ln1_w.shape[0]
    assert depth == 1 and d == D_MODEL and s % TM_FFN == 0 and s % TQ == 0 and N_SUB % 2 == 0
    t = b * s
    x2d = x.reshape(t, d)
    l = 0
    tile2 = lambda w: jnp.concatenate([w, w], axis=-1)[None, :]

    proj, g, vt, (w_out_b, w_up_b, w_down_b) = _in_proj(
        x2d, ln1_w[l][None, :], w_in[l].astype(BF16), tile2(q_norm_w[l]), tile2(k_norm_w[l]),
        hgrn_lb_logits, [w_out[l], w_up[l], w_down[l]])
    proj4 = proj.reshape(N_SLOT * N_HEADS, b, s, HEAD_W)
    g4 = g.reshape(N_HEADS, b, s, HEAD_W)

    o_diff = _diff_attn(proj4, vt, lam_q1[l][None, :], lam_k1[l][None, :], lam_q2[l][None, :],
                        lam_k2[l][None, :], diff_subln_w[l][:, None])
    o_hgrn = _hgrn2(proj4, g4, hgrn_norm_w[l][None, :])

    out = _out_ffn(x2d, o_diff.reshape(N_HEADS, t, HEAD_W), o_hgrn.reshape(N_HEADS, t, HEAD_W),
                   w_out_b, ln2_w[l][None, :], w_up_b, conv_w[l], conv_b[l][None, :], w_down_b, s)
    return out.reshape(b, s, d)
```

```python
import functools
import math

import numpy as np
import jax
import jax.numpy as jnp
from jax import lax
from jax.experimental import pallas as pl
from jax.experimental.pallas import tpu as pltpu

F32 = jnp.float32
BF16 = jnp.bfloat16

D_MODEL = 1024
CHUNK = 64
HEAD_W = 128
DIFF_HEAD_DIM = 64
N_HEADS = 4
SEC_W = N_HEADS * HEAD_W
N_SEC = 7
IN_COLS = N_SEC * SEC_W
D_FF = 2816
CONV_WIDTH = 3
EPS = 1e-6
LAM_INIT = 0.8 - 0.6 * math.exp(-0.3 * 0)
LOG2E = math.log2(math.e)

SEC_DQ, SEC_DK, SEC_DV, SEC_HQ, SEC_HK, SEC_HI, SEC_HG = range(N_SEC)
SLOT_DQ, SLOT_DK, SLOT_HQ, SLOT_HK, SLOT_HI, SLOT_HG = range(6)
N_SLOT = 6
BF16_SUBLANES = 16
V_ROWS = HEAD_W + BF16_SUBLANES

LANES = 128
SUBLANES = 8
MXU_W = 256
TM_PROJ = 1024
ATT_HEADS = 4
TQ = 1024
TK = 256
N_SUB = TQ // TK
HQ = TK
TM_FFN = 512
FF_CHUNK = 768
VMEM_LIMIT = 56 * 1024 * 1024

HGRN_CHUNK = 64
N_LEVELS = 6
N_COARSE = 3
HGRN_GROUP = 4


def _nt_dot(a, b):
    return lax.dot_general(a, b, (((1,), (1,)), ((), ())), preferred_element_type=F32)


def _tn_dot(a, b):
    return lax.dot_general(a, b, (((0,), (0,)), ((), ())), preferred_element_type=F32)


def _dot(a, b):
    return jnp.dot(a, b, preferred_element_type=F32)


def _split_bf16(x):
    hi = x.astype(BF16)
    lo = (x - hi.astype(F32)).astype(BF16)
    return hi, lo


def _sigmoid(x):
    return 1.0 / (1.0 + jnp.exp(-x))


def _in_proj_kernel(n_later, x_ref, ln1_ref, w_ref, qw_ref, kw_ref, lbl_ref, *rest):
    later_f32, (proj_ref, g_ref, vt_ref), later_bf16 = (
        rest[:n_later], rest[n_later:n_later + 3], rest[n_later + 3:])
    for src, dst in zip(later_f32, later_bf16):
        dst[...] = src[...].astype(BF16)

    x = x_ref[...]
    ms = jnp.mean(x * x, axis=-1, keepdims=True)
    h = (x * lax.rsqrt(ms + EPS) * ln1_ref[...]).astype(BF16)

    r = lax.broadcasted_iota(jnp.int32, (MXU_W, MXU_W), 0) // DIFF_HEAD_DIM
    c = lax.broadcasted_iota(jnp.int32, (MXU_W, MXU_W), 1) // DIFF_HEAD_DIM
    grp = jnp.where(r == c, 1.0, 0.0).astype(BF16)

    def head_cols(a, hd):
        return a[:, hd * HEAD_W:(hd + 1) * HEAD_W]

    def store_heads(slot, a):
        for hd in range(N_HEADS):
            proj_ref[slot * N_HEADS + hd] = head_cols(a, hd).astype(BF16)

    def project(j):
        return _dot(h, w_ref[:, j * SEC_W:(j + 1) * SEC_W])

    acc = [None] * N_SEC
    halves = [slice(c0, c0 + MXU_W) for c0 in range(0, SEC_W, MXU_W)]
    ss = {}
    for sec in (SEC_DQ, SEC_DK):
        acc[sec] = project(sec)
        ss[sec] = [_dot((acc[sec][:, cols] * acc[sec][:, cols]).astype(BF16), grp)
                   for cols in halves]
    for sec in (SEC_HK, SEC_HG, SEC_DV, SEC_HQ, SEC_HI):
        acc[sec] = project(sec)

    q_gain = qw_ref[...] * (DIFF_HEAD_DIM ** -0.5 * LOG2E)
    for sec, slot, w_norm in ((SEC_DQ, SLOT_DQ, q_gain), (SEC_DK, SLOT_DK, kw_ref[...])):
        w2 = jnp.concatenate([w_norm] * (MXU_W // HEAD_W), axis=1)
        y = [acc[sec][:, cols] * lax.rsqrt(ss[sec][n] * (1.0 / DIFF_HEAD_DIM) + EPS) * w2
             for n, cols in enumerate(halves)]
        store_heads(slot, jnp.concatenate(y, axis=1))

    for hd in range(N_HEADS):
        vt_ref[hd, 0:HEAD_W, :] = head_cols(acc[SEC_DV], hd).T.astype(BF16)
        vt_ref[hd, HEAD_W:V_ROWS, :] = jnp.ones((V_ROWS - HEAD_W, x.shape[0]), BF16)

    store_heads(SLOT_HQ, acc[SEC_HQ])
    store_heads(SLOT_HI, acc[SEC_HI])

    lbl = lbl_ref[...]
    e = jnp.exp(lbl - jnp.max(lbl, axis=0, keepdims=True))
    lb = e[0:1] / jnp.sum(e, axis=0, keepdims=True)
    f = lb + (1.0 - lb) * _sigmoid(acc[SEC_HK])
    log2_f = jnp.log(f) * LOG2E
    for hd in range(N_HEADS):
        g_ref[hd] = head_cols(log2_f, hd)
    store_heads(SLOT_HK, 1.0 - f)

    store_heads(SLOT_HG, acc[SEC_HG] * _sigmoid(acc[SEC_HG]))


def _in_proj(x2d, ln1_w, w_in, q_norm_w, k_norm_w, lb_logits, later_weights):
    t = x2d.shape[0]
    steps = t // TM_PROJ
    const = lambda *_: (0, 0)
    sliced = [w.reshape(steps, w.shape[0] // steps, w.shape[1]) for w in later_weights]
    slice_specs = [pl.BlockSpec((1,) + w.shape[1:], lambda i: (i, 0, 0)) for w in sliced]
    outs = pl.pallas_call(
        functools.partial(_in_proj_kernel, len(sliced)),
        grid=(steps,),
        in_specs=[
            pl.BlockSpec((TM_PROJ, D_MODEL), lambda i: (i, 0)),
            pl.BlockSpec((1, D_MODEL), const),
            pl.BlockSpec((D_MODEL, IN_COLS), const, pipeline_mode=pl.Buffered(1)),
            pl.BlockSpec((1, HEAD_W), const),
            pl.BlockSpec((1, HEAD_W), const),
            pl.BlockSpec(lb_logits.shape, const),
        ] + slice_specs,
        out_specs=[
            pl.BlockSpec((N_SLOT * N_HEADS, TM_PROJ, HEAD_W), lambda i: (0, i, 0)),
            pl.BlockSpec((N_HEADS, TM_PROJ, HEAD_W), lambda i: (0, i, 0)),
            pl.BlockSpec((N_HEADS, V_ROWS, TM_PROJ), lambda i: (0, 0, i)),
        ] + slice_specs,
        out_shape=[
            jax.ShapeDtypeStruct((N_SLOT * N_HEADS, t, HEAD_W), BF16),
            jax.ShapeDtypeStruct((N_HEADS, t, HEAD_W), F32),
            jax.ShapeDtypeStruct((N_HEADS, V_ROWS, t), BF16),
        ] + [jax.ShapeDtypeStruct(w.shape, BF16) for w in sliced],
        compiler_params=pltpu.CompilerParams(
            dimension_semantics=("parallel",), vmem_limit_bytes=VMEM_LIMIT),
        name="in_proj",
    )(x2d, ln1_w, w_in, q_norm_w, k_norm_w, lb_logits, *sliced)
    rounded = [o.reshape(w.shape) for o, w in zip(outs[3:], later_weights)]
    return outs[0], outs[1], outs[2], rounded


NEG_BIG = -1e30


def _diff_attn_kernel(q_ref, k_ref, vt_ref, lq1_ref, lk1_ref, lq2_ref, lk2_ref, sw_ref,
                      o_ref, qs_ref, s_ref, m_ref, acc_ref):
    n_hd = q_ref.shape[0]
    heads = range(n_hd)
    lane = lax.broadcasted_iota(jnp.int32, (HQ, HEAD_W), 1)

    def stack_queries(qb):
        for hd in heads:
            for grp in range(N_SUB):
                r0 = qb * TQ + grp * HQ
                q = q_ref[hd, 0, r0:r0 + HQ, :]
                zero = jnp.zeros_like(q)
                qs_ref[hd, (2 * grp) * HQ:(2 * grp + 1) * HQ] = jnp.where(lane < DIFF_HEAD_DIM, q, zero)
                qs_ref[hd, (2 * grp + 1) * HQ:(2 * grp + 2) * HQ] = jnp.where(lane >= DIFF_HEAD_DIM, q, zero)

    ck = lax.broadcasted_iota(jnp.int32, (TK, 2 * TQ), 0) // CHUNK
    col = lax.broadcasted_iota(jnp.int32, (TK, 2 * TQ), 1)
    diag_mask = ck <= jnp.where(col >= 2 * HQ, TK // CHUNK, (col % HQ) // CHUNK)

    def kv_rows(j):
        return slice(j * TK, (j + 1) * TK)

    def score_step(j, slot, cols=slice(None)):
        rows = kv_rows(j)
        for hd in heads:
            s_ref[slot, hd, :, cols] = _nt_dot(k_ref[hd, 0, rows, :], qs_ref[hd, cols])

    def softmax_step(j, slot, cols=slice(None), mask=None, first=False):
        rows = kv_rows(j)
        p_all, alpha_all = [], []
        for hd in heads:
            s = s_ref[slot, hd, :, cols]
            if mask is not None:
                s = jnp.where(mask, s, NEG_BIG)
            m_new = jnp.max(s, axis=0, keepdims=True)
            if not first:
                m_old = m_ref[hd, :, cols]
                m_new = jnp.maximum(m_old, m_new)
                alpha_all.append(jnp.exp2(m_old - m_new))
            m_ref[hd, :, cols] = m_new
            p_all.append(jnp.exp2((s - m_new).astype(BF16)))
        pv_all = [_dot(vt_ref[hd, :, rows], p_all[hd]) for hd in heads]
        for hd in heads:
            if first:
                acc_ref[hd, :, cols] = pv_all[hd]
            else:
                acc_ref[hd, :, cols] = alpha_all[hd] * acc_ref[hd, :, cols] + pv_all[hd]

    def run(q):
        blocks = [(j, slice(None), None) for j in range(N_SUB * q)]
        blocks += [(N_SUB * q + r, slice(2 * HQ * r, 2 * TQ), diag_mask[:, 0:2 * TQ - 2 * HQ * r])
                   for r in range(N_SUB)]
        score_step(blocks[0][0], 0, blocks[0][1])
        for n, (j, cols, mask) in enumerate(blocks):
            if n + 1 < len(blocks):
                score_step(blocks[n + 1][0], (n + 1) % 2, blocks[n + 1][1])
            softmax_step(j, n % 2, cols, mask, first=(n == 0))

    lam = (jnp.exp(jnp.sum(lq1_ref[...] * lk1_ref[...], axis=-1, keepdims=True))
           - jnp.exp(jnp.sum(lq2_ref[...] * lk2_ref[...], axis=-1, keepdims=True)) + LAM_INIT)
    gain = sw_ref[...] * (1.0 - LAM_INIT)

    def finish(qb):
        for hd in heads:
            o = acc_ref[hd, 0:HEAD_W] / acc_ref[hd, HEAD_W:HEAD_W + 1]
            for grp in range(N_SUB):
                c0 = 2 * grp * HQ
                r0 = qb * TQ + grp * HQ
                d = o[:, c0:c0 + HQ] - lam * o[:, c0 + HQ:c0 + 2 * HQ]
                ms = jnp.mean(d * d, axis=0, keepdims=True)
                o_ref[hd, 0, r0:r0 + HQ, :] = (d * lax.rsqrt(ms + EPS) * gain).T.astype(BF16)

    for qb in range(k_ref.shape[2] // TQ):
        stack_queries(qb)
        run(qb)
        finish(qb)


def _diff_attn(proj4, vt, lam_q1, lam_k1, lam_q2, lam_k2, subln_w):
    _, b, s, _ = proj4.shape
    groups = N_HEADS // ATT_HEADS
    const = lambda *_: (0, 0)
    lam_spec = pl.BlockSpec((1, DIFF_HEAD_DIM), const)
    return pl.pallas_call(
        _diff_attn_kernel,
        grid=(b, groups),
        in_specs=[
            pl.BlockSpec((ATT_HEADS, 1, s, HEAD_W),
                         lambda bi, gi: (SLOT_DQ * groups + gi, bi, 0, 0)),
            pl.BlockSpec((ATT_HEADS, 1, s, HEAD_W),
                         lambda bi, gi: (SLOT_DK * groups + gi, bi, 0, 0)),
            pl.BlockSpec((ATT_HEADS, V_ROWS, s), lambda bi, gi: (gi, 0, bi)),
            lam_spec, lam_spec, lam_spec, lam_spec,
            pl.BlockSpec((HEAD_W, 1), const),
        ],
        out_specs=pl.BlockSpec((ATT_HEADS, 1, s, HEAD_W), lambda bi, gi: (gi, bi, 0, 0)),
        out_shape=jax.ShapeDtypeStruct((N_HEADS, b, s, HEAD_W), BF16),
        scratch_shapes=[
            pltpu.VMEM((ATT_HEADS, 2 * TQ, HEAD_W), BF16),
            pltpu.VMEM((2, ATT_HEADS, TK, 2 * TQ), F32),
            pltpu.VMEM((ATT_HEADS, 1, 2 * TQ), F32),
            pltpu.VMEM((ATT_HEADS, V_ROWS, 2 * TQ), F32),
        ],
        compiler_params=pltpu.CompilerParams(
            dimension_semantics=("parallel", "parallel"),
            vmem_limit_bytes=VMEM_LIMIT),
        name="diff_attn",
    )(proj4, proj4, vt, lam_q1, lam_k1, lam_q2, lam_k2, subln_w)


def _hgrn_constants():
    n = HGRN_CHUNK
    t = np.arange(n)[:, None]
    s = np.arange(n)[None, :]
    ltri = (s <= t)
    masks = [(s == t)]
    fine, roles = [], []
    for lvl in range(N_LEVELS):
        hs = n >> (lvl + 1)
        blk = t // (2 * hs)
        mid = blk * 2 * hs + hs - 1
        is_q = (t % (2 * hs)) >= hs
        if lvl >= N_COARSE:
            fine.append(np.where(is_q, (s > mid) & (s <= t), (s > t) & (s <= mid)))
        masks.append((blk == (s // (2 * hs))) & is_q & ((s % (2 * hs)) < hs))
        roles.append(np.broadcast_to(np.where(is_q, 1.0, -1.0), (n, HEAD_W)))
    ltri = ltri.astype(np.float32)
    wfine = np.concatenate(fine, axis=0).astype(np.float32)
    masks = np.stack(masks).astype(np.float32)
    roles = np.stack(roles).astype(np.float32)
    return ltri, wfine, masks, roles


def _hgrn_kernel(q_ref, k_ref, v_ref, gate_ref, g_ref, ltri_ref, wfine_ref, mask_ref, role_ref,
                 nw_ref, o_ref, st_ref):
    n_hd = q_ref.shape[0]
    n_chunks = q_ref.shape[2] // HGRN_CHUNK
    st_ref[...] = jnp.zeros(st_ref.shape, F32)
    ltri = ltri_ref[...]
    wfine = wfine_ref[...]
    nw = nw_ref[...]

    def exponents(g):
        g_hi, g_lo = _split_bf16(g)
        return _dot(ltri, g_hi) + _dot(ltri, g_lo), _dot(wfine, g_hi)

    def decays(b, fine):
        def row_bcast(r, n):
            return jnp.broadcast_to(b[r:r + 1, :], (n, HEAD_W))

        e_b = jnp.exp2(b)
        e_u = jnp.exp2(row_bcast(HGRN_CHUNK - 1, HGRN_CHUNK) - b)
        z = []
        for lvl in range(N_COARSE):
            hs = HGRN_CHUNK >> (lvl + 1)
            b_mid = jnp.concatenate([row_bcast(blk * 2 * hs + hs - 1, 2 * hs)
                                     for blk in range(HGRN_CHUNK // (2 * hs))], axis=0)
            z.append(jnp.exp2((b - b_mid) * role_ref[lvl]))
        fine = jnp.exp2(fine)
        z += [fine[i * HGRN_CHUNK:(i + 1) * HGRN_CHUNK] for i in range(N_LEVELS - N_COARSE)]
        return e_b, e_u, z

    def group(gi, carry):
        base = gi * (HGRN_GROUP * HGRN_CHUNK)
        items = [(hd, pl.ds(pl.multiple_of(base + u * HGRN_CHUNK, HGRN_CHUNK), HGRN_CHUNK))
                 for u in range(HGRN_GROUP) for hd in range(n_hd)]
        q_b = [q_ref[hd, 0, rows, :] for hd, rows in items]
        k_b = [k_ref[hd, 0, rows, :] for hd, rows in items]
        v_b = [v_ref[hd, 0, rows, :] for hd, rows in items]
        expo = [exponents(g_ref[hd, 0, rows, :]) for hd, rows in items]
        dec = [decays(*e) for e in expo]

        def by_role(lvl, q, k):
            if lvl >= N_COARSE:
                return jnp.where(role_ref[lvl] > 0.0, q, k)
            hs = HGRN_CHUNK >> (lvl + 1)
            return jnp.concatenate([(q if part % 2 else k)[part * hs:(part + 1) * hs]
                                    for part in range(HGRN_CHUNK // hs)], axis=0)

        xz, q_dec, k_dec = [], [], []
        for n, (e_b, e_u, z) in enumerate(dec):
            q = q_b[n].astype(F32)
            k = k_b[n].astype(F32)
            xz.append([(by_role(lvl, q, k) * z[lvl]).astype(BF16) for lvl in range(N_LEVELS)])
            q_dec.append((q * e_b).astype(BF16))
            k_dec.append((k * e_u).astype(BF16))

        pair = [[_nt_dot(q_b[n], k_b[n])] + [_nt_dot(x, x) for x in xz[n]]
                for n in range(len(items))]
        kv = [_tn_dot(v_b[n], k_dec[n]) for n in range(len(items))]
        owns = [mask_ref[lvl] > 0.5 for lvl in range(N_LEVELS + 1)]
        att = []
        for p in pair:
            a = jnp.where(owns[0], p[0], 0.0)
            for lvl in range(1, N_LEVELS + 1):
                a = jnp.where(owns[lvl], p[lvl], a)
            att.append(a.astype(BF16))
        o = [_dot(att[n], v_b[n]) for n in range(len(items))]

        st = [st_ref[hd] for hd in range(n_hd)]
        for n, (hd, rows) in enumerate(items):
            o[n] = o[n] + _nt_dot(q_dec[n], st[hd].astype(BF16))
            st[hd] = st[hd] * dec[n][0][HGRN_CHUNK - 1:HGRN_CHUNK, :] + kv[n]
        for hd in range(n_hd):
            st_ref[hd] = st[hd]

        for n, (hd, rows) in enumerate(items):
            ms = jnp.mean(o[n] * o[n], axis=-1, keepdims=True)
            y = o[n] * lax.rsqrt(ms + EPS) * nw * gate_ref[hd, 0, rows, :].astype(F32)
            o_ref[hd, 0, rows, :] = y.astype(BF16)
        return carry

    lax.fori_loop(0, n_chunks // HGRN_GROUP, group, 0, unroll=True)


def _hgrn2(proj4, g4, norm_w):
    _, b, s, _ = proj4.shape
    ltri, wfine, masks, roles = _hgrn_constants()
    heads_spec = lambda sec: pl.BlockSpec((N_HEADS, 1, s, HEAD_W), lambda bi: (sec, bi, 0, 0))
    return pl.pallas_call(
        _hgrn_kernel,
        grid=(b,),
        in_specs=[
            heads_spec(SLOT_HQ), heads_spec(SLOT_HK), heads_spec(SLOT_HI), heads_spec(SLOT_HG),
            heads_spec(0),
            pl.BlockSpec(ltri.shape, lambda *_: (0, 0)),
            pl.BlockSpec(wfine.shape, lambda *_: (0, 0)),
            pl.BlockSpec(masks.shape, lambda *_: (0, 0, 0)),
            pl.BlockSpec(roles.shape, lambda *_: (0, 0, 0)),
            pl.BlockSpec((1, HEAD_W), lambda *_: (0, 0)),
        ],
        out_specs=heads_spec(0),
        out_shape=jax.ShapeDtypeStruct((N_HEADS, b, s, HEAD_W), BF16),
        scratch_shapes=[pltpu.VMEM((N_HEADS, HEAD_W, HEAD_W), F32)],
        compiler_params=pltpu.CompilerParams(
            dimension_semantics=("parallel",), vmem_limit_bytes=VMEM_LIMIT),
        name="hgrn2",
    )(proj4, proj4, proj4, proj4, g4, jnp.asarray(ltri, BF16), jnp.asarray(wfine, BF16),
      jnp.asarray(masks), jnp.asarray(roles), norm_w)


def _ff_chunks():
    chunks, c0 = [], 0
    while c0 < D_FF:
        fc = min(FF_CHUNK, D_FF - c0)
        chunks.append((c0, fc))
        c0 += fc
    return chunks


def _out_ffn_kernel(tiles_per_seq, x_ref, od_ref, oh_ref, wout_ref, ln2_ref, wup_ref, cw_ref,
                    cb_ref, wdn_ref, out_ref, tail_ref, act_ref):
    i = pl.program_id(0)
    mix = jnp.concatenate([od_ref[hd] for hd in range(N_HEADS)]
                          + [oh_ref[hd] for hd in range(N_HEADS)], axis=1)
    x1 = x_ref[...] + _dot(mix, wout_ref[...])
    ms = jnp.mean(x1 * x1, axis=-1, keepdims=True)
    h2 = (x1 * lax.rsqrt(ms + EPS) * ln2_ref[...]).astype(BF16)

    seq_start = (i % tiles_per_seq) == 0
    tm = x1.shape[0]
    for c0, fc in _ff_chunks():
        cols = slice(c0, c0 + fc)
        u = _dot(h2, wup_ref[:, cols])
        v = _dot(h2, wup_ref[:, D_FF + c0:D_FF + c0 + fc])
        tail = jnp.where(seq_start, 0.0, tail_ref[:, cols])
        tail_ref[:, cols] = u[tm - SUBLANES:, :]
        ext = jnp.concatenate([tail, u], axis=0)
        u1 = pltpu.roll(ext, 1, 0)[SUBLANES:]
        u2 = pltpu.roll(ext, 2, 0)[SUBLANES:]
        cw = cw_ref[:, cols]
        c = cb_ref[:, cols] + u2 * cw[0:1] + u1 * cw[1:2] + u * cw[2:3]
        act_ref[:, cols] = (c * _sigmoid(c) * v).astype(BF16)
    out_ref[...] = x1 + _dot(act_ref[...], wdn_ref[...])


def _out_ffn(x2d, o_diff, o_hgrn, w_out, ln2_w, w_up, conv_w, conv_b, w_down, seq_len):
    t = x2d.shape[0]
    const = lambda *_: (0, 0)
    resident = functools.partial(pl.BlockSpec, index_map=const, pipeline_mode=pl.Buffered(1))
    return pl.pallas_call(
        functools.partial(_out_ffn_kernel, seq_len // TM_FFN),
        grid=(t // TM_FFN,),
        in_specs=[
            pl.BlockSpec((TM_FFN, D_MODEL), lambda i: (i, 0)),
            pl.BlockSpec((N_HEADS, TM_FFN, HEAD_W), lambda i: (0, i, 0)),
            pl.BlockSpec((N_HEADS, TM_FFN, HEAD_W), lambda i: (0, i, 0)),
            resident((D_MODEL, D_MODEL)),
            pl.BlockSpec((1, D_MODEL), const),
            resident((D_MODEL, 2 * D_FF)),
            pl.BlockSpec((CONV_WIDTH, D_FF), const),
            pl.BlockSpec((1, D_FF), const),
            resident((D_FF, D_MODEL)),
        ],
        out_specs=pl.BlockSpec((TM_FFN, D_MODEL), lambda i: (i, 0)),
        out_shape=jax.ShapeDtypeStruct((t, D_MODEL), F32),
        scratch_shapes=[
            pltpu.VMEM((SUBLANES, D_FF), F32),
            pltpu.VMEM((TM_FFN, D_FF), BF16),
        ],
        compiler_params=pltpu.CompilerParams(
            dimension_semantics=("arbitrary",), vmem_limit_bytes=VMEM_LIMIT),
        name="out_ffn",
    )(x2d, o_diff, o_hgrn, w_out, ln2_w, w_up, conv_w, conv_b, w_down)


def kernel(x, ln1_w, w_in, q_norm_w, k_norm_w, lam_q1, lam_k1, lam_q2, lam_k2, diff_subln_w,
           hgrn_lb_logits, hgrn_norm_w, w_out, ln2_w, w_up, conv_w, conv_b, w_down):
    b, s, d = x.shape
    depth = ln1_w.shape[0]
    assert depth == 1 and d == D_MODEL and s % TM_FFN == 0 and s % TQ == 0 and N_SUB % 2 == 0
    t = b * s
    x2d = x.reshape(t, d)
    l = 0
    tile2 = lambda w: jnp.concatenate([w, w], axis=-1)[None, :]

    proj, g, vt, (w_out_b, w_up_b, w_down_b) = _in_proj(
        x2d, ln1_w[l][None, :], w_in[l].astype(BF16), tile2(q_norm_w[l]), tile2(k_norm_w[l]),
        hgrn_lb_logits, [w_out[l], w_up[l], w_down[l]])
    proj4 = proj.reshape(N_SLOT * N_HEADS, b, s, HEAD_W)
    g4 = g.reshape(N_HEADS, b, s, HEAD_W)

    o_diff = _diff_attn(proj4, vt, lam_q1[l][None, :], lam_k1[l][None, :], lam_q2[l][None, :],
                        lam_k2[l][None, :], diff_subln_w[l][:, None])
    o_hgrn = _hgrn2(proj4, g4, hgrn_norm_w[l][None, :])

    out = _out_ffn(x2d, o_diff.reshape(N_HEADS, t, HEAD_W), o_hgrn.reshape(N_HEADS, t, HEAD_W),
                   w_out_b, ln2_w[l][None, :], w_up_b, conv_w[l], conv_b[l][None, :], w_down_b, s)
    return out.reshape(b, s, d)
```

```python
import functools
import math

import numpy as np
import jax
import jax.numpy as jnp
from jax import lax
from jax.experimental import pallas as pl
from jax.experimental.pallas import tpu as pltpu

F32 = jnp.float32
BF16 = jnp.bfloat16

D_MODEL = 1024
CHUNK = 64
HEAD_W = 128
DIFF_HEAD_DIM = 64
N_HEADS = 4
SEC_W = N_HEADS * HEAD_W
N_SEC = 7
IN_COLS = N_SEC * SEC_W
D_FF = 2816
CONV_WIDTH = 3
EPS = 1e-6
LAM_INIT = 0.8 - 0.6 * math.exp(-0.3 * 0)
LOG2E = math.log2(math.e)

SEC_DQ, SEC_DK, SEC_DV, SEC_HQ, SEC_HK, SEC_HI, SEC_HG = range(N_SEC)
SLOT_DQ, SLOT_DK, SLOT_HQ, SLOT_HK, SLOT_HI, SLOT_HG = range(6)
N_SLOT = 6
BF16_SUBLANES = 16
V_ROWS = HEAD_W + BF16_SUBLANES

LANES = 128
SUBLANES = 8
MXU_W = 256
TM_PROJ = 1024
ATT_HEADS = 4
TQ = 1024
TK = 256
N_SUB = TQ // TK
HQ = TK
TM_FFN = 512
FF_CHUNK = 768
VMEM_LIMIT = 56 * 1024 * 1024

HGRN_CHUNK = 64
N_LEVELS = 6
N_COARSE = 3
HGRN_GROUP = 4


def _nt_dot(a, b):
    return lax.dot_general(a, b, (((1,), (1,)), ((), ())), preferred_element_type=F32)


def _tn_dot(a, b):
    return lax.dot_general(a, b, (((0,), (0,)), ((), ())), preferred_element_type=F32)


def _dot(a, b):
    return jnp.dot(a, b, preferred_element_type=F32)


def _split_bf16(x):
    hi = x.astype(BF16)
    lo = (x - hi.astype(F32)).astype(BF16)
    return hi, lo


def _sigmoid(x):
    return 1.0 / (1.0 + jnp.exp(-x))


def _in_proj_kernel(n_later, x_ref, ln1_ref, w_ref, qw_ref, kw_ref, lbl_ref, *rest):
    later_f32, (proj_ref, g_ref, vt_ref), later_bf16 = (
        rest[:n_later], rest[n_later:n_later + 3], rest[n_later + 3:])
    for src, dst in zip(later_f32, later_bf16):
        dst[...] = src[...].astype(BF16)

    x = x_ref[...]
    ms = jnp.mean(x * x, axis=-1, keepdims=True)
    h = (x * lax.rsqrt(ms + EPS) * ln1_ref[...]).astype(BF16)

    r = lax.broadcasted_iota(jnp.int32, (MXU_W, MXU_W), 0) // DIFF_HEAD_DIM
    c = lax.broadcasted_iota(jnp.int32, (MXU_W, MXU_W), 1) // DIFF_HEAD_DIM
    grp = jnp.where(r == c, 1.0, 0.0).astype(BF16)

    def head_cols(a, hd):
        return a[:, hd * HEAD_W:(hd + 1) * HEAD_W]

    def store_heads(slot, a):
        for hd in range(N_HEADS):
            proj_ref[slot * N_HEADS + hd] = head_cols(a, hd).astype(BF16)

    def project(j):
        return _dot(h, w_ref[:, j * SEC_W:(j + 1) * SEC_W])

    acc = [None] * N_SEC
    halves = [slice(c0, c0 + MXU_W) for c0 in range(0, SEC_W, MXU_W)]
    ss = {}
    for sec in (SEC_DQ, SEC_DK):
        acc[sec] = project(sec)
        ss[sec] = [_dot((acc[sec][:, cols] * acc[sec][:, cols]).astype(BF16), grp)
                   for cols in halves]
    for sec in (SEC_HK, SEC_HG, SEC_DV, SEC_HQ, SEC_HI):
        acc[sec] = project(sec)

    q_gain = qw_ref[...] * (DIFF_HEAD_DIM ** -0.5 * LOG2E)
    for sec, slot, w_norm in ((SEC_DQ, SLOT_DQ, q_gain), (SEC_DK, SLOT_DK, kw_ref[...])):
        w2 = jnp.concatenate([w_norm] * (MXU_W // HEAD_W), axis=1)
        y = [acc[sec][:, cols] * lax.rsqrt(ss[sec][n] * (1.0 / DIFF_HEAD_DIM) + EPS) * w2
             for n, cols in enumerate(halves)]
        store_heads(slot, jnp.concatenate(y, axis=1))

    for hd in range(N_HEADS):
        vt_ref[hd, 0:HEAD_W, :] = head_cols(acc[SEC_DV], hd).T.astype(BF16)
        vt_ref[hd, HEAD_W:V_ROWS, :] = jnp.ones((V_ROWS - HEAD_W, x.shape[0]), BF16)

    store_heads(SLOT_HQ, acc[SEC_HQ])
    store_heads(SLOT_HI, acc[SEC_HI])

    lbl = lbl_ref[...]
    e = jnp.exp(lbl - jnp.max(lbl, axis=0, keepdims=True))
    lb = e[0:1] / jnp.sum(e, axis=0, keepdims=True)
    f = lb + (1.0 - lb) * _sigmoid(acc[SEC_HK])
    log2_f = jnp.log(f) * LOG2E
    for hd in range(N_HEADS):
        g_ref[hd] = head_cols(log2_f, hd)
    store_heads(SLOT_HK, 1.0 - f)

    store_heads(SLOT_HG, acc[SEC_HG] * _sigmoid(acc[SEC_HG]))


def _in_proj(x2d, ln1_w, w_in, q_norm_w, k_norm_w, lb_logits, later_weights):
    t = x2d.shape[0]
    steps = t // TM_PROJ
    const = lambda *_: (0, 0)
    sliced = [w.reshape(steps, w.shape[0] // steps, w.shape[1]) for w in later_weights]
    slice_specs = [pl.BlockSpec((1,) + w.shape[1:], lambda i: (i, 0, 0)) for w in sliced]
    outs = pl.pallas_call(
        functools.partial(_in_proj_kernel, len(sliced)),
        grid=(steps,),
        in_specs=[
            pl.BlockSpec((TM_PROJ, D_MODEL), lambda i: (i, 0)),
            pl.BlockSpec((1, D_MODEL), const),
            pl.BlockSpec((D_MODEL, IN_COLS), const, pipeline_mode=pl.Buffered(1)),
            pl.BlockSpec((1, HEAD_W), const),
            pl.BlockSpec((1, HEAD_W), const),
            pl.BlockSpec(lb_logits.shape, const),
        ] + slice_specs,
        out_specs=[
            pl.BlockSpec((N_SLOT * N_HEADS, TM_PROJ, HEAD_W), lambda i: (0, i, 0)),
            pl.BlockSpec((N_HEADS, TM_PROJ, HEAD_W), lambda i: (0, i, 0)),
            pl.BlockSpec((N_HEADS, V_ROWS, TM_PROJ), lambda i: (0, 0, i)),
        ] + slice_specs,
        out_shape=[
            jax.ShapeDtypeStruct((N_SLOT * N_HEADS, t, HEAD_W), BF16),
            jax.ShapeDtypeStruct((N_HEADS, t, HEAD_W), F32),
            jax.ShapeDtypeStruct((N_HEADS, V_ROWS, t), BF16),
        ] + [jax.ShapeDtypeStruct(w.shape, BF16) for w in sliced],
        compiler_params=pltpu.CompilerParams(
            dimension_semantics=("parallel",), vmem_limit_bytes=VMEM_LIMIT),
        name="in_proj",
    )(x2d, ln1_w, w_in, q_norm_w, k_norm_w, lb_logits, *sliced)
    rounded = [o.reshape(w.shape) for o, w in zip(outs[3:], later_weights)]
    return outs[0], outs[1], outs[2], rounded


NEG_BIG = -1e30


def _diff_attn_kernel(q_ref, k_ref, vt_ref, lq1_ref, lk1_ref, lq2_ref, lk2_ref, sw_ref,
                      o_ref, qs_ref, s_ref, m_ref, acc_ref):
    qi = pl.program_id(2)
    n_hd = q_ref.shape[0]
    heads = range(n_hd)
    lane = lax.broadcasted_iota(jnp.int32, (HQ, HEAD_W), 1)
    for hd in heads:
        for grp in range(N_SUB):
            q = q_ref[hd, 0, grp * HQ:(grp + 1) * HQ, :]
            zero = jnp.zeros_like(q)
            qs_ref[hd, (2 * grp) * HQ:(2 * grp + 1) * HQ] = jnp.where(lane < DIFF_HEAD_DIM, q, zero)
            qs_ref[hd, (2 * grp + 1) * HQ:(2 * grp + 2) * HQ] = jnp.where(lane >= DIFF_HEAD_DIM, q, zero)

    ck = lax.broadcasted_iota(jnp.int32, (TK, 2 * TQ), 0) // CHUNK
    col = lax.broadcasted_iota(jnp.int32, (TK, 2 * TQ), 1)
    diag_mask = ck <= jnp.where(col >= 2 * HQ, TK // CHUNK, (col % HQ) // CHUNK)

    def kv_rows(j):
        return slice(j * TK, (j + 1) * TK)

    def score_step(j, slot, cols=slice(None)):
        rows = kv_rows(j)
        for hd in heads:
            s_ref[slot, hd, :, cols] = _nt_dot(k_ref[hd, 0, rows, :], qs_ref[hd, cols])

    def softmax_step(j, slot, cols=slice(None), mask=None, first=False):
        rows = kv_rows(j)
        p_all, alpha_all = [], []
        for hd in heads:
            s = s_ref[slot, hd, :, cols]
            if mask is not None:
                s = jnp.where(mask, s, NEG_BIG)
            m_new = jnp.max(s, axis=0, keepdims=True)
            if not first:
                m_old = m_ref[hd, :, cols]
                m_new = jnp.maximum(m_old, m_new)
                alpha_all.append(jnp.exp2(m_old - m_new))
            m_ref[hd, :, cols] = m_new
            p_all.append(jnp.exp2((s - m_new).astype(BF16)))
        pv_all = [_dot(vt_ref[hd, :, rows], p_all[hd]) for hd in heads]
        for hd in heads:
            if first:
                acc_ref[hd, :, cols] = pv_all[hd]
            else:
                acc_ref[hd, :, cols] = alpha_all[hd] * acc_ref[hd, :, cols] + pv_all[hd]

    def run(q):
        blocks = [(j, slice(None), None) for j in range(N_SUB * q)]
        blocks += [(N_SUB * q + r, slice(2 * HQ * r, 2 * TQ), diag_mask[:, 0:2 * TQ - 2 * HQ * r])
                   for r in range(N_SUB)]
        score_step(blocks[0][0], 0, blocks[0][1])
        for n, (j, cols, mask) in enumerate(blocks):
            if n + 1 < len(blocks):
                score_step(blocks[n + 1][0], (n + 1) % 2, blocks[n + 1][1])
            softmax_step(j, n % 2, cols, mask, first=(n == 0))

    for q in range(k_ref.shape[2] // TQ):
        pl.when(qi == q)(functools.partial(run, q))

    lam = (jnp.exp(jnp.sum(lq1_ref[...] * lk1_ref[...], axis=-1, keepdims=True))
           - jnp.exp(jnp.sum(lq2_ref[...] * lk2_ref[...], axis=-1, keepdims=True)) + LAM_INIT)
    gain = sw_ref[...] * (1.0 - LAM_INIT)
    for hd in heads:
        o = acc_ref[hd, 0:HEAD_W] / acc_ref[hd, HEAD_W:HEAD_W + 1]
        for grp in range(N_SUB):
            c0 = 2 * grp * HQ
            d = o[:, c0:c0 + HQ] - lam * o[:, c0 + HQ:c0 + 2 * HQ]
            ms = jnp.mean(d * d, axis=0, keepdims=True)
            o_ref[hd, 0, grp * HQ:(grp + 1) * HQ, :] = (
                d * lax.rsqrt(ms + EPS) * gain).T.astype(BF16)


def _diff_attn(proj4, vt, lam_q1, lam_k1, lam_q2, lam_k2, subln_w):
    _, b, s, _ = proj4.shape
    groups = N_HEADS // ATT_HEADS
    const = lambda *_: (0, 0)
    lam_spec = pl.BlockSpec((1, DIFF_HEAD_DIM), const)
    return pl.pallas_call(
        _diff_attn_kernel,
        grid=(b, groups, s // TQ),
        in_specs=[
            pl.BlockSpec((ATT_HEADS, 1, TQ, HEAD_W),
                         lambda bi, gi, qi: (SLOT_DQ * groups + gi, bi, qi, 0)),
            pl.BlockSpec((ATT_HEADS, 1, s, HEAD_W),
                         lambda bi, gi, qi: (SLOT_DK * groups + gi, bi, 0, 0)),
            pl.BlockSpec((ATT_HEADS, V_ROWS, s), lambda bi, gi, qi: (gi, 0, bi)),
            lam_spec, lam_spec, lam_spec, lam_spec,
            pl.BlockSpec((HEAD_W, 1), const),
        ],
        out_specs=pl.BlockSpec((ATT_HEADS, 1, TQ, HEAD_W), lambda bi, gi, qi: (gi, bi, qi, 0)),
        out_shape=jax.ShapeDtypeStruct((N_HEADS, b, s, HEAD_W), BF16),
        scratch_shapes=[
            pltpu.VMEM((ATT_HEADS, 2 * TQ, HEAD_W), BF16),
            pltpu.VMEM((2, ATT_HEADS, TK, 2 * TQ), F32),
            pltpu.VMEM((ATT_HEADS, 1, 2 * TQ), F32),
            pltpu.VMEM((ATT_HEADS, V_ROWS, 2 * TQ), F32),
        ],
        compiler_params=pltpu.CompilerParams(
            dimension_semantics=("parallel", "parallel", "parallel"),
            vmem_limit_bytes=VMEM_LIMIT),
        name="diff_attn",
    )(proj4, proj4, vt, lam_q1, lam_k1, lam_q2, lam_k2, subln_w)


def _hgrn_constants():
    n = HGRN_CHUNK
    t = np.arange(n)[:, None]
    s = np.arange(n)[None, :]
    ltri = (s <= t)
    masks = [(s == t)]
    fine, roles = [], []
    for lvl in range(N_LEVELS):
        hs = n >> (lvl + 1)
        blk = t // (2 * hs)
        mid = blk * 2 * hs + hs - 1
        is_q = (t % (2 * hs)) >= hs
        if lvl >= N_COARSE:
            fine.append(np.where(is_q, (s > mid) & (s <= t), (s > t) & (s <= mid)))
        masks.append((blk == (s // (2 * hs))) & is_q & ((s % (2 * hs)) < hs))
        roles.append(np.broadcast_to(np.where(is_q, 1.0, -1.0), (n, HEAD_W)))
    ltri = ltri.astype(np.float32)
    wfine = np.concatenate(fine, axis=0).astype(np.float32)
    masks = np.stack(masks).astype(np.float32)
    roles = np.stack(roles).astype(np.float32)
    return ltri, wfine, masks, roles


def _hgrn_kernel(q_ref, k_ref, v_ref, gate_ref, g_ref, ltri_ref, wfine_ref, mask_ref, role_ref,
                 nw_ref, o_ref, st_ref):
    n_hd = q_ref.shape[0]
    n_chunks = q_ref.shape[2] // HGRN_CHUNK
    st_ref[...] = jnp.zeros(st_ref.shape, F32)
    ltri = ltri_ref[...]
    wfine = wfine_ref[...]
    nw = nw_ref[...]

    def exponents(g):
        g_hi, g_lo = _split_bf16(g)
        return _dot(ltri, g_hi) + _dot(ltri, g_lo), _dot(wfine, g_hi)

    def decays(b, fine):
        def row_bcast(r, n):
            return jnp.broadcast_to(b[r:r + 1, :], (n, HEAD_W))

        e_b = jnp.exp2(b)
        e_u = jnp.exp2(row_bcast(HGRN_CHUNK - 1, HGRN_CHUNK) - b)
        z = []
        for lvl in range(N_COARSE):
            hs = HGRN_CHUNK >> (lvl + 1)
            b_mid = jnp.concatenate([row_bcast(blk * 2 * hs + hs - 1, 2 * hs)
                                     for blk in range(HGRN_CHUNK // (2 * hs))], axis=0)
            z.append(jnp.exp2((b - b_mid) * role_ref[lvl]))
        fine = jnp.exp2(fine)
        z += [fine[i * HGRN_CHUNK:(i + 1) * HGRN_CHUNK] for i in range(N_LEVELS - N_COARSE)]
        return e_b, e_u, z

    def group(gi, carry):
        base = gi * (HGRN_GROUP * HGRN_CHUNK)
        items = [(hd, slice(base + u * HGRN_CHUNK, base + (u + 1) * HGRN_CHUNK))
                 for u in range(HGRN_GROUP) for hd in range(n_hd)]
        q_b = [q_ref[hd, 0, rows, :] for hd, rows in items]
        k_b = [k_ref[hd, 0, rows, :] for hd, rows in items]
        v_b = [v_ref[hd, 0, rows, :] for hd, rows in items]
        expo = [exponents(g_ref[hd, 0, rows, :]) for hd, rows in items]
        dec = [decays(*e) for e in expo]

        def by_role(lvl, q, k):
            if lvl >= N_COARSE:
                return jnp.where(role_ref[lvl] > 0.0, q, k)
            hs = HGRN_CHUNK >> (lvl + 1)
            return jnp.concatenate([(q if part % 2 else k)[part * hs:(part + 1) * hs]
                                    for part in range(HGRN_CHUNK // hs)], axis=0)

        xz, q_dec, k_dec = [], [], []
        for n, (e_b, e_u, z) in enumerate(dec):
            q = q_b[n].astype(F32)
            k = k_b[n].astype(F32)
            xz.append([(by_role(lvl, q, k) * z[lvl]).astype(BF16) for lvl in range(N_LEVELS)])
            q_dec.append((q * e_b).astype(BF16))
            k_dec.append((k * e_u).astype(BF16))

        pair = [[_nt_dot(q_b[n], k_b[n])] + [_nt_dot(x, x) for x in xz[n]]
                for n in range(len(items))]
        kv = [_tn_dot(v_b[n], k_dec[n]) for n in range(len(items))]
        owns = [mask_ref[lvl] > 0.5 for lvl in range(N_LEVELS + 1)]
        att = []
        for p in pair:
            a = jnp.where(owns[0], p[0], 0.0)
            for lvl in range(1, N_LEVELS + 1):
                a = jnp.where(owns[lvl], p[lvl], a)
            att.append(a.astype(BF16))
        o = [_dot(att[n], v_b[n]) for n in range(len(items))]

        st = [st_ref[hd] for hd in range(n_hd)]
        for n, (hd, rows) in enumerate(items):
            if gi == 0 and n < n_hd:
                st[hd] = kv[n]
                continue
            o[n] = o[n] + _nt_dot(q_dec[n], st[hd].astype(BF16))
            st[hd] = st[hd] * dec[n][0][HGRN_CHUNK - 1:HGRN_CHUNK, :] + kv[n]
        for hd in range(n_hd):
            st_ref[hd] = st[hd]

        for n, (hd, rows) in enumerate(items):
            ms = jnp.mean(o[n] * o[n], axis=-1, keepdims=True)
            y = o[n] * lax.rsqrt(ms + EPS) * nw * gate_ref[hd, 0, rows, :].astype(F32)
            o_ref[hd, 0, rows, :] = y.astype(BF16)
        return carry

    for gi in range(n_chunks // HGRN_GROUP):
        group(gi, 0)


def _hgrn2(proj4, g4, norm_w):
    _, b, s, _ = proj4.shape
    ltri, wfine, masks, roles = _hgrn_constants()
    heads_spec = lambda sec: pl.BlockSpec((N_HEADS, 1, s, HEAD_W), lambda bi: (sec, bi, 0, 0))
    return pl.pallas_call(
        _hgrn_kernel,
        grid=(b,),
        in_specs=[
            heads_spec(SLOT_HQ), heads_spec(SLOT_HK), heads_spec(SLOT_HI), heads_spec(SLOT_HG),
            heads_spec(0),
            pl.BlockSpec(ltri.shape, lambda *_: (0, 0)),
            pl.BlockSpec(wfine.shape, lambda *_: (0, 0)),
            pl.BlockSpec(masks.shape, lambda *_: (0, 0, 0)),
            pl.BlockSpec(roles.shape, lambda *_: (0, 0, 0)),
            pl.BlockSpec((1, HEAD_W), lambda *_: (0, 0)),
        ],
        out_specs=heads_spec(0),
        out_shape=jax.ShapeDtypeStruct((N_HEADS, b, s, HEAD_W), BF16),
        scratch_shapes=[pltpu.VMEM((N_HEADS, HEAD_W, HEAD_W), F32)],
        compiler_params=pltpu.CompilerParams(
            dimension_semantics=("parallel",), vmem_limit_bytes=VMEM_LIMIT),
        name="hgrn2",
    )(proj4, proj4, proj4, proj4, g4, jnp.asarray(ltri, BF16), jnp.asarray(wfine, BF16),
      jnp.asarray(masks), jnp.asarray(roles), norm_w)


def _ff_chunks():
    chunks, c0 = [], 0
    while c0 < D_FF:
        fc = min(FF_CHUNK, D_FF - c0)
        chunks.append((c0, fc))
        c0 += fc
    return chunks


def _out_ffn_kernel(tiles_per_seq, x_ref, od_ref, oh_ref, wout_ref, ln2_ref, wup_ref, cw_ref,
                    cb_ref, wdn_ref, out_ref, tail_ref, act_ref):
    i = pl.program_id(0)
    mix = jnp.concatenate([od_ref[hd] for hd in range(N_HEADS)]
                          + [oh_ref[hd] for hd in range(N_HEADS)], axis=1)
    x1 = x_ref[...] + _dot(mix, wout_ref[...])
    ms = jnp.mean(x1 * x1, axis=-1, keepdims=True)
    h2 = (x1 * lax.rsqrt(ms + EPS) * ln2_ref[...]).astype(BF16)

    seq_start = (i % tiles_per_seq) == 0
    tm = x1.shape[0]
    for c0, fc in _ff_chunks():
        cols = slice(c0, c0 + fc)
        u = _dot(h2, wup_ref[:, cols])
        v = _dot(h2, wup_ref[:, D_FF + c0:D_FF + c0 + fc])
        tail = jnp.where(seq_start, 0.0, tail_ref[:, cols])
        tail_ref[:, cols] = u[tm - SUBLANES:, :]
        ext = jnp.concatenate([tail, u], axis=0)
        u1 = pltpu.roll(ext, 1, 0)[SUBLANES:]
        u2 = pltpu.roll(ext, 2, 0)[SUBLANES:]
        cw = cw_ref[:, cols]
        c = cb_ref[:, cols] + u2 * cw[0:1] + u1 * cw[1:2] + u * cw[2:3]
        act_ref[:, cols] = (c * _sigmoid(c) * v).astype(BF16)
    out_ref[...] = x1 + _dot(act_ref[...], wdn_ref[...])


def _out_ffn(x2d, o_diff, o_hgrn, w_out, ln2_w, w_up, conv_w, conv_b, w_down, seq_len):
    t = x2d.shape[0]
    const = lambda *_: (0, 0)
    resident = functools.partial(pl.BlockSpec, index_map=const, pipeline_mode=pl.Buffered(1))
    return pl.pallas_call(
        functools.partial(_out_ffn_kernel, seq_len // TM_FFN),
        grid=(t // TM_FFN,),
        in_specs=[
            pl.BlockSpec((TM_FFN, D_MODEL), lambda i: (i, 0)),
            pl.BlockSpec((N_HEADS, TM_FFN, HEAD_W), lambda i: (0, i, 0)),
            pl.BlockSpec((N_HEADS, TM_FFN, HEAD_W), lambda i: (0, i, 0)),
            resident((D_MODEL, D_MODEL)),
            pl.BlockSpec((1, D_MODEL), const),
            resident((D_MODEL, 2 * D_FF)),
            pl.BlockSpec((CONV_WIDTH, D_FF), const),
            pl.BlockSpec((1, D_FF), const),
            resident((D_FF, D_MODEL)),
        ],
        out_specs=pl.BlockSpec((TM_FFN, D_MODEL), lambda i: (i, 0)),
        out_shape=jax.ShapeDtypeStruct((t, D_MODEL), F32),
        scratch_shapes=[
            pltpu.VMEM((SUBLANES, D_FF), F32),
            pltpu.VMEM((TM_FFN, D_FF), BF16),
        ],
        compiler_params=pltpu.CompilerParams(
            dimension_semantics=("arbitrary",), vmem_limit_bytes=VMEM_LIMIT),
        name="out_ffn",
    )(x2d, o_diff, o_hgrn, w_out, ln2_w, w_up, conv_w, conv_b, w_down)


def kernel(x, ln1_w, w_in, q_norm_w, k_norm_w, lam_q1, lam_k1, lam_q2, lam_k2, diff_subln_w,
           hgrn_lb_logits, hgrn_norm_w, w_out, ln2_w, w_up, conv_w, conv_b, w_down):
    b, s, d = x.shape
    depth = ln1_w.shape[0]
    assert depth == 1 and d == D_MODEL and s % TM_FFN == 0 and s % TQ == 0 and N_SUB % 2 == 0
    t = b * s
    x2d = x.reshape(t, d)
    l = 0
    tile2 = lambda w: jnp.concatenate([w, w], axis=-1)[None, :]

    proj, g, vt, (w_out_b, w_up_b, w_down_b) = _in_proj(
        x2d, ln1_w[l][None, :], w_in[l].astype(BF16), tile2(q_norm_w[l]), tile2(k_norm_w[l]),
        hgrn_lb_logits, [w_out[l], w_up[l], w_down[l]])
    proj4 = proj.reshape(N_SLOT * N_HEADS, b, s, HEAD_W)
    g4 = g.reshape(N_HEADS, b, s, HEAD_W)

    o_diff = _diff_attn(proj4, vt, lam_q1[l][None, :], lam_k1[l][None, :], lam_q2[l][None, :],
                        lam_k2[l][None, :], diff_subln_w[l][:, None])
    o_hgrn = _hgrn2(proj4, g4, hgrn_norm_w[l][None, :])

    out = _out_ffn(x2d, o_diff.reshape(N_HEADS, t, HEAD_W), o_hgrn.reshape(N_HEADS, t, HEAD_W),
                   w_out_b, ln2_w[l][None, :], w_up_b, conv_w[l], conv_b[l][None, :], w_down_b, s)
    return out.reshape(b, s, d)
```
